```python
import math
import jax, jax.numpy as jnp
from jax import lax
import numpy as np

D_MODEL = 1024
BATCH = 8
SEQ = 2048
DEPTH = 4

HEAD_DIM = 64
MIX_WIDTH = D_MODEL
A_HEADS = MIX_WIDTH // 2 // HEAD_DIM
A_PATTERNS = ((128, 1), (512, 4), (2048, 16))
B_HEADS = MIX_WIDTH // 2 // (2 * HEAD_DIM)
C_HEADS = MIX_WIDTH // HEAD_DIM
C_KV_HEADS = C_HEADS // 4
C_HALF_WINDOW = 128
Q_BLOCK = 128
A_WIDTH = A_HEADS * HEAD_DIM
B_QK_WIDTH = B_HEADS * 2 * HEAD_DIM
B_V_WIDTH = B_HEADS * 2 * HEAD_DIM
EVEN_IN = 3 * A_WIDTH + 2 * B_QK_WIDTH + B_V_WIDTH + MIX_WIDTH
C_KV_WIDTH = C_KV_HEADS * HEAD_DIM
ODD_IN = MIX_WIDTH + 2 * C_KV_WIDTH + MIX_WIDTH
EPS = 1e-6
NEG_INF = -1e30

kernel_name = "hybrid_dilated_diff_swa_encoder"


def rms_norm(x, g):
    x32 = x.astype(jnp.float32)
    y = x32 * lax.rsqrt(jnp.mean(x32 * x32, axis=-1, keepdims=True) + EPS)
    return (y * g.astype(jnp.float32)).astype(x.dtype)


def alibi_slopes(n):
    return 2.0 ** (-8.0 * jnp.arange(1, n + 1, dtype=jnp.float32) / n)


def banded_attention(q, k, v, slopes, spacing, W):
    B, G, R, L, dh = q.shape
    nb = -(-L // W)
    Lp = nb * W
    pad = Lp - L
    qb = jnp.pad(q, ((0, 0), (0, 0), (0, 0), (0, pad), (0, 0))).reshape(B, G, R, nb, W, dh)

    def windows(t):
        tp = jnp.pad(t, ((0, 0), (0, 0), (W, W + pad), (0, 0))).reshape(B, G, nb + 2, W, t.shape[-1])
        return jnp.concatenate([tp[:, :, 0:nb], tp[:, :, 1:nb + 1], tp[:, :, 2:nb + 2]], axis=3)

    kw, vw = windows(k), windows(v)
    s = jnp.einsum('bgrnid,bgncd->bgrnic', qb, kw, preferred_element_type=jnp.float32)
    blk = jnp.arange(nb)[:, None, None]
    qpos = blk * W + jnp.arange(W)[None, :, None]
    kpos = blk * W - W + jnp.arange(3 * W)[None, None, :]
    rel = jnp.abs(kpos - qpos)
    valid = (rel <= W) & (kpos >= 0) & (kpos < L)
    dist = (rel * spacing).astype(jnp.float32)
    s = s - slopes.astype(jnp.float32)[None, :, :, None, None, None] * dist
    s = jnp.where(valid, s, NEG_INF)
    m = jnp.max(s, axis=-1)
    p = jnp.exp(s - m[..., None])
    l = jnp.sum(p, axis=-1)
    acc = jnp.einsum('bgrnic,bgncd->bgrnid', p, vw.astype(jnp.float32))
    m = m.reshape(B, G, R, Lp)[..., :L]
    l = l.reshape(B, G, R, Lp)[..., :L]
    acc = acc.reshape(B, G, R, Lp, -1)[:, :, :, :L]
    return m, l, acc


def dilated_mixture_attention(q, k, v):
    B, H, S, dh = q.shape
    q = q * (dh ** -0.5)
    slopes = alibi_slopes(H)
    ms, ls, accs = [], [], []
    for window, dil in A_PATTERNS:
        L = S // dil
        half = window // (2 * dil)

        def to_res(t):
            return t.reshape(B, H, L, dil, t.shape[-1]).transpose(0, 1, 3, 2, 4).reshape(B, H * dil, L, t.shape[-1])

        def from_res(t):
            t = t.reshape((B, H, dil, L) + t.shape[3:])
            t = jnp.swapaxes(t, 2, 3)
            return t.reshape((B, H, S) + t.shape[4:])

        m, l, acc = banded_attention(to_res(q)[:, :, None], to_res(k), to_res(v),
                                     jnp.repeat(slopes, dil)[:, None], dil, half)
        ms.append(from_res(m[:, :, 0]))
        ls.append(from_res(l[:, :, 0]))
        accs.append(from_res(acc[:, :, 0]))
    m = jnp.stack(ms)
    w = jnp.exp(m - jnp.max(m, axis=0))
    num = jnp.sum(w[..., None] * jnp.stack(accs), axis=0)
    den = jnp.sum(w * jnp.stack(ls), axis=0)
    return num / den[..., None]


def differential_attention(q1, q2, k1, k2, v, lam, lam_init, subln_g):
    B, H, S, dh = q1.shape
    scale = dh ** -0.5
    slopes = alibi_slopes(H)
    kpos = jnp.arange(S)
    v32 = v.astype(jnp.float32)

    def block(n):
        start = n * Q_BLOCK
        qpos = start + jnp.arange(Q_BLOCK)
        bias = -slopes[:, None, None] * jnp.abs(qpos[:, None] - kpos[None, :]).astype(jnp.float32)

        def attn(q, k):
            qs = lax.dynamic_slice_in_dim(q, start, Q_BLOCK, axis=2)
            s = jnp.einsum('bhqd,bhkd->bhqk', qs, k, preferred_element_type=jnp.float32) * scale + bias
            return jax.nn.softmax(s, axis=-1)

        a = attn(q1, k1) - lam * attn(q2, k2)
        return jnp.einsum('bhqk,bhkd->bhqd', a, v32)

    out = lax.map(block, jnp.arange(S // Q_BLOCK))
    out = jnp.moveaxis(out, 0, 2).reshape(B, H, S, 2 * dh)
    return rms_norm(out, subln_g) * (1.0 - lam_init)


def even_mixer(h, w_in, w_out, lq1, lk1, lq2, lk2, subln_g, lam_init):
    B, S, _ = h.shape
    proj = h @ w_in
    cuts = np.cumsum([A_WIDTH] * 3 + [B_QK_WIDTH] * 2 + [B_V_WIDTH]).tolist()
    qa, ka, va, qb, kb, vb, g = jnp.split(proj, cuts, axis=-1)

    def heads(t, n):
        return t.reshape(B, S, n, -1).transpose(0, 2, 1, 3)

    ya = dilated_mixture_attention(heads(qa, A_HEADS), heads(ka, A_HEADS), heads(va, A_HEADS))
    qb, kb, vb = heads(qb, B_HEADS), heads(kb, B_HEADS), heads(vb, B_HEADS)
    f32 = jnp.float32
    lam = (jnp.exp(jnp.sum(lq1.astype(f32) * lk1.astype(f32)))
           - jnp.exp(jnp.sum(lq2.astype(f32) * lk2.astype(f32))) + lam_init)
    yb = differential_attention(qb[..., :HEAD_DIM], qb[..., HEAD_DIM:],
                                kb[..., :HEAD_DIM], kb[..., HEAD_DIM:], vb, lam, lam_init, subln_g)
    y = jnp.concatenate([ya.transpose(0, 2, 1, 3).reshape(B, S, A_WIDTH),
                         yb.transpose(0, 2, 1, 3).reshape(B, S, B_V_WIDTH)], axis=-1).astype(h.dtype)
    return (y * jax.nn.silu(g)) @ w_out


def odd_mixer(h, w_in, w_out, sink):
    B, S, _ = h.shape
    R = C_HEADS // C_KV_HEADS
    proj = h @ w_in
    q, k, v, g = jnp.split(proj, [MIX_WIDTH, MIX_WIDTH + C_KV_WIDTH, MIX_WIDTH + 2 * C_KV_WIDTH], axis=-1)
    q = q.reshape(B, S, C_KV_HEADS, R, HEAD_DIM).transpose(0, 2, 3, 1, 4) * (HEAD_DIM ** -0.5)
    k = k.reshape(B, S, C_KV_HEADS, HEAD_DIM).transpose(0, 2, 1, 3)
    v = v.reshape(B, S, C_KV_HEADS, HEAD_DIM).transpose(0, 2, 1, 3)
    slopes = alibi_slopes(C_HEADS).reshape(C_KV_HEADS, R)
    m, l, acc = banded_attention(q, k, v, slopes, 1, C_HALF_WINDOW)
    sk = sink.astype(jnp.float32).reshape(C_KV_HEADS, R)[None, :, :, None]
    M = jnp.maximum(m, sk)
    e = jnp.exp(m - M)
    y = acc * e[..., None] / (l * e + jnp.exp(sk - M))[..., None]
    y = y.transpose(0, 3, 1, 2, 4).reshape(B, S, MIX_WIDTH).astype(h.dtype)
    return (y * jax.nn.silu(g)) @ w_out


def setup_inputs(seed: int = 0) -> dict:
    key = jax.random.key(seed)
    ks = jax.random.split(key, 16)
    n_even = (DEPTH + 1) // 2
    n_odd = DEPTH // 2

    def nrm(k, shape, scale):
        return jax.random.normal(k, shape, jnp.float32) * scale

    return {
        "x": nrm(ks[0], (BATCH, SEQ, D_MODEL), 1.0),
        "c": nrm(ks[1], (BATCH, D_MODEL), 1.0),
        "ada_w": nrm(ks[2], (DEPTH, D_MODEL, 3 * D_MODEL), 0.5 * D_MODEL ** -0.5),
        "ada_b": nrm(ks[3], (DEPTH, 3 * D_MODEL), 0.02),
        "norm_g": 1.0 + nrm(ks[4], (DEPTH, D_MODEL), 0.05),
        "ab_w_in": nrm(ks[5], (n_even, D_MODEL, EVEN_IN), D_MODEL ** -0.5),
        "ab_w_out": nrm(ks[6], (n_even, MIX_WIDTH, D_MODEL), MIX_WIDTH ** -0.5),
        "diff_lq1": nrm(ks[7], (n_even, HEAD_DIM), 0.1),
        "diff_lk1": nrm(ks[8], (n_even, HEAD_DIM), 0.1),
        "diff_lq2": nrm(ks[9], (n_even, HEAD_DIM), 0.1),
        "diff_lk2": nrm(ks[10], (n_even, HEAD_DIM), 0.1),
        "diff_subln_g": 1.0 + nrm(ks[11], (n_even, 2 * HEAD_DIM), 0.05),
        "c_w_in": nrm(ks[12], (n_odd, D_MODEL, ODD_IN), D_MODEL ** -0.5),
        "c_w_out": nrm(ks[13], (n_odd, MIX_WIDTH, D_MODEL), MIX_WIDTH ** -0.5),
        "c_sink": nrm(ks[14], (n_odd, C_HEADS), 1.0),
        "final_g": 1.0 + nrm(ks[15], (D_MODEL,), 0.05),
    }


def reference(x, c, ada_w, ada_b, norm_g, ab_w_in, ab_w_out, diff_lq1, diff_lk1,
              diff_lq2, diff_lk2, diff_subln_g, c_w_in, c_w_out, c_sink, final_g):
    cs = jax.nn.silu(c)
    for layer in range(DEPTH):
        mod = cs @ ada_w[layer] + ada_b[layer]
        shift, scale, gate = jnp.split(mod, 3, axis=-1)
        h = rms_norm(x, norm_g[layer]) * (1.0 + scale[:, None, :]) + shift[:, None, :]
        j = layer // 2
        if layer % 2 == 0:
            lam_init = 0.8 - 0.6 * math.exp(-0.3 * layer)
            y = even_mixer(h, ab_w_in[j], ab_w_out[j], diff_lq1[j], diff_lk1[j],
                           diff_lq2[j], diff_lk2[j], diff_subln_g[j], lam_init)
        else:
            y = odd_mixer(h, c_w_in[j], c_w_out[j], c_sink[j])
        x = x + gate[:, None, :] * y
    return rms_norm(x, final_g)
```

```python
import functools
import math

import numpy as np
import jax
import jax.numpy as jnp
from jax import lax
from jax.experimental import pallas as pl
from jax.experimental.pallas import tpu as pltpu

D_MODEL = 1024
BATCH = 8
SEQ = 2048
DEPTH = 4
HEAD_DIM = 64
LANES = 128
A_HEADS = 8
A_PATTERNS = ((128, 1), (512, 4), (2048, 16))
A_HALF = 64
B_HEADS = 4
C_HEADS = 16
C_KV_HEADS = 4
C_HALF_WINDOW = 128
EVEN_IN = 4096
ODD_IN = 2560
EPS = 1e-6
NEG_INF = -1e30
QK_SCALE = HEAD_DIM ** -0.5

F32 = jnp.float32
BF16 = jnp.bfloat16

ROW_TILE = 512
COL_TILE = 512
BLK = 128
A_KEYS = BLK + 2 * A_HALF
C_KEYS = BLK + 2 * C_HALF_WINDOW
B_TQ = 256
B_TK = 512
VMEM_LIMIT = 56 * 1024 * 1024


def _silu(t):
    return t * (1.0 / (1.0 + jnp.exp(-t)))


def _dot_nt(a, b):
    return lax.dot_general(a, b, (((1,), (1,)), ((), ())), preferred_element_type=F32)


def _alibi_slopes(n):
    return (2.0 ** (-8.0 * np.arange(1, n + 1, dtype=np.float32) / n)).astype(np.float32)


def _band_bias(slopes, spacing, half, tq, tk, offsets):
    i = np.arange(tq)[:, None]
    c = np.arange(tk)[None, :]
    out = np.empty((len(slopes), len(offsets), tq, tk), np.float32)
    for v, off in enumerate(offsets):
        rel = np.abs(c - (i + off))
        dist = (rel * spacing).astype(np.float32)
        for h, m in enumerate(slopes):
            out[h, v] = np.where(rel <= half, -m * dist, np.float32(NEG_INF))
    return out


def _mod_kernel(c_ref, w_ref, b_ref, o_ref):
    cs = _silu(c_ref[...])
    o_ref[0] = jnp.dot(cs, w_ref[0], preferred_element_type=F32,
                       precision=lax.Precision.HIGHEST) + b_ref[0]


def _modulation(c, ada_w, ada_b):
    nblk = 3 * D_MODEL // D_MODEL
    return pl.pallas_call(
        _mod_kernel,
        grid=(DEPTH, nblk),
        in_specs=[
            pl.BlockSpec((BATCH, D_MODEL), lambda l, j: (0, 0)),
            pl.BlockSpec((1, D_MODEL, D_MODEL), lambda l, j: (l, 0, j)),
            pl.BlockSpec((1, 1, D_MODEL), lambda l, j: (l, 0, j)),
        ],
        out_specs=pl.BlockSpec((1, BATCH, D_MODEL), lambda l, j: (l, 0, j)),
        out_shape=jax.ShapeDtypeStruct((DEPTH, BATCH, 3 * D_MODEL), F32),
        name="adaln_mod",
    )(c, ada_w, ada_b.reshape(DEPTH, 1, 3 * D_MODEL))


def _in_proj_kernel(x_ref, g_ref, sc_ref, sh_ref, w_ref, o_ref, h_ref, *, n_out):
    x = x_ref[0]
    ms = jnp.mean(x * x, axis=-1, keepdims=True)
    y = x * lax.rsqrt(ms + EPS) * g_ref[...]
    h_ref[...] = (y * (1.0 + sc_ref[0]) + sh_ref[0]).astype(BF16)
    for j in range(n_out // COL_TILE):
        cols = slice(j * COL_TILE, (j + 1) * COL_TILE)
        o_ref[0, :, cols] = jnp.dot(h_ref[...], w_ref[:, cols],
                                    preferred_element_type=F32).astype(BF16)


def _in_proj(x, g, scale, shift, w):
    n_out = w.shape[1]
    return pl.pallas_call(
        functools.partial(_in_proj_kernel, n_out=n_out),
        grid=(BATCH, SEQ // ROW_TILE),
        in_specs=[
            pl.BlockSpec((1, ROW_TILE, D_MODEL), lambda b, i: (b, i, 0)),
            pl.BlockSpec((1, D_MODEL), lambda b, i: (0, 0)),
            pl.BlockSpec((1, 1, D_MODEL), lambda b, i: (b, 0, 0)),
            pl.BlockSpec((1, 1, D_MODEL), lambda b, i: (b, 0, 0)),
            pl.BlockSpec((D_MODEL, n_out), lambda b, i: (0, 0)),
        ],
        out_specs=pl.BlockSpec((1, ROW_TILE, n_out), lambda b, i: (b, i, 0)),
        out_shape=jax.ShapeDtypeStruct((BATCH, SEQ, n_out), BF16),
        scratch_shapes=[pltpu.VMEM((ROW_TILE, D_MODEL), BF16)],
        compiler_params=pltpu.CompilerParams(
            dimension_semantics=("arbitrary", "arbitrary"), vmem_limit_bytes=VMEM_LIMIT),
        name="norm_in_proj",
    )(x, g.reshape(1, D_MODEL), scale.reshape(BATCH, 1, D_MODEL),
      shift.reshape(BATCH, 1, D_MODEL), w)


def _out_proj_kernel(*refs, n_z, final):
    x_ref, gate_ref = refs[0], refs[1]
    z_refs = refs[2:2 + n_z]
    w_ref = refs[2 + n_z]
    fg_ref = refs[3 + n_z] if final else None
    o_ref = refs[-1]
    kz = D_MODEL // n_z
    acc = jnp.dot(z_refs[0][0], w_ref[0:kz, :], preferred_element_type=F32)
    for i in range(1, n_z):
        acc = acc + jnp.dot(z_refs[i][0], w_ref[i * kz:(i + 1) * kz, :],
                            preferred_element_type=F32)
    xn = x_ref[0] + gate_ref[0] * acc
    if final:
        ms = jnp.mean(xn * xn, axis=-1, keepdims=True)
        xn = xn * lax.rsqrt(ms + EPS) * fg_ref[...]
    o_ref[0] = xn


def _out_proj(x, gate, zs, w, final_g=None):
    n_z = len(zs)
    kz = D_MODEL // n_z
    final = final_g is not None
    in_specs = [
        pl.BlockSpec((1, ROW_TILE, D_MODEL), lambda b, i: (b, i, 0)),
        pl.BlockSpec((1, 1, D_MODEL), lambda b, i: (b, 0, 0)),
    ]
    in_specs += [pl.BlockSpec((1, ROW_TILE, kz), lambda b, i: (b, i, 0)) for _ in zs]
    in_specs += [pl.BlockSpec((D_MODEL, D_MODEL), lambda b, i: (0, 0))]
    args = [x, gate.reshape(BATCH, 1, D_MODEL), *zs, w]
    if final:
        in_specs += [pl.BlockSpec((1, D_MODEL), lambda b, i: (0, 0))]
        args += [final_g.reshape(1, D_MODEL)]
    return pl.pallas_call(
        functools.partial(_out_proj_kernel, n_z=n_z, final=final),
        grid=(BATCH, SEQ // ROW_TILE),
        in_specs=in_specs,
        out_specs=pl.BlockSpec((1, ROW_TILE, D_MODEL), lambda b, i: (b, i, 0)),
        out_shape=jax.ShapeDtypeStruct((BATCH, SEQ, D_MODEL), F32),
        compiler_params=pltpu.CompilerParams(
            dimension_semantics=("arbitrary", "arbitrary"), vmem_limit_bytes=VMEM_LIMIT),
        name="out_proj_residual",
    )(*args)


def _lane_lo(rows):
    return lax.broadcasted_iota(jnp.int32, (rows, LANES), 1) < HEAD_DIM


def _softmax_block(qm, kw, vw, bias):
    s = _dot_nt(qm, kw) + bias
    m = jnp.max(s, axis=-1, keepdims=True)
    p = jnp.exp(s - m)
    l = jnp.sum(p, axis=-1, keepdims=True)
    acc = jnp.dot(p.astype(BF16), vw, preferred_element_type=F32)
    return acc, m, l


def _a_kernel(q_ref, k_ref, v_ref, g_ref, b12_ref, b3_ref, o_ref,
              qs_ref, qf_ref, kf_ref, vf_ref, qd_ref, kd_ref, vd_ref,
              acc_ref, m_ref, l_ref):
    lo = _lane_lo(BLK)
    zero = jnp.zeros((BLK, LANES), BF16)

    def pair_block(qb, kw, vw, bias0, bias1):
        a0, m0, l0 = _softmax_block(jnp.where(lo, qb, zero), kw, vw, bias0)
        a1, m1, l1 = _softmax_block(jnp.where(lo, zero, qb), kw, vw, bias1)
        return jnp.where(lo, a0, a1), jnp.where(lo, m0, m1), jnp.where(lo, l0, l1)

    chunk = 512
    for c in range(SEQ // chunk):
        rows = slice(c * chunk, (c + 1) * chunk)
        qs = q_ref[0, rows, :] * QK_SCALE
        qs_ref[rows, :] = qs
        qf_ref[rows, :] = qs.astype(F32)
        kf_ref[rows, :] = k_ref[0, rows, :].astype(F32)
        vf_ref[rows, :] = v_ref[0, rows, :].astype(F32)

    n_blk = SEQ // BLK

    def variant(blk, last):
        return jnp.where(blk == 0, 0, jnp.where(blk == last, 2, 1))

    def p0_body(j, carry):
        r0 = pl.multiple_of(j * BLK, BLK)
        ws = pl.multiple_of(jnp.clip(j * BLK - A_HALF, 0, SEQ - A_KEYS), A_HALF)
        var = variant(j, n_blk - 1)
        acc, m, l = pair_block(qs_ref[pl.ds(r0, BLK), :],
                               k_ref[0, pl.ds(ws, A_KEYS), :], v_ref[0, pl.ds(ws, A_KEYS), :],
                               b12_ref[0, 0, var], b12_ref[1, 0, var])
        acc_ref[0, pl.ds(r0, BLK), :] = acc
        m_ref[0, pl.ds(r0, BLK), :] = m
        l_ref[0, pl.ds(r0, BLK), :] = l
        return carry

    lax.fori_loop(0, n_blk, p0_body, 0)

    def deinterleave(dil):
        seg = SEQ // dil
        for r in range(dil):
            dst = slice(r * seg, (r + 1) * seg)
            qd_ref[dst, :] = qf_ref[pl.ds(r, seg, stride=dil), :].astype(BF16)
            kd_ref[dst, :] = kf_ref[pl.ds(r, seg, stride=dil), :].astype(BF16)
            vd_ref[dst, :] = vf_ref[pl.ds(r, seg, stride=dil), :].astype(BF16)

    dil1 = A_PATTERNS[1][1]
    seg1 = SEQ // dil1
    per1 = seg1 // BLK
    deinterleave(dil1)

    def p1_body(j, carry):
        r = j // per1
        blk = j % per1
        r0 = pl.multiple_of(j * BLK, BLK)
        ws = pl.multiple_of(r * seg1 + jnp.clip(blk * BLK - A_HALF, 0, seg1 - A_KEYS), A_HALF)
        var = variant(blk, per1 - 1)
        acc, m, l = pair_block(qd_ref[pl.ds(r0, BLK), :],
                               kd_ref[pl.ds(ws, A_KEYS), :], vd_ref[pl.ds(ws, A_KEYS), :],
                               b12_ref[0, 1, var], b12_ref[1, 1, var])
        dst = pl.ds(blk * BLK * dil1 + r, BLK, stride=dil1)
        acc_ref[1, dst, :] = acc
        m_ref[1, dst, :] = m
        l_ref[1, dst, :] = l
        return carry

    lax.fori_loop(0, n_blk, p1_body, 0)

    dil2 = A_PATTERNS[2][1]
    deinterleave(dil2)

    def p2_body(j, carry):
        r0 = pl.multiple_of(j * BLK, BLK)
        acc, m, l = pair_block(qd_ref[pl.ds(r0, BLK), :],
                               kd_ref[pl.ds(r0, BLK), :], vd_ref[pl.ds(r0, BLK), :],
                               b3_ref[0], b3_ref[1])
        dst = pl.ds(j, BLK, stride=dil2)
        acc_ref[2, dst, :] = acc
        m_ref[2, dst, :] = m
        l_ref[2, dst, :] = l
        return carry

    lax.fori_loop(0, n_blk, p2_body, 0)

    mrows = 256

    def merge_body(c, carry):
        rows = pl.ds(pl.multiple_of(c * mrows, mrows), mrows)
        m0, m1, m2 = m_ref[0, rows, :], m_ref[1, rows, :], m_ref[2, rows, :]
        mx = jnp.maximum(jnp.maximum(m0, m1), m2)
        w0, w1, w2 = jnp.exp(m0 - mx), jnp.exp(m1 - mx), jnp.exp(m2 - mx)
        num = w0 * acc_ref[0, rows, :] + w1 * acc_ref[1, rows, :] + w2 * acc_ref[2, rows, :]
        den = w0 * l_ref[0, rows, :] + w1 * l_ref[1, rows, :] + w2 * l_ref[2, rows, :]
        y = num / den
        o_ref[0, rows, :] = (y * _silu(g_ref[0, rows, :].astype(F32))).astype(BF16)
        return carry

    lax.fori_loop(0, SEQ // mrows, merge_body, 0)


def _mixer_a(proj, bias12, bias3):
    pairs = A_HEADS // 2
    blk = lambda off: pl.BlockSpec((1, SEQ, LANES), lambda hp, b: (b, 0, off + hp))
    return pl.pallas_call(
        _a_kernel,
        grid=(pairs, BATCH),
        in_specs=[
            blk(0), blk(pairs), blk(2 * pairs), blk(3072 // LANES),
            pl.BlockSpec((2, 2, 3, BLK, A_KEYS), lambda hp, b: (hp, 0, 0, 0, 0)),
            pl.BlockSpec((2, BLK, BLK), lambda hp, b: (hp, 0, 0)),
        ],
        out_specs=pl.BlockSpec((1, SEQ, LANES), lambda hp, b: (b, 0, hp)),
        out_shape=jax.ShapeDtypeStruct((BATCH, SEQ, A_HEADS * HEAD_DIM), BF16),
        scratch_shapes=[
            pltpu.VMEM((SEQ, LANES), BF16),
            pltpu.VMEM((SEQ, LANES), F32), pltpu.VMEM((SEQ, LANES), F32),
            pltpu.VMEM((SEQ, LANES), F32),
            pltpu.VMEM((SEQ, LANES), BF16), pltpu.VMEM((SEQ, LANES), BF16),
            pltpu.VMEM((SEQ, LANES), BF16),
            pltpu.VMEM((3, SEQ, LANES), F32), pltpu.VMEM((3, SEQ, LANES), F32),
            pltpu.VMEM((3, SEQ, LANES), F32),
        ],
        compiler_params=pltpu.CompilerParams(
            dimension_semantics=("arbitrary", "arbitrary"), vmem_limit_bytes=VMEM_LIMIT),
        name="mixer_a_dilated",
    )(proj, proj, proj, proj, bias12, bias3)


def _b_kernel(q_ref, k_ref, v_ref, g_ref, u_ref, lq1_ref, lk1_ref, lq2_ref, lk2_ref,
              sg_ref, o_ref, *, lam_init):
    i = pl.program_id(2)
    lo = _lane_lo(B_TQ)
    q = q_ref[0] * QK_SCALE
    zero = jnp.zeros_like(q)
    q1 = jnp.where(lo, q, zero)
    q2 = jnp.where(lo, zero, q)

    def update(s, m, l, acc, vv):
        m_new = jnp.maximum(m, jnp.max(s, axis=-1, keepdims=True))
        alpha = jnp.exp(m - m_new)
        p = jnp.exp(s - m_new)
        l_new = alpha * l + jnp.sum(p, axis=-1, keepdims=True)
        acc_new = alpha * acc + jnp.dot(p.astype(BF16), vv, preferred_element_type=F32)
        return m_new, l_new, acc_new

    def body(t, carry):
        m1, l1, a1, m2, l2, a2 = carry
        k0 = pl.multiple_of(t * B_TK, B_TK)
        kk = k_ref[0, pl.ds(k0, B_TK), :]
        vv = v_ref[0, pl.ds(k0, B_TK), :]
        off = pl.multiple_of(k0 - i * B_TQ + (SEQ - B_TQ), LANES)
        ub = u_ref[0, :, pl.ds(off, B_TK)]
        m1, l1, a1 = update(_dot_nt(q1, kk) + ub, m1, l1, a1, vv)
        m2, l2, a2 = update(_dot_nt(q2, kk) + ub, m2, l2, a2, vv)
        return m1, l1, a1, m2, l2, a2

    m0 = jnp.full((B_TQ, 1), NEG_INF, F32)
    l0 = jnp.zeros((B_TQ, 1), F32)
    a0 = jnp.zeros((B_TQ, LANES), F32)
    m1, l1, a1, m2, l2, a2 = lax.fori_loop(0, SEQ // B_TK, body, (m0, l0, a0, m0, l0, a0))

    lam = (jnp.exp(jnp.sum(lq1_ref[...] * lk1_ref[...], axis=-1, keepdims=True))
           - jnp.exp(jnp.sum(lq2_ref[...] * lk2_ref[...], axis=-1, keepdims=True)) + lam_init)
    out = a1 / l1 - lam * (a2 / l2)
    ms = jnp.mean(out * out, axis=-1, keepdims=True)
    y = out * lax.rsqrt(ms + EPS) * sg_ref[...] * (1.0 - lam_init)
    o_ref[0] = (y * _silu(g_ref[0].astype(F32))).astype(BF16)


def _mixer_b(proj, u, lq1, lk1, lq2, lk2, subln_g, lam_init):
    qb = 3 * A_HEADS * HEAD_DIM // LANES
    kb = qb + B_HEADS
    vb = kb + B_HEADS
    gb = 3072 // LANES + A_HEADS * HEAD_DIM // LANES
    vec = lambda n: pl.BlockSpec((1, n), lambda h, b, i: (0, 0))
    return pl.pallas_call(
        functools.partial(_b_kernel, lam_init=lam_init),
        grid=(B_HEADS, BATCH, SEQ // B_TQ),
        in_specs=[
            pl.BlockSpec((1, B_TQ, LANES), lambda h, b, i: (b, i, qb + h)),
            pl.BlockSpec((1, SEQ, LANES), lambda h, b, i: (b, 0, kb + h)),
            pl.BlockSpec((1, SEQ, LANES), lambda h, b, i: (b, 0, vb + h)),
            pl.BlockSpec((1, B_TQ, LANES), lambda h, b, i: (b, i, gb + h)),
            pl.BlockSpec((1, B_TQ, 2 * SEQ - B_TQ), lambda h, b, i: (h, 0, 0)),
            vec(HEAD_DIM), vec(HEAD_DIM), vec(HEAD_DIM), vec(HEAD_DIM), vec(2 * HEAD_DIM),
        ],
        out_specs=pl.BlockSpec((1, B_TQ, LANES), lambda h, b, i: (b, i, h)),
        out_shape=jax.ShapeDtypeStruct((BATCH, SEQ, B_HEADS * 2 * HEAD_DIM), BF16),
        compiler_params=pltpu.CompilerParams(
            dimension_semantics=("arbitrary", "arbitrary", "arbitrary"),
            vmem_limit_bytes=VMEM_LIMIT),
        name="mixer_b_differential",
    )(proj, proj, proj, proj, u, lq1.reshape(1, -1), lk1.reshape(1, -1),
      lq2.reshape(1, -1), lk2.reshape(1, -1), subln_g.reshape(1, -1))


def _c_kernel(sink_ref, q_ref, k_ref, v_ref, g_ref, bias_ref, o_ref, ka_ref, va_ref):
    grp = pl.program_id(0)
    half = grp % 2
    lo = _lane_lo(BLK)
    zero = jnp.zeros((BLK, LANES), BF16)
    row = lax.broadcasted_iota(jnp.int32, (LANES, LANES), 0)
    col = lax.broadcasted_iota(jnp.int32, (LANES, LANES), 1)

    chunk = 512
    for a in range(2):
        shift = jnp.where(half == a, 0, HEAD_DIM)
        perm = jnp.where(col == (row + shift) % LANES, 1.0, 0.0).astype(BF16)
        for c in range(SEQ // chunk):
            rows = slice(c * chunk, (c + 1) * chunk)
            ka_ref[a, rows, :] = jnp.dot(k_ref[0, rows, :], perm,
                                         preferred_element_type=F32).astype(BF16)
            va_ref[a, rows, :] = jnp.dot(v_ref[0, rows, :], perm,
                                         preferred_element_type=F32).astype(BF16)

    n_blk = SEQ // BLK
    rep = C_HEADS // C_KV_HEADS

    def body(n, carry):
        r0 = pl.multiple_of(n * BLK, BLK)
        ws = pl.multiple_of(jnp.clip(n * BLK - C_HALF_WINDOW, 0, SEQ - C_KEYS), BLK)
        var = jnp.where(n == 0, 0, jnp.where(n == n_blk - 1, 2, 1))
        ys = []
        for r in range(rep):
            a = r % 2
            cols = slice((r // 2) * LANES, (r // 2 + 1) * LANES)
            qb = q_ref[0, pl.ds(r0, BLK), cols] * QK_SCALE
            qm = jnp.where(lo, qb, zero) if a == 0 else jnp.where(lo, zero, qb)
            acc, m, l = _softmax_block(qm, ka_ref[a, pl.ds(ws, C_KEYS), :],
                                       va_ref[a, pl.ds(ws, C_KEYS), :], bias_ref[r, var])
            sk = sink_ref[grp * rep + r]
            mx = jnp.maximum(m, sk)
            e = jnp.exp(m - mx)
            ys.append(acc * e / (l * e + jnp.exp(sk - mx)))
        for c in range(rep // 2):
            cols = slice(c * LANES, (c + 1) * LANES)
            y = jnp.where(lo, ys[2 * c], ys[2 * c + 1])
            gate = _silu(g_ref[0, pl.ds(r0, BLK), cols].astype(F32))
            o_ref[0, pl.ds(r0, BLK), cols] = (y * gate).astype(BF16)
        return carry

    lax.fori_loop(0, n_blk, body, 0)


def _mixer_c(proj, sink, bias):
    width = C_HEADS // C_KV_HEADS * HEAD_DIM
    kb = C_HEADS * HEAD_DIM // LANES
    vb = kb + C_KV_HEADS * HEAD_DIM // LANES
    gb = (C_HEADS + 2 * C_KV_HEADS) * HEAD_DIM // width
    return pl.pallas_call(
        _c_kernel,
        grid=(C_KV_HEADS, BATCH),
        in_specs=[
            pl.BlockSpec(memory_space=pltpu.SMEM),
            pl.BlockSpec((1, SEQ, width), lambda g, b: (b, 0, g)),
            pl.BlockSpec((1, SEQ, LANES), lambda g, b: (b, 0, kb + g // 2)),
            pl.BlockSpec((1, SEQ, LANES), lambda g, b: (b, 0, vb + g // 2)),
            pl.BlockSpec((1, SEQ, width), lambda g, b: (b, 0, gb + g)),
            pl.BlockSpec((C_HEADS // C_KV_HEADS, 3, BLK, C_KEYS), lambda g, b: (g, 0, 0, 0)),
        ],
        out_specs=pl.BlockSpec((1, SEQ, width), lambda g, b: (b, 0, g)),
        out_shape=jax.ShapeDtypeStruct((BATCH, SEQ, C_HEADS * HEAD_DIM), BF16),
        scratch_shapes=[pltpu.VMEM((2, SEQ, LANES), BF16), pltpu.VMEM((2, SEQ, LANES), BF16)],
        compiler_params=pltpu.CompilerParams(
            dimension_semantics=("arbitrary", "arbitrary"), vmem_limit_bytes=VMEM_LIMIT),
        name="mixer_c_windowed",
    )(sink, proj, proj, proj, proj, bias)


@functools.lru_cache(maxsize=None)
def _bias_tables():
    sa = _alibi_slopes(A_HEADS)
    offs = (0, A_HALF, 2 * A_HALF)
    b12 = np.stack([_band_bias(sa, A_PATTERNS[0][1], A_HALF, BLK, A_KEYS, offs),
                    _band_bias(sa, A_PATTERNS[1][1], A_HALF, BLK, A_KEYS, offs)], axis=1)
    b3 = _band_bias(sa, A_PATTERNS[2][1], A_HALF, BLK, BLK, (0,))[:, 0]
    sc = _alibi_slopes(C_HEADS)
    bc = _band_bias(sc, 1, C_HALF_WINDOW, BLK, C_KEYS,
                    (0, C_HALF_WINDOW, 2 * C_HALF_WINDOW))
    sb = _alibi_slopes(B_HEADS)
    il = np.arange(B_TQ)[:, None]
    uu = np.arange(2 * SEQ - B_TQ)[None, :]
    dist = np.abs(il - uu + (SEQ - B_TQ)).astype(np.float32)
    ub = -sb[:, None, None] * dist[None]
    return b12, b3, bc, ub.astype(np.float32)


def kernel(x, c, ada_w, ada_b, norm_g, ab_w_in, ab_w_out, diff_lq1, diff_lk1, diff_lq2,
           diff_lk2, diff_subln_g, c_w_in, c_w_out, c_sink, final_g):
    b12, b3, bc, ub = (jnp.asarray(t) for t in _bias_tables())
    mod = _modulation(c, ada_w, ada_b)
    for layer in range(DEPTH):
        shift = mod[layer, :, :D_MODEL]
        scale = mod[layer, :, D_MODEL:2 * D_MODEL]
        gate = mod[layer, :, 2 * D_MODEL:]
        j = layer // 2
        final = final_g if layer == DEPTH - 1 else None
        if layer % 2 == 0:
            lam_init = 0.8 - 0.6 * math.exp(-0.3 * layer)
            proj = _in_proj(x, norm_g[layer], scale, shift, ab_w_in[j].astype(BF16))
            za = _mixer_a(proj, b12, b3)
            zb = _mixer_b(proj, ub, diff_lq1[j], diff_lk1[j], diff_lq2[j], diff_lk2[j],
                          diff_subln_g[j], lam_init)
            x = _out_proj(x, gate, [za, zb], ab_w_out[j].astype(BF16), final)
        else:
            proj = _in_proj(x, norm_g[layer], scale, shift, c_w_in[j].astype(BF16))
            zc = _mixer_c(proj, c_sink[j], bc)
            x = _out_proj(x, gate, [zc], c_w_out[j].astype(BF16), final)
    return x
```

```python
import functools
import math

import numpy as np
import jax
import jax.numpy as jnp
from jax import lax
from jax.experimental import pallas as pl
from jax.experimental.pallas import tpu as pltpu

D_MODEL = 1024
BATCH = 8
SEQ = 2048
DEPTH = 4
HEAD_DIM = 64
LANES = 128
A_HEADS = 8
A_PATTERNS = ((128, 1), (512, 4), (2048, 16))
A_HALF = 64
B_HEADS = 4
C_HEADS = 16
C_KV_HEADS = 4
C_HALF_WINDOW = 128
EVEN_IN = 4096
ODD_IN = 2560
EPS = 1e-6
NEG_INF = -1e30
QK_SCALE = HEAD_DIM ** -0.5

F32 = jnp.float32
BF16 = jnp.bfloat16

ROW_TILE = 512
COL_TILE = 512
BLK = 128
A_KEYS = BLK + 2 * A_HALF
C_KEYS = BLK + 2 * C_HALF_WINDOW
A_UNROLL = 8
C_UNROLL = 4
B_TQ = 256
B_TK = 512
VMEM_LIMIT = 56 * 1024 * 1024


def _silu(t):
    return t * (1.0 / (1.0 + jnp.exp(-t)))


def _dot_nt(a, b):
    return lax.dot_general(a, b, (((1,), (1,)), ((), ())), preferred_element_type=F32)


def _alibi_slopes(n):
    return (2.0 ** (-8.0 * np.arange(1, n + 1, dtype=np.float32) / n)).astype(np.float32)


def _band_bias(slopes, spacing, half, tq, tk, offsets):
    i = np.arange(tq)[:, None]
    c = np.arange(tk)[None, :]
    out = np.empty((len(slopes), len(offsets), tq, tk), np.float32)
    for v, off in enumerate(offsets):
        rel = np.abs(c - (i + off))
        dist = (rel * spacing).astype(np.float32)
        for h, m in enumerate(slopes):
            out[h, v] = np.where(rel <= half, -m * dist, np.float32(NEG_INF))
    return out


def _mod_kernel(c_ref, w_ref, b_ref, o_ref):
    cs = _silu(c_ref[...])
    o_ref[0] = jnp.dot(cs, w_ref[0], preferred_element_type=F32,
                       precision=lax.Precision.HIGHEST) + b_ref[0]


def _modulation(c, ada_w, ada_b):
    nblk = 3 * D_MODEL // D_MODEL
    return pl.pallas_call(
        _mod_kernel,
        grid=(DEPTH, nblk),
        in_specs=[
            pl.BlockSpec((BATCH, D_MODEL), lambda l, j: (0, 0)),
            pl.BlockSpec((1, D_MODEL, D_MODEL), lambda l, j: (l, 0, j)),
            pl.BlockSpec((1, 1, D_MODEL), lambda l, j: (l, 0, j)),
        ],
        out_specs=pl.BlockSpec((1, BATCH, D_MODEL), lambda l, j: (l, 0, j)),
        out_shape=jax.ShapeDtypeStruct((DEPTH, BATCH, 3 * D_MODEL), F32),
        name="adaln_mod",
    )(c, ada_w, ada_b.reshape(DEPTH, 1, 3 * D_MODEL))


def _in_proj_kernel(x_ref, g_ref, sc_ref, sh_ref, w_ref, o_ref, h_ref, *, n_out):
    x = x_ref[0]
    ms = jnp.mean(x * x, axis=-1, keepdims=True)
    y = x * lax.rsqrt(ms + EPS) * g_ref[...]
    h_ref[...] = (y * (1.0 + sc_ref[0]) + sh_ref[0]).astype(BF16)
    for j in range(n_out // COL_TILE):
        cols = slice(j * COL_TILE, (j + 1) * COL_TILE)
        o_ref[0, :, cols] = jnp.dot(h_ref[...], w_ref[:, cols],
                                    preferred_element_type=F32).astype(BF16)


def _in_proj(x, g, scale, shift, w):
    n_out = w.shape[1]
    return pl.pallas_call(
        functools.partial(_in_proj_kernel, n_out=n_out),
        grid=(BATCH, SEQ // ROW_TILE),
        in_specs=[
            pl.BlockSpec((1, ROW_TILE, D_MODEL), lambda b, i: (b, i, 0)),
            pl.BlockSpec((1, D_MODEL), lambda b, i: (0, 0)),
            pl.BlockSpec((1, 1, D_MODEL), lambda b, i: (b, 0, 0)),
            pl.BlockSpec((1, 1, D_MODEL), lambda b, i: (b, 0, 0)),
            pl.BlockSpec((D_MODEL, n_out), lambda b, i: (0, 0)),
        ],
        out_specs=pl.BlockSpec((1, ROW_TILE, n_out), lambda b, i: (b, i, 0)),
        out_shape=jax.ShapeDtypeStruct((BATCH, SEQ, n_out), BF16),
        scratch_shapes=[pltpu.VMEM((ROW_TILE, D_MODEL), BF16)],
        compiler_params=pltpu.CompilerParams(
            dimension_semantics=("arbitrary", "arbitrary"), vmem_limit_bytes=VMEM_LIMIT),
        name="norm_in_proj",
    )(x, g.reshape(1, D_MODEL), scale.reshape(BATCH, 1, D_MODEL),
      shift.reshape(BATCH, 1, D_MODEL), w)


def _out_proj_kernel(*refs, n_z, final):
    x_ref, gate_ref = refs[0], refs[1]
    z_refs = refs[2:2 + n_z]
    w_ref = refs[2 + n_z]
    fg_ref = refs[3 + n_z] if final else None
    o_ref = refs[-1]
    kz = D_MODEL // n_z
    acc = jnp.dot(z_refs[0][0], w_ref[0:kz, :], preferred_element_type=F32)
    for i in range(1, n_z):
        acc = acc + jnp.dot(z_refs[i][0], w_ref[i * kz:(i + 1) * kz, :],
                            preferred_element_type=F32)
    xn = x_ref[0] + gate_ref[0] * acc
    if final:
        ms = jnp.mean(xn * xn, axis=-1, keepdims=True)
        xn = xn * lax.rsqrt(ms + EPS) * fg_ref[...]
    o_ref[0] = xn


def _out_proj(x, gate, zs, w, final_g=None):
    n_z = len(zs)
    kz = D_MODEL // n_z
    final = final_g is not None
    in_specs = [
        pl.BlockSpec((1, ROW_TILE, D_MODEL), lambda b, i: (b, i, 0)),
        pl.BlockSpec((1, 1, D_MODEL), lambda b, i: (b, 0, 0)),
    ]
    in_specs += [pl.BlockSpec((1, ROW_TILE, kz), lambda b, i: (b, i, 0)) for _ in zs]
    in_specs += [pl.BlockSpec((D_MODEL, D_MODEL), lambda b, i: (0, 0))]
    args = [x, gate.reshape(BATCH, 1, D_MODEL), *zs, w]
    if final:
        in_specs += [pl.BlockSpec((1, D_MODEL), lambda b, i: (0, 0))]
        args += [final_g.reshape(1, D_MODEL)]
    return pl.pallas_call(
        functools.partial(_out_proj_kernel, n_z=n_z, final=final),
        grid=(BATCH, SEQ // ROW_TILE),
        in_specs=in_specs,
        out_specs=pl.BlockSpec((1, ROW_TILE, D_MODEL), lambda b, i: (b, i, 0)),
        out_shape=jax.ShapeDtypeStruct((BATCH, SEQ, D_MODEL), F32),
        compiler_params=pltpu.CompilerParams(
            dimension_semantics=("arbitrary", "arbitrary"), vmem_limit_bytes=VMEM_LIMIT),
        name="out_proj_residual",
    )(*args)


def _lane_lo(rows):
    return lax.broadcasted_iota(jnp.int32, (rows, LANES), 1) < HEAD_DIM


def _softmax_block(qm, kw, vw, bias):
    s = _dot_nt(qm, kw) + bias
    m = jnp.max(s, axis=-1, keepdims=True)
    p = jnp.exp(s - m)
    l = jnp.sum(p, axis=-1, keepdims=True)
    acc = jnp.dot(p.astype(BF16), vw, preferred_element_type=F32)
    return acc, m, l


def _stack_heads(qb, lo):
    zero = jnp.zeros_like(qb)
    return jnp.concatenate([jnp.where(lo, qb, zero), jnp.where(lo, zero, qb)], axis=0)


def _a_kernel(q_ref, k_ref, v_ref, g_ref, b12_ref, b3_ref, o_ref,
              qs_ref, qf_ref, kf_ref, vf_ref, qd_ref, kd_ref, vd_ref,
              acc_ref, m_ref, l_ref):
    lo = _lane_lo(BLK)

    def pair_block(qb, kw, vw, bias):
        acc, m, l = _softmax_block(_stack_heads(qb, lo), kw, vw, bias)
        return (jnp.where(lo, acc[:BLK], acc[BLK:]), jnp.where(lo, m[:BLK], m[BLK:]),
                jnp.where(lo, l[:BLK], l[BLK:]))

    chunk = 512
    for c in range(SEQ // chunk):
        rows = slice(c * chunk, (c + 1) * chunk)
        qs = q_ref[0, rows, :] * QK_SCALE
        qs_ref[rows, :] = qs
        qf_ref[rows, :] = qs.astype(F32)
        kf_ref[rows, :] = k_ref[0, rows, :].astype(F32)
        vf_ref[rows, :] = v_ref[0, rows, :].astype(F32)

    n_blk = SEQ // BLK

    def variant(blk, last):
        return jnp.where(blk == 0, 0, jnp.where(blk == last, 2, 1))

    def p0_body(j, carry):
        r0 = pl.multiple_of(j * BLK, BLK)
        ws = pl.multiple_of(jnp.clip(j * BLK - A_HALF, 0, SEQ - A_KEYS), A_HALF)
        var = variant(j, n_blk - 1)
        acc, m, l = pair_block(qs_ref[pl.ds(r0, BLK), :],
                               k_ref[0, pl.ds(ws, A_KEYS), :], v_ref[0, pl.ds(ws, A_KEYS), :],
                               b12_ref[0, 0, var])
        acc_ref[0, pl.ds(r0, BLK), :] = acc
        m_ref[0, pl.ds(r0, BLK), :] = m
        l_ref[0, pl.ds(r0, BLK), :] = l
        return carry

    lax.fori_loop(0, n_blk, p0_body, 0, unroll=A_UNROLL)

    def deinterleave(dil):
        seg = SEQ // dil
        for r in range(dil):
            dst = slice(r * seg, (r + 1) * seg)
            qd_ref[dst, :] = qf_ref[pl.ds(r, seg, stride=dil), :].astype(BF16)
            kd_ref[dst, :] = kf_ref[pl.ds(r, seg, stride=dil), :].astype(BF16)
            vd_ref[dst, :] = vf_ref[pl.ds(r, seg, stride=dil), :].astype(BF16)

    dil1 = A_PATTERNS[1][1]
    seg1 = SEQ // dil1
    per1 = seg1 // BLK
    deinterleave(dil1)

    def p1_body(j, carry):
        r = j // per1
        blk = j % per1
        r0 = pl.multiple_of(j * BLK, BLK)
        ws = pl.multiple_of(r * seg1 + jnp.clip(blk * BLK - A_HALF, 0, seg1 - A_KEYS), A_HALF)
        var = variant(blk, per1 - 1)
        acc, m, l = pair_block(qd_ref[pl.ds(r0, BLK), :],
                               kd_ref[pl.ds(ws, A_KEYS), :], vd_ref[pl.ds(ws, A_KEYS), :],
                               b12_ref[0, 1, var])
        dst = pl.ds(blk * BLK * dil1 + r, BLK, stride=dil1)
        acc_ref[1, dst, :] = acc
        m_ref[1, dst, :] = m
        l_ref[1, dst, :] = l
        return carry

    lax.fori_loop(0, n_blk, p1_body, 0, unroll=A_UNROLL)

    dil2 = A_PATTERNS[2][1]
    deinterleave(dil2)

    def p2_body(j, carry):
        r0 = pl.multiple_of(j * BLK, BLK)
        acc, m, l = pair_block(qd_ref[pl.ds(r0, BLK), :],
                               kd_ref[pl.ds(r0, BLK), :], vd_ref[pl.ds(r0, BLK), :],
                               b3_ref[0])
        dst = pl.ds(j, BLK, stride=dil2)
        acc_ref[2, dst, :] = acc
        m_ref[2, dst, :] = m
        l_ref[2, dst, :] = l
        return carry

    lax.fori_loop(0, n_blk, p2_body, 0, unroll=A_UNROLL)

    mrows = 256

    def merge_body(c, carry):
        rows = pl.ds(pl.multiple_of(c * mrows, mrows), mrows)
        m0, m1, m2 = m_ref[0, rows, :], m_ref[1, rows, :], m_ref[2, rows, :]
        mx = jnp.maximum(jnp.maximum(m0, m1), m2)
        w0, w1, w2 = jnp.exp(m0 - mx), jnp.exp(m1 - mx), jnp.exp(m2 - mx)
        num = w0 * acc_ref[0, rows, :] + w1 * acc_ref[1, rows, :] + w2 * acc_ref[2, rows, :]
        den = w0 * l_ref[0, rows, :] + w1 * l_ref[1, rows, :] + w2 * l_ref[2, rows, :]
        y = num / den
        o_ref[0, rows, :] = (y * _silu(g_ref[0, rows, :].astype(F32))).astype(BF16)
        return carry

    lax.fori_loop(0, SEQ // mrows, merge_body, 0)


def _mixer_a(proj, bias12, bias3):
    pairs = A_HEADS // 2
    blk = lambda off: pl.BlockSpec((1, SEQ, LANES), lambda hp, b: (b, 0, off + hp))
    return pl.pallas_call(
        _a_kernel,
        grid=(pairs, BATCH),
        in_specs=[
            blk(0), blk(pairs), blk(2 * pairs), blk(3072 // LANES),
            pl.BlockSpec((1, 2, 3, 2 * BLK, A_KEYS), lambda hp, b: (hp, 0, 0, 0, 0)),
            pl.BlockSpec((1, 2 * BLK, BLK), lambda hp, b: (hp, 0, 0)),
        ],
        out_specs=pl.BlockSpec((1, SEQ, LANES), lambda hp, b: (b, 0, hp)),
        out_shape=jax.ShapeDtypeStruct((BATCH, SEQ, A_HEADS * HEAD_DIM), BF16),
        scratch_shapes=[
            pltpu.VMEM((SEQ, LANES), BF16),
            pltpu.VMEM((SEQ, LANES), F32), pltpu.VMEM((SEQ, LANES), F32),
            pltpu.VMEM((SEQ, LANES), F32),
            pltpu.VMEM((SEQ, LANES), BF16), pltpu.VMEM((SEQ, LANES), BF16),
            pltpu.VMEM((SEQ, LANES), BF16),
            pltpu.VMEM((3, SEQ, LANES), F32), pltpu.VMEM((3, SEQ, LANES), F32),
            pltpu.VMEM((3, SEQ, LANES), F32),
        ],
        compiler_params=pltpu.CompilerParams(
            dimension_semantics=("arbitrary", "arbitrary"), vmem_limit_bytes=VMEM_LIMIT),
        name="mixer_a_dilated",
    )(proj, proj, proj, proj, bias12, bias3)


def _b_kernel(q_ref, k_ref, v_ref, g_ref, u_ref, lq1_ref, lk1_ref, lq2_ref, lk2_ref,
              sg_ref, o_ref, sa_ref, sb_ref, pma_ref, pmb_ref, *, lam_init):
    lo = _lane_lo(B_TQ)
    n_t = SEQ // B_TK
    n_q = SEQ // B_TQ
    lam = (jnp.exp(jnp.sum(lq1_ref[...] * lk1_ref[...], axis=-1, keepdims=True))
           - jnp.exp(jnp.sum(lq2_ref[...] * lk2_ref[...], axis=-1, keepdims=True)) + lam_init)

    def lane_fold(t, op):
        acc = t[:, :LANES]
        for c in range(1, t.shape[1] // LANES):
            acc = op(acc, t[:, c * LANES:(c + 1) * LANES])
        return acc

    def scores(i, s_ref, pm_ref):
        r0 = pl.multiple_of(i * B_TQ, B_TQ)
        qst = _stack_heads(q_ref[0, pl.ds(r0, B_TQ), :] * QK_SCALE, lo)
        pm = None
        for t in range(n_t):
            cols = slice(t * B_TK, (t + 1) * B_TK)
            off = pl.multiple_of(t * B_TK + (SEQ - B_TQ) - r0, LANES)
            ub = u_ref[0, :, pl.ds(off, B_TK)]
            s = _dot_nt(qst, k_ref[0, cols, :])
            s1 = s[:B_TQ] + ub
            s2 = s[B_TQ:] + ub
            s_ref[:B_TQ, cols] = s1
            s_ref[B_TQ:, cols] = s2
            f = jnp.concatenate([lane_fold(s1, jnp.maximum), lane_fold(s2, jnp.maximum)], axis=0)
            pm = f if pm is None else jnp.maximum(pm, f)
        pm_ref[...] = pm

    def finish(i, s_ref, pm_ref):
        r0 = pl.multiple_of(i * B_TQ, B_TQ)
        m = jnp.max(pm_ref[...], axis=-1, keepdims=True)
        ps = None
        acc = None
        for t in range(n_t):
            cols = slice(t * B_TK, (t + 1) * B_TK)
            p = jnp.exp(s_ref[:, cols] - m)
            f = lane_fold(p, jnp.add)
            ps = f if ps is None else ps + f
            d = jnp.dot(p.astype(BF16), v_ref[0, cols, :], preferred_element_type=F32)
            acc = d if acc is None else acc + d
        o = acc / jnp.sum(ps, axis=-1, keepdims=True)
        out = o[:B_TQ] - lam * o[B_TQ:]
        ms = jnp.mean(out * out, axis=-1, keepdims=True)
        y = out * lax.rsqrt(ms + EPS) * sg_ref[...] * (1.0 - lam_init)
        gate = _silu(g_ref[0, pl.ds(r0, B_TQ), :].astype(F32))
        o_ref[0, pl.ds(r0, B_TQ), :] = (y * gate).astype(BF16)

    buf_a = (sa_ref, pma_ref)
    buf_b = (sb_ref, pmb_ref)
    scores(0, *buf_a)

    def body(j, carry):
        finish(2 * j, *buf_a)
        scores(2 * j + 1, *buf_b)
        finish(2 * j + 1, *buf_b)
        scores(2 * j + 2, *buf_a)
        return carry

    lax.fori_loop(0, n_q // 2 - 1, body, 0)
    finish(n_q - 2, *buf_a)
    scores(n_q - 1, *buf_b)
    finish(n_q - 1, *buf_b)


def _mixer_b(proj, u, lq1, lk1, lq2, lk2, subln_g, lam_init):
    qb = 3 * A_HEADS * HEAD_DIM // LANES
    kb = qb + B_HEADS
    vb = kb + B_HEADS
    gb = 3072 // LANES + A_HEADS * HEAD_DIM // LANES
    vec = lambda n: pl.BlockSpec((1, n), lambda h, b: (0, 0))
    blk = lambda off: pl.BlockSpec((1, SEQ, LANES), lambda h, b: (b, 0, off + h))
    return pl.pallas_call(
        functools.partial(_b_kernel, lam_init=lam_init),
        grid=(B_HEADS, BATCH),
        in_specs=[
            blk(qb), blk(kb), blk(vb), blk(gb),
            pl.BlockSpec((1, B_TQ, 2 * SEQ - B_TQ), lambda h, b: (h, 0, 0)),
            vec(HEAD_DIM), vec(HEAD_DIM), vec(HEAD_DIM), vec(HEAD_DIM), vec(2 * HEAD_DIM),
        ],
        out_specs=pl.BlockSpec((1, SEQ, LANES), lambda h, b: (b, 0, h)),
        out_shape=jax.ShapeDtypeStruct((BATCH, SEQ, B_HEADS * 2 * HEAD_DIM), BF16),
        scratch_shapes=[pltpu.VMEM((2 * B_TQ, SEQ), F32), pltpu.VMEM((2 * B_TQ, SEQ), F32),
                        pltpu.VMEM((2 * B_TQ, LANES), F32), pltpu.VMEM((2 * B_TQ, LANES), F32)],
        compiler_params=pltpu.CompilerParams(
            dimension_semantics=("arbitrary", "arbitrary"), vmem_limit_bytes=VMEM_LIMIT),
        name="mixer_b_differential",
    )(proj, proj, proj, proj, u, lq1.reshape(1, -1), lk1.reshape(1, -1),
      lq2.reshape(1, -1), lk2.reshape(1, -1), subln_g.reshape(1, -1))


def _c_kernel(sink_ref, q_ref, k_ref, v_ref, g_ref, bias_ref, o_ref, ka_ref, va_ref):
    grp = pl.program_id(0)
    half = grp % 2
    lo = _lane_lo(BLK)
    zero = jnp.zeros((BLK, LANES), BF16)
    row = lax.broadcasted_iota(jnp.int32, (LANES, LANES), 0)
    col = lax.broadcasted_iota(jnp.int32, (LANES, LANES), 1)

    chunk = 512
    for a in range(2):
        shift = jnp.where(half == a, 0, HEAD_DIM)
        perm = jnp.where(col == (row + shift) % LANES, 1.0, 0.0).astype(BF16)
        for c in range(SEQ // chunk):
            rows = slice(c * chunk, (c + 1) * chunk)
            ka_ref[a, rows, :] = jnp.dot(k_ref[0, rows, :], perm,
                                         preferred_element_type=F32).astype(BF16)
            va_ref[a, rows, :] = jnp.dot(v_ref[0, rows, :], perm,
                                         preferred_element_type=F32).astype(BF16)

    n_blk = SEQ // BLK
    rep = C_HEADS // C_KV_HEADS

    top = lax.broadcasted_iota(jnp.int32, (2 * BLK, 1), 0) < BLK

    def body(n, carry):
        r0 = pl.multiple_of(n * BLK, BLK)
        ws = pl.multiple_of(jnp.clip(n * BLK - C_HALF_WINDOW, 0, SEQ - C_KEYS), BLK)
        var = jnp.where(n == 0, 0, jnp.where(n == n_blk - 1, 2, 1))
        ys = []
        for a in range(2):
            qa = [q_ref[0, pl.ds(r0, BLK), c * LANES:(c + 1) * LANES] * QK_SCALE
                  for c in range(rep // 2)]
            qm = jnp.concatenate(
                [jnp.where(lo, q, zero) if a == 0 else jnp.where(lo, zero, q) for q in qa], axis=0)
            acc, m, l = _softmax_block(qm, ka_ref[a, pl.ds(ws, C_KEYS), :],
                                       va_ref[a, pl.ds(ws, C_KEYS), :], bias_ref[0, a, var])
            sk = jnp.where(top, sink_ref[grp * rep + a], sink_ref[grp * rep + a + 2])
            mx = jnp.maximum(m, sk)
            e = jnp.exp(m - mx)
            ys.append(acc * e / (l * e + jnp.exp(sk - mx)))
        for c in range(rep // 2):
            rows = slice(c * BLK, (c + 1) * BLK)
            cols = slice(c * LANES, (c + 1) * LANES)
            y = jnp.where(lo, ys[0][rows], ys[1][rows])
            gate = _silu(g_ref[0, pl.ds(r0, BLK), cols].astype(F32))
            o_ref[0, pl.ds(r0, BLK), cols] = (y * gate).astype(BF16)
        return carry

    lax.fori_loop(0, n_blk, body, 0, unroll=C_UNROLL)


def _mixer_c(proj, sink, bias):
    width = C_HEADS // C_KV_HEADS * HEAD_DIM
    kb = C_HEADS * HEAD_DIM // LANES
    vb = kb + C_KV_HEADS * HEAD_DIM // LANES
    gb = (C_HEADS + 2 * C_KV_HEADS) * HEAD_DIM // width
    return pl.pallas_call(
        _c_kernel,
        grid=(C_KV_HEADS, BATCH),
        in_specs=[
            pl.BlockSpec(memory_space=pltpu.SMEM),
            pl.BlockSpec((1, SEQ, width), lambda g, b: (b, 0, g)),
            pl.BlockSpec((1, SEQ, LANES), lambda g, b: (b, 0, kb + g // 2)),
            pl.BlockSpec((1, SEQ, LANES), lambda g, b: (b, 0, vb + g // 2)),
            pl.BlockSpec((1, SEQ, width), lambda g, b: (b, 0, gb + g)),
            pl.BlockSpec((1, 2, 3, 2 * BLK, C_KEYS), lambda g, b: (g, 0, 0, 0, 0)),
        ],
        out_specs=pl.BlockSpec((1, SEQ, width), lambda g, b: (b, 0, g)),
        out_shape=jax.ShapeDtypeStruct((BATCH, SEQ, C_HEADS * HEAD_DIM), BF16),
        scratch_shapes=[pltpu.VMEM((2, SEQ, LANES), BF16), pltpu.VMEM((2, SEQ, LANES), BF16)],
        compiler_params=pltpu.CompilerParams(
            dimension_semantics=("arbitrary", "arbitrary"), vmem_limit_bytes=VMEM_LIMIT),
        name="mixer_c_windowed",
    )(sink, proj, proj, proj, proj, bias)


@functools.lru_cache(maxsize=None)
def _bias_tables():
    def stack_pairs(t, first, second):
        return np.concatenate([t[first], t[second]], axis=2)

    sa = _alibi_slopes(A_HEADS)
    offs = (0, A_HALF, 2 * A_HALF)
    ev, od = slice(0, None, 2), slice(1, None, 2)
    b12 = np.stack(
        [stack_pairs(_band_bias(sa, dil, A_HALF, BLK, A_KEYS, offs), ev, od)
         for _, dil in A_PATTERNS[:2]], axis=1)
    b3 = stack_pairs(_band_bias(sa, A_PATTERNS[2][1], A_HALF, BLK, BLK, (0,)), ev, od)[:, 0]
    sc = _alibi_slopes(C_HEADS)
    bc_heads = _band_bias(sc, 1, C_HALF_WINDOW, BLK, C_KEYS,
                          (0, C_HALF_WINDOW, 2 * C_HALF_WINDOW))
    rep = C_HEADS // C_KV_HEADS
    bc = np.stack([stack_pairs(bc_heads, slice(a, None, rep), slice(a + 2, None, rep))
                   for a in range(2)], axis=1)
    sb = _alibi_slopes(B_HEADS)
    il = np.arange(B_TQ)[:, None]
    uu = np.arange(2 * SEQ - B_TQ)[None, :]
    dist = np.abs(il - uu + (SEQ - B_TQ)).astype(np.float32)
    ub = -sb[:, None, None] * dist[None]
    return b12, b3, bc, ub.astype(np.float32)


def kernel(x, c, ada_w, ada_b, norm_g, ab_w_in, ab_w_out, diff_lq1, diff_lk1, diff_lq2,
           diff_lk2, diff_subln_g, c_w_in, c_w_out, c_sink, final_g):
    b12, b3, bc, ub = (jnp.asarray(t) for t in _bias_tables())
    mod = _modulation(c, ada_w, ada_b)
    for layer in range(DEPTH):
        shift = mod[layer, :, :D_MODEL]
        scale = mod[layer, :, D_MODEL:2 * D_MODEL]
        gate = mod[layer, :, 2 * D_MODEL:]
        j = layer // 2
        final = final_g if layer == DEPTH - 1 else None
        if layer % 2 == 0:
            lam_init = 0.8 - 0.6 * math.exp(-0.3 * layer)
            proj = _in_proj(x, norm_g[layer], scale, shift, ab_w_in[j].astype(BF16))
            za = _mixer_a(proj, b12, b3)
            zb = _mixer_b(proj, ub, diff_lq1[j], diff_lk1[j], diff_lq2[j], diff_lk2[j],
                          diff_subln_g[j], lam_init)
            x = _out_proj(x, gate, [za, zb], ab_w_out[j].astype(BF16), final)
        else:
            proj = _in_proj(x, norm_g[layer], scale, shift, c_w_in[j].astype(BF16))
            zc = _mixer_c(proj, c_sink[j], bc)
            x = _out_proj(x, gate, [zc], c_w_out[j].astype(BF16), final)
    return x
```

```python
import functools
import math

import numpy as np
import jax
import jax.numpy as jnp
from jax import lax
from jax.experimental import pallas as pl
from jax.experimental.pallas import tpu as pltpu

D_MODEL = 1024
BATCH = 8
SEQ = 2048
DEPTH = 4
HEAD_DIM = 64
LANES = 128
A_HEADS = 8
A_PATTERNS = ((128, 1), (512, 4), (2048, 16))
A_HALF = 64
B_HEADS = 4
C_HEADS = 16
C_KV_HEADS = 4
C_HALF_WINDOW = 128
EVEN_IN = 4096
ODD_IN = 2560
EPS = 1e-6
NEG_INF = -1e30
QK_SCALE = HEAD_DIM ** -0.5

F32 = jnp.float32
BF16 = jnp.bfloat16

ROW_TILE = 512
COL_TILE = 512
BLK = 128
A_KEYS = BLK + 2 * A_HALF
C_KEYS = BLK + 2 * C_HALF_WINDOW
A_UNROLL = 8
C_UNROLL = 4
B_TQ = 256
B_TK = 512
VMEM_LIMIT = 56 * 1024 * 1024


def _silu(t):
    return t * (1.0 / (1.0 + jnp.exp(-t)))


def _dot_nt(a, b):
    return lax.dot_general(a, b, (((1,), (1,)), ((), ())), preferred_element_type=F32)


def _alibi_slopes(n):
    return (2.0 ** (-8.0 * np.arange(1, n + 1, dtype=np.float32) / n)).astype(np.float32)


def _band_bias(slopes, spacing, half, tq, tk, offsets):
    i = np.arange(tq)[:, None]
    c = np.arange(tk)[None, :]
    out = np.empty((len(slopes), len(offsets), tq, tk), np.float32)
    for v, off in enumerate(offsets):
        rel = np.abs(c - (i + off))
        dist = (rel * spacing).astype(np.float32)
        for h, m in enumerate(slopes):
            out[h, v] = np.where(rel <= half, -m * dist, np.float32(NEG_INF))
    return out


def _mod_kernel(c_ref, w_ref, b_ref, o_ref):
    cs = _silu(c_ref[...])
    o_ref[0] = jnp.dot(cs, w_ref[0], preferred_element_type=F32,
                       precision=lax.Precision.HIGHEST) + b_ref[0]


def _modulation(c, ada_w, ada_b):
    nblk = 3 * D_MODEL // D_MODEL
    return pl.pallas_call(
        _mod_kernel,
        grid=(DEPTH, nblk),
        in_specs=[
            pl.BlockSpec((BATCH, D_MODEL), lambda l, j: (0, 0)),
            pl.BlockSpec((1, D_MODEL, D_MODEL), lambda l, j: (l, 0, j)),
            pl.BlockSpec((1, 1, D_MODEL), lambda l, j: (l, 0, j)),
        ],
        out_specs=pl.BlockSpec((1, BATCH, D_MODEL), lambda l, j: (l, 0, j)),
        out_shape=jax.ShapeDtypeStruct((DEPTH, BATCH, 3 * D_MODEL), F32),
        name="adaln_mod",
    )(c, ada_w, ada_b.reshape(DEPTH, 1, 3 * D_MODEL))


def _in_proj_kernel(x_ref, g_ref, sc_ref, sh_ref, w_ref, o_ref, h_ref, *, n_out):
    x = x_ref[0]
    ms = jnp.mean(x * x, axis=-1, keepdims=True)
    y = x * lax.rsqrt(ms + EPS) * g_ref[...]
    h_ref[...] = (y * (1.0 + sc_ref[0]) + sh_ref[0]).astype(BF16)
    for j in range(n_out // COL_TILE):
        cols = slice(j * COL_TILE, (j + 1) * COL_TILE)
        o_ref[0, :, cols] = jnp.dot(h_ref[...], w_ref[:, cols],
                                    preferred_element_type=F32).astype(BF16)


def _in_proj(x, g, scale, shift, w):
    n_out = w.shape[1]
    return pl.pallas_call(
        functools.partial(_in_proj_kernel, n_out=n_out),
        grid=(BATCH, SEQ // ROW_TILE),
        in_specs=[
            pl.BlockSpec((1, ROW_TILE, D_MODEL), lambda b, i: (b, i, 0)),
            pl.BlockSpec((1, D_MODEL), lambda b, i: (0, 0)),
            pl.BlockSpec((1, 1, D_MODEL), lambda b, i: (b, 0, 0)),
            pl.BlockSpec((1, 1, D_MODEL), lambda b, i: (b, 0, 0)),
            pl.BlockSpec((D_MODEL, n_out), lambda b, i: (0, 0)),
        ],
        out_specs=pl.BlockSpec((1, ROW_TILE, n_out), lambda b, i: (b, i, 0)),
        out_shape=jax.ShapeDtypeStruct((BATCH, SEQ, n_out), BF16),
        scratch_shapes=[pltpu.VMEM((ROW_TILE, D_MODEL), BF16)],
        compiler_params=pltpu.CompilerParams(
            dimension_semantics=("arbitrary", "arbitrary"), vmem_limit_bytes=VMEM_LIMIT),
        name="norm_in_proj",
    )(x, g.reshape(1, D_MODEL), scale.reshape(BATCH, 1, D_MODEL),
      shift.reshape(BATCH, 1, D_MODEL), w)


def _out_proj_kernel(*refs, n_z, final):
    x_ref, gate_ref = refs[0], refs[1]
    z_refs = refs[2:2 + n_z]
    w_ref = refs[2 + n_z]
    fg_ref = refs[3 + n_z] if final else None
    o_ref = refs[-1]
    kz = D_MODEL // n_z
    acc = jnp.dot(z_refs[0][0], w_ref[0:kz, :], preferred_element_type=F32)
    for i in range(1, n_z):
        acc = acc + jnp.dot(z_refs[i][0], w_ref[i * kz:(i + 1) * kz, :],
                            preferred_element_type=F32)
    xn = x_ref[0] + gate_ref[0] * acc
    if final:
        ms = jnp.mean(xn * xn, axis=-1, keepdims=True)
        xn = xn * lax.rsqrt(ms + EPS) * fg_ref[...]
    o_ref[0] = xn


def _out_proj(x, gate, zs, w, final_g=None):
    n_z = len(zs)
    kz = D_MODEL // n_z
    final = final_g is not None
    in_specs = [
        pl.BlockSpec((1, ROW_TILE, D_MODEL), lambda b, i: (b, i, 0)),
        pl.BlockSpec((1, 1, D_MODEL), lambda b, i: (b, 0, 0)),
    ]
    in_specs += [pl.BlockSpec((1, ROW_TILE, kz), lambda b, i: (b, i, 0)) for _ in zs]
    in_specs += [pl.BlockSpec((D_MODEL, D_MODEL), lambda b, i: (0, 0))]
    args = [x, gate.reshape(BATCH, 1, D_MODEL), *zs, w]
    if final:
        in_specs += [pl.BlockSpec((1, D_MODEL), lambda b, i: (0, 0))]
        args += [final_g.reshape(1, D_MODEL)]
    return pl.pallas_call(
        functools.partial(_out_proj_kernel, n_z=n_z, final=final),
        grid=(BATCH, SEQ // ROW_TILE),
        in_specs=in_specs,
        out_specs=pl.BlockSpec((1, ROW_TILE, D_MODEL), lambda b, i: (b, i, 0)),
        out_shape=jax.ShapeDtypeStruct((BATCH, SEQ, D_MODEL), F32),
        compiler_params=pltpu.CompilerParams(
            dimension_semantics=("arbitrary", "arbitrary"), vmem_limit_bytes=VMEM_LIMIT),
        name="out_proj_residual",
    )(*args)


def _lane_lo(rows):
    return lax.broadcasted_iota(jnp.int32, (rows, LANES), 1) < HEAD_DIM


def _softmax_block(qm, kw, vw, bias):
    s = _dot_nt(qm, kw) + bias
    m = jnp.max(s, axis=-1, keepdims=True)
    p = jnp.exp(s - m)
    l = jnp.sum(p, axis=-1, keepdims=True)
    acc = jnp.dot(p.astype(BF16), vw, preferred_element_type=F32)
    return acc, m, l


def _stack_heads(qb, lo):
    zero = jnp.zeros_like(qb)
    return jnp.concatenate([jnp.where(lo, qb, zero), jnp.where(lo, zero, qb)], axis=0)


def _a_kernel(q_ref, k_ref, v_ref, g_ref, b12_ref, b3_ref, o_ref,
              qs_ref, qf_ref, kf_ref, vf_ref, qd_ref, kd_ref, vd_ref,
              acc_ref, m_ref, l_ref):
    lo = _lane_lo(BLK)

    def pair_block(qb, kw, vw, bias):
        acc, m, l = _softmax_block(_stack_heads(qb, lo), kw, vw, bias)
        return (jnp.where(lo, acc[:BLK], acc[BLK:]), jnp.where(lo, m[:BLK], m[BLK:]),
                jnp.where(lo, l[:BLK], l[BLK:]))

    chunk = 512
    for c in range(SEQ // chunk):
        rows = slice(c * chunk, (c + 1) * chunk)
        qs = q_ref[0, rows, :] * QK_SCALE
        qs_ref[rows, :] = qs
        qf_ref[rows, :] = qs.astype(F32)
        kf_ref[rows, :] = k_ref[0, rows, :].astype(F32)
        vf_ref[rows, :] = v_ref[0, rows, :].astype(F32)

    n_blk = SEQ // BLK

    def variant(blk, last):
        return jnp.where(blk == 0, 0, jnp.where(blk == last, 2, 1))

    def p0_body(j, carry):
        r0 = pl.multiple_of(j * BLK, BLK)
        ws = pl.multiple_of(jnp.clip(j * BLK - A_HALF, 0, SEQ - A_KEYS), A_HALF)
        var = variant(j, n_blk - 1)
        acc, m, l = pair_block(qs_ref[pl.ds(r0, BLK), :],
                               k_ref[0, pl.ds(ws, A_KEYS), :], v_ref[0, pl.ds(ws, A_KEYS), :],
                               b12_ref[0, 0, var])
        acc_ref[0, pl.ds(r0, BLK), :] = acc
        m_ref[0, pl.ds(r0, BLK), :] = m
        l_ref[0, pl.ds(r0, BLK), :] = l
        return carry

    lax.fori_loop(0, n_blk, p0_body, 0, unroll=A_UNROLL)

    def deinterleave(dil):
        seg = SEQ // dil
        for r in range(dil):
            dst = slice(r * seg, (r + 1) * seg)
            qd_ref[dst, :] = qf_ref[pl.ds(r, seg, stride=dil), :].astype(BF16)
            kd_ref[dst, :] = kf_ref[pl.ds(r, seg, stride=dil), :].astype(BF16)
            vd_ref[dst, :] = vf_ref[pl.ds(r, seg, stride=dil), :].astype(BF16)

    dil1 = A_PATTERNS[1][1]
    seg1 = SEQ // dil1
    per1 = seg1 // BLK
    deinterleave(dil1)

    def p1_body(j, carry):
        r = j // per1
        blk = j % per1
        r0 = pl.multiple_of(j * BLK, BLK)
        ws = pl.multiple_of(r * seg1 + jnp.clip(blk * BLK - A_HALF, 0, seg1 - A_KEYS), A_HALF)
        var = variant(blk, per1 - 1)
        acc, m, l = pair_block(qd_ref[pl.ds(r0, BLK), :],
                               kd_ref[pl.ds(ws, A_KEYS), :], vd_ref[pl.ds(ws, A_KEYS), :],
                               b12_ref[0, 1, var])
        dst = pl.ds(blk * BLK * dil1 + r, BLK, stride=dil1)
        acc_ref[1, dst, :] = acc
        m_ref[1, dst, :] = m
        l_ref[1, dst, :] = l
        return carry

    lax.fori_loop(0, n_blk, p1_body, 0, unroll=A_UNROLL)

    dil2 = A_PATTERNS[2][1]
    deinterleave(dil2)

    def p2_body(j, carry):
        r0 = pl.multiple_of(j * BLK, BLK)
        acc, m, l = pair_block(qd_ref[pl.ds(r0, BLK), :],
                               kd_ref[pl.ds(r0, BLK), :], vd_ref[pl.ds(r0, BLK), :],
                               b3_ref[0])
        dst = pl.ds(j, BLK, stride=dil2)
        acc_ref[2, dst, :] = acc
        m_ref[2, dst, :] = m
        l_ref[2, dst, :] = l
        return carry

    lax.fori_loop(0, n_blk, p2_body, 0, unroll=A_UNROLL)

    mrows = 256

    def merge_body(c, carry):
        rows = pl.ds(pl.multiple_of(c * mrows, mrows), mrows)
        m0, m1, m2 = m_ref[0, rows, :], m_ref[1, rows, :], m_ref[2, rows, :]
        mx = jnp.maximum(jnp.maximum(m0, m1), m2)
        w0, w1, w2 = jnp.exp(m0 - mx), jnp.exp(m1 - mx), jnp.exp(m2 - mx)
        num = w0 * acc_ref[0, rows, :] + w1 * acc_ref[1, rows, :] + w2 * acc_ref[2, rows, :]
        den = w0 * l_ref[0, rows, :] + w1 * l_ref[1, rows, :] + w2 * l_ref[2, rows, :]
        y = num / den
        o_ref[0, rows, :] = (y * _silu(g_ref[0, rows, :].astype(F32))).astype(BF16)
        return carry

    lax.fori_loop(0, SEQ // mrows, merge_body, 0)


def _mixer_a(proj, bias12, bias3):
    pairs = A_HEADS // 2
    blk = lambda off: pl.BlockSpec((1, SEQ, LANES), lambda hp, b: (b, 0, off + hp))
    return pl.pallas_call(
        _a_kernel,
        grid=(pairs, BATCH),
        in_specs=[
            blk(0), blk(pairs), blk(2 * pairs), blk(3072 // LANES),
            pl.BlockSpec((1, 2, 3, 2 * BLK, A_KEYS), lambda hp, b: (hp, 0, 0, 0, 0)),
            pl.BlockSpec((1, 2 * BLK, BLK), lambda hp, b: (hp, 0, 0)),
        ],
        out_specs=pl.BlockSpec((1, SEQ, LANES), lambda hp, b: (b, 0, hp)),
        out_shape=jax.ShapeDtypeStruct((BATCH, SEQ, A_HEADS * HEAD_DIM), BF16),
        scratch_shapes=[
            pltpu.VMEM((SEQ, LANES), BF16),
            pltpu.VMEM((SEQ, LANES), F32), pltpu.VMEM((SEQ, LANES), F32),
            pltpu.VMEM((SEQ, LANES), F32),
            pltpu.VMEM((SEQ, LANES), BF16), pltpu.VMEM((SEQ, LANES), BF16),
            pltpu.VMEM((SEQ, LANES), BF16),
            pltpu.VMEM((3, SEQ, LANES), F32), pltpu.VMEM((3, SEQ, LANES), F32),
            pltpu.VMEM((3, SEQ, LANES), F32),
        ],
        compiler_params=pltpu.CompilerParams(
            dimension_semantics=("arbitrary", "arbitrary"), vmem_limit_bytes=VMEM_LIMIT),
        name="mixer_a_dilated",
    )(proj, proj, proj, proj, bias12, bias3)


def _b_kernel(q_ref, k_ref, v_ref, g_ref, u_ref, lq1_ref, lk1_ref, lq2_ref, lk2_ref,
              sg_ref, o_ref, vt_ref, sa_ref, sb_ref, ma_ref, mb_ref, *, lam_init):
    lo = _lane_lo(B_TQ)
    n_t = SEQ // B_TK
    n_q = SEQ // B_TQ
    lam = (jnp.exp(jnp.sum(lq1_ref[...] * lk1_ref[...], axis=-1, keepdims=True))
           - jnp.exp(jnp.sum(lq2_ref[...] * lk2_ref[...], axis=-1, keepdims=True)) + lam_init)

    eye = jnp.where(lax.broadcasted_iota(jnp.int32, (LANES, LANES), 0)
                    == lax.broadcasted_iota(jnp.int32, (LANES, LANES), 1), 1.0, 0.0).astype(BF16)
    for t in range(n_t):
        rows = slice(t * B_TK, (t + 1) * B_TK)
        vt_ref[:, rows] = _dot_nt(eye, v_ref[0, rows, :]).astype(BF16)

    def scores(i, s_ref, m_ref):
        r0 = pl.multiple_of(i * B_TQ, B_TQ)
        qst = _stack_heads(q_ref[0, pl.ds(r0, B_TQ), :] * QK_SCALE, lo)
        m = None
        for t in range(n_t):
            rows = slice(t * B_TK, (t + 1) * B_TK)
            off = pl.multiple_of(t * B_TK + (SEQ - B_TQ) - r0, LANES)
            ub = u_ref[0, pl.ds(off, B_TK), :]
            st = _dot_nt(k_ref[0, rows, :], qst)
            s1 = st[:, :B_TQ] + ub
            s2 = st[:, B_TQ:] + ub
            s_ref[rows, :B_TQ] = s1
            s_ref[rows, B_TQ:] = s2
            f = jnp.concatenate([jnp.max(s1, axis=0, keepdims=True),
                                 jnp.max(s2, axis=0, keepdims=True)], axis=1)
            m = f if m is None else jnp.maximum(m, f)
        m_ref[...] = m

    def finish(i, s_ref, m_ref):
        r0 = pl.multiple_of(i * B_TQ, B_TQ)
        m = m_ref[...]
        l = None
        acct = None
        for t in range(n_t):
            rows = slice(t * B_TK, (t + 1) * B_TK)
            p = jnp.exp(s_ref[rows, :] - m)
            f = jnp.sum(p, axis=0, keepdims=True)
            l = f if l is None else l + f
            d = jnp.dot(vt_ref[:, rows], p.astype(BF16), preferred_element_type=F32)
            acct = d if acct is None else acct + d
        ot = acct * (1.0 / l)
        outt = ot[:, :B_TQ] - lam * ot[:, B_TQ:]
        ms = jnp.mean(outt * outt, axis=0, keepdims=True)
        yt = outt * lax.rsqrt(ms + EPS) * sg_ref[...] * (1.0 - lam_init)
        gate = _silu(g_ref[0, pl.ds(r0, B_TQ), :].astype(F32))
        o_ref[0, pl.ds(r0, B_TQ), :] = (yt.T * gate).astype(BF16)

    buf_a = (sa_ref, ma_ref)
    buf_b = (sb_ref, mb_ref)
    scores(0, *buf_a)

    def body(j, carry):
        finish(2 * j, *buf_a)
        scores(2 * j + 1, *buf_b)
        finish(2 * j + 1, *buf_b)
        scores(2 * j + 2, *buf_a)
        return carry

    lax.fori_loop(0, n_q // 2 - 1, body, 0)
    finish(n_q - 2, *buf_a)
    scores(n_q - 1, *buf_b)
    finish(n_q - 1, *buf_b)


def _mixer_b(proj, u, lq1, lk1, lq2, lk2, subln_g, lam_init):
    qb = 3 * A_HEADS * HEAD_DIM // LANES
    kb = qb + B_HEADS
    vb = kb + B_HEADS
    gb = 3072 // LANES + A_HEADS * HEAD_DIM // LANES
    vec = lambda n: pl.BlockSpec((1, n), lambda h, b: (0, 0))
    blk = lambda off: pl.BlockSpec((1, SEQ, LANES), lambda h, b: (b, 0, off + h))
    return pl.pallas_call(
        functools.partial(_b_kernel, lam_init=lam_init),
        grid=(B_HEADS, BATCH),
        in_specs=[
            blk(qb), blk(kb), blk(vb), blk(gb),
            pl.BlockSpec((1, 2 * SEQ - B_TQ, B_TQ), lambda h, b: (h, 0, 0)),
            vec(HEAD_DIM), vec(HEAD_DIM), vec(HEAD_DIM), vec(HEAD_DIM),
            pl.BlockSpec((2 * HEAD_DIM, 1), lambda h, b: (0, 0)),
        ],
        out_specs=pl.BlockSpec((1, SEQ, LANES), lambda h, b: (b, 0, h)),
        out_shape=jax.ShapeDtypeStruct((BATCH, SEQ, B_HEADS * 2 * HEAD_DIM), BF16),
        scratch_shapes=[pltpu.VMEM((LANES, SEQ), BF16),
                        pltpu.VMEM((SEQ, 2 * B_TQ), F32), pltpu.VMEM((SEQ, 2 * B_TQ), F32),
                        pltpu.VMEM((1, 2 * B_TQ), F32), pltpu.VMEM((1, 2 * B_TQ), F32)],
        compiler_params=pltpu.CompilerParams(
            dimension_semantics=("arbitrary", "arbitrary"), vmem_limit_bytes=VMEM_LIMIT),
        name="mixer_b_differential",
    )(proj, proj, proj, proj, u, lq1.reshape(1, -1), lk1.reshape(1, -1),
      lq2.reshape(1, -1), lk2.reshape(1, -1), subln_g.reshape(-1, 1))


def _c_kernel(sink_ref, q_ref, k_ref, v_ref, g_ref, bias_ref, o_ref, ka_ref, vt_ref,
              sa_ref, sb_ref, ma_ref, mb_ref):
    grp = pl.program_id(0)
    half = grp % 2
    lo = _lane_lo(BLK)
    zero = jnp.zeros((BLK, LANES), BF16)
    row = lax.broadcasted_iota(jnp.int32, (LANES, LANES), 0)
    col = lax.broadcasted_iota(jnp.int32, (LANES, LANES), 1)

    chunk = 512
    for a in range(2):
        shift = jnp.where(half == a, 0, HEAD_DIM)
        perm = jnp.where(col == (row + shift) % LANES, 1.0, 0.0).astype(BF16)
        for c in range(SEQ // chunk):
            rows = slice(c * chunk, (c + 1) * chunk)
            ka_ref[a, rows, :] = jnp.dot(k_ref[0, rows, :], perm,
                                         preferred_element_type=F32).astype(BF16)
            vt_ref[a, :, rows] = _dot_nt(perm, v_ref[0, rows, :]).astype(BF16)

    n_blk = SEQ // BLK
    rep = C_HEADS // C_KV_HEADS
    left = lax.broadcasted_iota(jnp.int32, (1, 2 * BLK), 1) < BLK
    upper = lax.broadcasted_iota(jnp.int32, (LANES, BLK), 0) < HEAD_DIM

    def window(n):
        return pl.multiple_of(jnp.clip(n * BLK - C_HALF_WINDOW, 0, SEQ - C_KEYS), BLK)

    def scores(n, s_ref, m_ref):
        r0 = pl.multiple_of(n * BLK, BLK)
        ws = window(n)
        var = jnp.where(n == 0, 0, jnp.where(n == n_blk - 1, 2, 1))
        for a in range(2):
            qa = [q_ref[0, pl.ds(r0, BLK), c * LANES:(c + 1) * LANES] * QK_SCALE
                  for c in range(rep // 2)]
            qm = jnp.concatenate(
                [jnp.where(lo, q, zero) if a == 0 else jnp.where(lo, zero, q) for q in qa], axis=0)
            st = _dot_nt(ka_ref[a, pl.ds(ws, C_KEYS), :], qm) + bias_ref[0, a, var]
            s_ref[a] = st
            m_ref[a] = jnp.max(st, axis=0, keepdims=True)

    def finish(n, s_ref, m_ref):
        r0 = pl.multiple_of(n * BLK, BLK)
        ws = window(n)
        yts = []
        for a in range(2):
            m = m_ref[a]
            p = jnp.exp(s_ref[a] - m)
            l = jnp.sum(p, axis=0, keepdims=True)
            acct = jnp.dot(vt_ref[a, :, pl.ds(ws, C_KEYS)], p.astype(BF16),
                           preferred_element_type=F32)
            sk = jnp.where(left, sink_ref[grp * rep + a], sink_ref[grp * rep + a + 2])
            mx = jnp.maximum(m, sk)
            e = jnp.exp(m - mx)
            yts.append(acct * (e / (l * e + jnp.exp(sk - mx))))
        for c in range(rep // 2):
            cols = slice(c * LANES, (c + 1) * LANES)
            y = jnp.where(upper, yts[0][:, cols], yts[1][:, cols]).T
            gate = _silu(g_ref[0, pl.ds(r0, BLK), cols].astype(F32))
            o_ref[0, pl.ds(r0, BLK), cols] = (y * gate).astype(BF16)

    buf_a = (sa_ref, ma_ref)
    buf_b = (sb_ref, mb_ref)
    scores(0, *buf_a)

    def body(j, carry):
        finish(2 * j, *buf_a)
        scores(2 * j + 1, *buf_b)
        finish(2 * j + 1, *buf_b)
        scores(2 * j + 2, *buf_a)
        return carry

    lax.fori_loop(0, n_blk // 2 - 1, body, 0)
    finish(n_blk - 2, *buf_a)
    scores(n_blk - 1, *buf_b)
    finish(n_blk - 1, *buf_b)


def _mixer_c(proj, sink, bias):
    width = C_HEADS // C_KV_HEADS * HEAD_DIM
    kb = C_HEADS * HEAD_DIM // LANES
    vb = kb + C_KV_HEADS * HEAD_DIM // LANES
    gb = (C_HEADS + 2 * C_KV_HEADS) * HEAD_DIM // width
    return pl.pallas_call(
        _c_kernel,
        grid=(C_KV_HEADS, BATCH),
        in_specs=[
            pl.BlockSpec(memory_space=pltpu.SMEM),
            pl.BlockSpec((1, SEQ, width), lambda g, b: (b, 0, g)),
            pl.BlockSpec((1, SEQ, LANES), lambda g, b: (b, 0, kb + g // 2)),
            pl.BlockSpec((1, SEQ, LANES), lambda g, b: (b, 0, vb + g // 2)),
            pl.BlockSpec((1, SEQ, width), lambda g, b: (b, 0, gb + g)),
            pl.BlockSpec((1, 2, 3, C_KEYS, 2 * BLK), lambda g, b: (g, 0, 0, 0, 0)),
        ],
        out_specs=pl.BlockSpec((1, SEQ, width), lambda g, b: (b, 0, g)),
        out_shape=jax.ShapeDtypeStruct((BATCH, SEQ, C_HEADS * HEAD_DIM), BF16),
        scratch_shapes=[pltpu.VMEM((2, SEQ, LANES), BF16), pltpu.VMEM((2, LANES, SEQ), BF16),
                        pltpu.VMEM((2, C_KEYS, 2 * BLK), F32), pltpu.VMEM((2, C_KEYS, 2 * BLK), F32),
                        pltpu.VMEM((2, 1, 2 * BLK), F32), pltpu.VMEM((2, 1, 2 * BLK), F32)],
        compiler_params=pltpu.CompilerParams(
            dimension_semantics=("arbitrary", "arbitrary"), vmem_limit_bytes=VMEM_LIMIT),
        name="mixer_c_windowed",
    )(sink, proj, proj, proj, proj, bias)


@functools.lru_cache(maxsize=None)
def _bias_tables():
    def stack_pairs(t, first, second):
        return np.concatenate([t[first], t[second]], axis=2)

    sa = _alibi_slopes(A_HEADS)
    offs = (0, A_HALF, 2 * A_HALF)
    ev, od = slice(0, None, 2), slice(1, None, 2)
    b12 = np.stack(
        [stack_pairs(_band_bias(sa, dil, A_HALF, BLK, A_KEYS, offs), ev, od)
         for _, dil in A_PATTERNS[:2]], axis=1)
    b3 = stack_pairs(_band_bias(sa, A_PATTERNS[2][1], A_HALF, BLK, BLK, (0,)), ev, od)[:, 0]
    sc = _alibi_slopes(C_HEADS)
    bc_heads = _band_bias(sc, 1, C_HALF_WINDOW, BLK, C_KEYS,
                          (0, C_HALF_WINDOW, 2 * C_HALF_WINDOW))
    rep = C_HEADS // C_KV_HEADS
    bc = np.stack([stack_pairs(bc_heads, slice(a, None, rep), slice(a + 2, None, rep))
                   for a in range(2)], axis=1)
    bc = np.ascontiguousarray(np.swapaxes(bc, -1, -2))
    sb = _alibi_slopes(B_HEADS)
    il = np.arange(B_TQ)[None, :]
    uu = np.arange(2 * SEQ - B_TQ)[:, None]
    dist = np.abs(il - uu + (SEQ - B_TQ)).astype(np.float32)
    ub = -sb[:, None, None] * dist[None]
    return b12, b3, bc, ub.astype(np.float32)


def kernel(x, c, ada_w, ada_b, norm_g, ab_w_in, ab_w_out, diff_lq1, diff_lk1, diff_lq2,
           diff_lk2, diff_subln_g, c_w_in, c_w_out, c_sink, final_g):
    b12, b3, bc, ub = (jnp.asarray(t) for t in _bias_tables())
    mod = _modulation(c, ada_w, ada_b)
    for layer in range(DEPTH):
        shift = mod[layer, :, :D_MODEL]
        scale = mod[layer, :, D_MODEL:2 * D_MODEL]
        gate = mod[layer, :, 2 * D_MODEL:]
        j = layer // 2
        final = final_g if layer == DEPTH - 1 else None
        if layer % 2 == 0:
            lam_init = 0.8 - 0.6 * math.exp(-0.3 * layer)
            proj = _in_proj(x, norm_g[layer], scale, shift, ab_w_in[j].astype(BF16))
            za = _mixer_a(proj, b12, b3)
            zb = _mixer_b(proj, ub, diff_lq1[j], diff_lk1[j], diff_lq2[j], diff_lk2[j],
                          diff_subln_g[j], lam_init)
            x = _out_proj(x, gate, [za, zb], ab_w_out[j].astype(BF16), final)
        else:
            proj = _in_proj(x, norm_g[layer], scale, shift, c_w_in[j].astype(BF16))
            zc = _mixer_c(proj, c_sink[j], bc)
            x = _out_proj(x, gate, [zc], c_w_out[j].astype(BF16), final)
    return x
```

```python
import functools
import math

import numpy as np
import jax
import jax.numpy as jnp
from jax import lax
from jax.experimental import pallas as pl
from jax.experimental.pallas import tpu as pltpu

D_MODEL = 1024
BATCH = 8
SEQ = 2048
DEPTH = 4
HEAD_DIM = 64
LANES = 128
A_HEADS = 8
A_PATTERNS = ((128, 1), (512, 4), (2048, 16))
A_HALF = 64
B_HEADS = 4
C_HEADS = 16
C_KV_HEADS = 4
C_HALF_WINDOW = 128
EVEN_IN = 4096
ODD_IN = 2560
EPS = 1e-6
NEG_INF = -1e30
LOG2E = math.log2(math.e)
Q_FOLD = HEAD_DIM ** -0.5 * LOG2E
ONES_ROWS = 16

F32 = jnp.float32
BF16 = jnp.bfloat16

ROW_TILE = 512
COL_TILE = 512
BLK = 128
A_KEYS = BLK + 2 * A_HALF
C_KEYS = BLK + 2 * C_HALF_WINDOW
A_UNROLL = 8
C_GROUP = 2
B_TQ = 256
B_TK = 256
VMEM_LIMIT = 56 * 1024 * 1024


def _silu(t):
    return t * (1.0 / (1.0 + jnp.exp(-t)))


def _dot_nt(a, b):
    return lax.dot_general(a, b, (((1,), (1,)), ((), ())), preferred_element_type=F32)


def _alibi_slopes(n):
    return (2.0 ** (-8.0 * np.arange(1, n + 1, dtype=np.float32) / n)).astype(np.float32)


def _band_bias(slopes, spacing, half, tq, tk, offsets):
    i = np.arange(tq)[:, None]
    c = np.arange(tk)[None, :]
    out = np.empty((len(slopes), len(offsets), tq, tk), np.float32)
    for v, off in enumerate(offsets):
        rel = np.abs(c - (i + off))
        dist = (rel * spacing).astype(np.float32)
        for h, m in enumerate(slopes):
            out[h, v] = np.where(rel <= half, -m * dist * np.float32(LOG2E), np.float32(NEG_INF))
    return out


def _mod_kernel(c_ref, w_ref, b_ref, o_ref):
    cs = _silu(c_ref[...])
    o_ref[0] = jnp.dot(cs, w_ref[0], preferred_element_type=F32,
                       precision=lax.Precision.HIGHEST) + b_ref[0]


def _modulation(c, ada_w, ada_b):
    nblk = 3 * D_MODEL // D_MODEL
    return pl.pallas_call(
        _mod_kernel,
        grid=(DEPTH, nblk),
        in_specs=[
            pl.BlockSpec((BATCH, D_MODEL), lambda l, j: (0, 0)),
            pl.BlockSpec((1, D_MODEL, D_MODEL), lambda l, j: (l, 0, j)),
            pl.BlockSpec((1, 1, D_MODEL), lambda l, j: (l, 0, j)),
        ],
        out_specs=pl.BlockSpec((1, BATCH, D_MODEL), lambda l, j: (l, 0, j)),
        out_shape=jax.ShapeDtypeStruct((DEPTH, BATCH, 3 * D_MODEL), F32),
        name="adaln_mod",
    )(c, ada_w, ada_b.reshape(DEPTH, 1, 3 * D_MODEL))


def _in_proj_kernel(x_ref, g_ref, sc_ref, sh_ref, w_ref, o_ref, h_ref, *, n_out, q_cols):
    x = x_ref[0]
    ms = jnp.mean(x * x, axis=-1, keepdims=True)
    y = x * lax.rsqrt(ms + EPS) * g_ref[...]
    h_ref[...] = (y * (1.0 + sc_ref[0]) + sh_ref[0]).astype(BF16)
    for j in range(n_out // COL_TILE):
        lo, hi = j * COL_TILE, (j + 1) * COL_TILE
        acc = jnp.dot(h_ref[...], w_ref[:, lo:hi], preferred_element_type=F32)
        if any(q0 <= lo and hi <= q1 for q0, q1 in q_cols):
            acc = acc * Q_FOLD
        o_ref[0, :, lo:hi] = acc.astype(BF16)


def _in_proj(x, g, scale, shift, w, q_cols):
    n_out = w.shape[1]
    assert all(q0 % COL_TILE == 0 and q1 % COL_TILE == 0 for q0, q1 in q_cols)
    return pl.pallas_call(
        functools.partial(_in_proj_kernel, n_out=n_out, q_cols=q_cols),
        grid=(BATCH, SEQ // ROW_TILE),
        in_specs=[
            pl.BlockSpec((1, ROW_TILE, D_MODEL), lambda b, i: (b, i, 0)),
            pl.BlockSpec((1, D_MODEL), lambda b, i: (0, 0)),
            pl.BlockSpec((1, 1, D_MODEL), lambda b, i: (b, 0, 0)),
            pl.BlockSpec((1, 1, D_MODEL), lambda b, i: (b, 0, 0)),
            pl.BlockSpec((D_MODEL, n_out), lambda b, i: (0, 0)),
        ],
        out_specs=pl.BlockSpec((1, ROW_TILE, n_out), lambda b, i: (b, i, 0)),
        out_shape=jax.ShapeDtypeStruct((BATCH, SEQ, n_out), BF16),
        scratch_shapes=[pltpu.VMEM((ROW_TILE, D_MODEL), BF16)],
        compiler_params=pltpu.CompilerParams(
            dimension_semantics=("arbitrary", "arbitrary"), vmem_limit_bytes=VMEM_LIMIT),
        name="norm_in_proj",
    )(x, g.reshape(1, D_MODEL), scale.reshape(BATCH, 1, D_MODEL),
      shift.reshape(BATCH, 1, D_MODEL), w)


def _out_proj_kernel(*refs, n_z, final):
    x_ref, gate_ref = refs[0], refs[1]
    z_refs = refs[2:2 + n_z]
    w_ref = refs[2 + n_z]
    fg_ref = refs[3 + n_z] if final else None
    o_ref = refs[-1]
    kz = D_MODEL // n_z
    acc = jnp.dot(z_refs[0][0], w_ref[0:kz, :], preferred_element_type=F32)
    for i in range(1, n_z):
        acc = acc + jnp.dot(z_refs[i][0], w_ref[i * kz:(i + 1) * kz, :],
                            preferred_element_type=F32)
    xn = x_ref[0] + gate_ref[0] * acc
    if final:
        ms = jnp.mean(xn * xn, axis=-1, keepdims=True)
        xn = xn * lax.rsqrt(ms + EPS) * fg_ref[...]
    o_ref[0] = xn


def _out_proj(x, gate, zs, w, final_g=None):
    n_z = len(zs)
    kz = D_MODEL // n_z
    final = final_g is not None
    in_specs = [
        pl.BlockSpec((1, ROW_TILE, D_MODEL), lambda b, i: (b, i, 0)),
        pl.BlockSpec((1, 1, D_MODEL), lambda b, i: (b, 0, 0)),
    ]
    in_specs += [pl.BlockSpec((1, ROW_TILE, kz), lambda b, i: (b, i, 0)) for _ in zs]
    in_specs += [pl.BlockSpec((D_MODEL, D_MODEL), lambda b, i: (0, 0))]
    args = [x, gate.reshape(BATCH, 1, D_MODEL), *zs, w]
    if final:
        in_specs += [pl.BlockSpec((1, D_MODEL), lambda b, i: (0, 0))]
        args += [final_g.reshape(1, D_MODEL)]
    return pl.pallas_call(
        functools.partial(_out_proj_kernel, n_z=n_z, final=final),
        grid=(BATCH, SEQ // ROW_TILE),
        in_specs=in_specs,
        out_specs=pl.BlockSpec((1, ROW_TILE, D_MODEL), lambda b, i: (b, i, 0)),
        out_shape=jax.ShapeDtypeStruct((BATCH, SEQ, D_MODEL), F32),
        compiler_params=pltpu.CompilerParams(
            dimension_semantics=("arbitrary", "arbitrary"), vmem_limit_bytes=VMEM_LIMIT),
        name="out_proj_residual",
    )(*args)


def _lane_lo(rows):
    return lax.broadcasted_iota(jnp.int32, (rows, LANES), 1) < HEAD_DIM


def _softmax_block(qm, kw, vw, bias):
    s = _dot_nt(qm, kw) + bias
    m = jnp.max(s, axis=-1, keepdims=True)
    p = jnp.exp2(s - m)
    l = jnp.sum(p, axis=-1, keepdims=True)
    acc = jnp.dot(p.astype(BF16), vw, preferred_element_type=F32)
    return acc, m, l


def _interleave(*stages):
    live = list(stages)
    while live:
        for g in list(live):
            try:
                next(g)
            except StopIteration:
                live.remove(g)


def _stack_heads(qb, lo):
    zero = jnp.zeros_like(qb)
    return jnp.concatenate([jnp.where(lo, qb, zero), jnp.where(lo, zero, qb)], axis=0)


def _a_kernel(q_ref, k_ref, v_ref, g_ref, b12_ref, b3_ref, o_ref,
              qf_ref, kf_ref, vf_ref, qd_ref, kd_ref, vd_ref,
              acc_ref, m_ref, l_ref):
    lo = _lane_lo(BLK)

    def pair_block(qb, kw, vw, bias):
        acc, m, l = _softmax_block(_stack_heads(qb, lo), kw, vw, bias)
        return (jnp.where(lo, acc[:BLK], acc[BLK:]), jnp.where(lo, m[:BLK], m[BLK:]),
                jnp.where(lo, l[:BLK], l[BLK:]))

    chunk = 512
    for c in range(SEQ // chunk):
        rows = slice(c * chunk, (c + 1) * chunk)
        qf_ref[rows, :] = q_ref[0, rows, :].astype(F32)
        kf_ref[rows, :] = k_ref[0, rows, :].astype(F32)
        vf_ref[rows, :] = v_ref[0, rows, :].astype(F32)

    n_blk = SEQ // BLK

    def variant(blk, last):
        return jnp.where(blk == 0, 0, jnp.where(blk == last, 2, 1))

    def p0_body(j, carry):
        r0 = pl.multiple_of(j * BLK, BLK)
        ws = pl.multiple_of(jnp.clip(j * BLK - A_HALF, 0, SEQ - A_KEYS), A_HALF)
        var = variant(j, n_blk - 1)
        acc, m, l = pair_block(q_ref[0, pl.ds(r0, BLK), :],
                               k_ref[0, pl.ds(ws, A_KEYS), :], v_ref[0, pl.ds(ws, A_KEYS), :],
                               b12_ref[0, 0, var])
        acc_ref[0, pl.ds(r0, BLK), :] = acc
        m_ref[0, pl.ds(r0, BLK), :] = m
        l_ref[0, pl.ds(r0, BLK), :] = l
        return carry

    lax.fori_loop(0, n_blk, p0_body, 0, unroll=A_UNROLL)

    def deinterleave(dil):
        seg = SEQ // dil
        for r in range(dil):
            dst = slice(r * seg, (r + 1) * seg)
            qd_ref[dst, :] = qf_ref[pl.ds(r, seg, stride=dil), :].astype(BF16)
            kd_ref[dst, :] = kf_ref[pl.ds(r, seg, stride=dil), :].astype(BF16)
            vd_ref[dst, :] = vf_ref[pl.ds(r, seg, stride=dil), :].astype(BF16)

    dil1 = A_PATTERNS[1][1]
    seg1 = SEQ // dil1
    per1 = seg1 // BLK
    deinterleave(dil1)

    def p1_body(j, carry):
        r = j // per1
        blk = j % per1
        r0 = pl.multiple_of(j * BLK, BLK)
        ws = pl.multiple_of(r * seg1 + jnp.clip(blk * BLK - A_HALF, 0, seg1 - A_KEYS), A_HALF)
        var = variant(blk, per1 - 1)
        acc, m, l = pair_block(qd_ref[pl.ds(r0, BLK), :],
                               kd_ref[pl.ds(ws, A_KEYS), :], vd_ref[pl.ds(ws, A_KEYS), :],
                               b12_ref[0, 1, var])
        dst = pl.ds(blk * BLK * dil1 + r, BLK, stride=dil1)
        acc_ref[1, dst, :] = acc
        m_ref[1, dst, :] = m
        l_ref[1, dst, :] = l
        return carry

    lax.fori_loop(0, n_blk, p1_body, 0, unroll=A_UNROLL)

    dil2 = A_PATTERNS[2][1]
    deinterleave(dil2)

    def p2_body(j, carry):
        r0 = pl.multiple_of(j * BLK, BLK)
        acc, m, l = pair_block(qd_ref[pl.ds(r0, BLK), :],
                               kd_ref[pl.ds(r0, BLK), :], vd_ref[pl.ds(r0, BLK), :],
                               b3_ref[0])
        dst = pl.ds(j, BLK, stride=dil2)
        acc_ref[2, dst, :] = acc
        m_ref[2, dst, :] = m
        l_ref[2, dst, :] = l
        return carry

    lax.fori_loop(0, n_blk, p2_body, 0, unroll=A_UNROLL)

    mrows = 256

    def merge_body(c, carry):
        rows = pl.ds(pl.multiple_of(c * mrows, mrows), mrows)
        m0, m1, m2 = m_ref[0, rows, :], m_ref[1, rows, :], m_ref[2, rows, :]
        mx = jnp.maximum(jnp.maximum(m0, m1), m2)
        w0, w1, w2 = jnp.exp2(m0 - mx), jnp.exp2(m1 - mx), jnp.exp2(m2 - mx)
        num = w0 * acc_ref[0, rows, :] + w1 * acc_ref[1, rows, :] + w2 * acc_ref[2, rows, :]
        den = w0 * l_ref[0, rows, :] + w1 * l_ref[1, rows, :] + w2 * l_ref[2, rows, :]
        y = num / den
        o_ref[0, rows, :] = (y * _silu(g_ref[0, rows, :].astype(F32))).astype(BF16)
        return carry

    lax.fori_loop(0, SEQ // mrows, merge_body, 0)


def _mixer_a(proj, bias12, bias3):
    pairs = A_HEADS // 2
    blk = lambda off: pl.BlockSpec((1, SEQ, LANES), lambda hp, b: (b, 0, off + hp))
    return pl.pallas_call(
        _a_kernel,
        grid=(pairs, BATCH),
        in_specs=[
            blk(0), blk(pairs), blk(2 * pairs), blk(3072 // LANES),
            pl.BlockSpec((1, 2, 3, 2 * BLK, A_KEYS), lambda hp, b: (hp, 0, 0, 0, 0)),
            pl.BlockSpec((1, 2 * BLK, BLK), lambda hp, b: (hp, 0, 0)),
        ],
        out_specs=pl.BlockSpec((1, SEQ, LANES), lambda hp, b: (b, 0, hp)),
        out_shape=jax.ShapeDtypeStruct((BATCH, SEQ, A_HEADS * HEAD_DIM), BF16),
        scratch_shapes=[
            pltpu.VMEM((SEQ, LANES), F32), pltpu.VMEM((SEQ, LANES), F32),
            pltpu.VMEM((SEQ, LANES), F32),
            pltpu.VMEM((SEQ, LANES), BF16), pltpu.VMEM((SEQ, LANES), BF16),
            pltpu.VMEM((SEQ, LANES), BF16),
            pltpu.VMEM((3, SEQ, LANES), F32), pltpu.VMEM((3, SEQ, LANES), F32),
            pltpu.VMEM((3, SEQ, LANES), F32),
        ],
        compiler_params=pltpu.CompilerParams(
            dimension_semantics=("arbitrary", "arbitrary"), vmem_limit_bytes=VMEM_LIMIT),
        name="mixer_a_dilated",
    )(proj, proj, proj, proj, bias12, bias3)


def _b_kernel(q_ref, k_ref, v_ref, g_ref, u_ref, lq1_ref, lk1_ref, lq2_ref, lk2_ref,
              sg_ref, o_ref, vt_ref, sa_ref, sb_ref, ma_ref, mb_ref, *, lam_init):
    lo = _lane_lo(B_TQ)
    n_t = SEQ // B_TK
    n_q = SEQ // B_TQ
    lam = (jnp.exp(jnp.sum(lq1_ref[...] * lk1_ref[...], axis=-1, keepdims=True))
           - jnp.exp(jnp.sum(lq2_ref[...] * lk2_ref[...], axis=-1, keepdims=True)) + lam_init)

    eye = jnp.where(lax.broadcasted_iota(jnp.int32, (LANES, LANES), 0)
                    == lax.broadcasted_iota(jnp.int32, (LANES, LANES), 1), 1.0, 0.0).astype(BF16)
    ones_row = jnp.where(lax.broadcasted_iota(jnp.int32, (ONES_ROWS, B_TK), 0) == 0,
                         1.0, 0.0).astype(BF16)
    for t in range(n_t):
        rows = slice(t * B_TK, (t + 1) * B_TK)
        vt_ref[:LANES, rows] = _dot_nt(eye, v_ref[0, rows, :]).astype(BF16)
        vt_ref[LANES:, rows] = ones_row

    def scores(i, s_ref, m_ref):
        r0 = pl.multiple_of(i * B_TQ, B_TQ)
        qst = _stack_heads(q_ref[0, pl.ds(r0, B_TQ), :], lo)
        m = None
        for t in range(n_t):
            rows = slice(t * B_TK, (t + 1) * B_TK)
            off = pl.multiple_of(t * B_TK + (SEQ - B_TQ) - r0, LANES)
            ub = u_ref[0, pl.ds(off, B_TK), :]
            st = _dot_nt(k_ref[0, rows, :], qst)
            s1 = st[:, :B_TQ] + ub
            s2 = st[:, B_TQ:] + ub
            s_ref[rows, :B_TQ] = s1
            s_ref[rows, B_TQ:] = s2
            f = jnp.concatenate([jnp.max(s1, axis=0, keepdims=True),
                                 jnp.max(s2, axis=0, keepdims=True)], axis=1)
            m = f if m is None else jnp.maximum(m, f)
            yield
        m_ref[...] = m

    def finish(i, s_ref, m_ref):
        r0 = pl.multiple_of(i * B_TQ, B_TQ)
        m = m_ref[...]
        acct = None
        for t in range(n_t):
            rows = slice(t * B_TK, (t + 1) * B_TK)
            p = jnp.exp2(s_ref[rows, :] - m).astype(BF16)
            d = jnp.dot(vt_ref[:, rows], p, preferred_element_type=F32)
            acct = d if acct is None else acct + d
            yield
        ot = acct[:LANES] * (1.0 / acct[LANES:LANES + 1])
        outt = ot[:, :B_TQ] - lam * ot[:, B_TQ:]
        ms = jnp.mean(outt * outt, axis=0, keepdims=True)
        yt = outt * lax.rsqrt(ms + EPS) * sg_ref[...] * (1.0 - lam_init)
        gate = _silu(g_ref[0, pl.ds(r0, B_TQ), :].astype(F32))
        o_ref[0, pl.ds(r0, B_TQ), :] = (yt.T * gate).astype(BF16)

    buf_a = (sa_ref, ma_ref)
    buf_b = (sb_ref, mb_ref)
    _interleave(scores(0, *buf_a))

    def body(j, carry):
        _interleave(finish(2 * j, *buf_a), scores(2 * j + 1, *buf_b))
        _interleave(finish(2 * j + 1, *buf_b), scores(2 * j + 2, *buf_a))
        return carry

    lax.fori_loop(0, n_q // 2 - 1, body, 0)
    _interleave(finish(n_q - 2, *buf_a), scores(n_q - 1, *buf_b))
    _interleave(finish(n_q - 1, *buf_b))


def _mixer_b(proj, u, lq1, lk1, lq2, lk2, subln_g, lam_init):
    qb = 3 * A_HEADS * HEAD_DIM // LANES
    kb = qb + B_HEADS
    vb = kb + B_HEADS
    gb = 3072 // LANES + A_HEADS * HEAD_DIM // LANES
    vec = lambda n: pl.BlockSpec((1, n), lambda h, b: (0, 0))
    blk = lambda off: pl.BlockSpec((1, SEQ, LANES), lambda h, b: (b, 0, off + h))
    return pl.pallas_call(
        functools.partial(_b_kernel, lam_init=lam_init),
        grid=(B_HEADS, BATCH),
        in_specs=[
            blk(qb), blk(kb), blk(vb), blk(gb),
            pl.BlockSpec((1, 2 * SEQ - B_TQ, B_TQ), lambda h, b: (h, 0, 0)),
            vec(HEAD_DIM), vec(HEAD_DIM), vec(HEAD_DIM), vec(HEAD_DIM),
            pl.BlockSpec((2 * HEAD_DIM, 1), lambda h, b: (0, 0)),
        ],
        out_specs=pl.BlockSpec((1, SEQ, LANES), lambda h, b: (b, 0, h)),
        out_shape=jax.ShapeDtypeStruct((BATCH, SEQ, B_HEADS * 2 * HEAD_DIM), BF16),
        scratch_shapes=[pltpu.VMEM((LANES + ONES_ROWS, SEQ), BF16),
                        pltpu.VMEM((SEQ, 2 * B_TQ), F32), pltpu.VMEM((SEQ, 2 * B_TQ), F32),
                        pltpu.VMEM((1, 2 * B_TQ), F32), pltpu.VMEM((1, 2 * B_TQ), F32)],
        compiler_params=pltpu.CompilerParams(
            dimension_semantics=("arbitrary", "arbitrary"), vmem_limit_bytes=VMEM_LIMIT),
        name="mixer_b_differential",
    )(proj, proj, proj, proj, u, lq1.reshape(1, -1), lk1.reshape(1, -1),
      lq2.reshape(1, -1), lk2.reshape(1, -1), subln_g.reshape(-1, 1))


def _c_kernel(sink_ref, q_ref, k_ref, v_ref, g_ref, bias_ref, o_ref, ka_ref, vt_ref, *bufs):
    s_refs, m_refs = bufs[:2 * C_GROUP], bufs[2 * C_GROUP:]
    grp = pl.program_id(0)
    half = grp % 2
    lo = _lane_lo(BLK)
    zero = jnp.zeros((BLK, LANES), BF16)
    row = lax.broadcasted_iota(jnp.int32, (LANES, LANES), 0)
    col = lax.broadcasted_iota(jnp.int32, (LANES, LANES), 1)

    chunk = 512
    ones_row = jnp.where(lax.broadcasted_iota(jnp.int32, (ONES_ROWS, chunk), 0) == 0,
                         1.0, 0.0).astype(BF16)
    for a in range(2):
        shift = jnp.where(half == a, 0, HEAD_DIM)
        perm = jnp.where(col == (row + shift) % LANES, 1.0, 0.0).astype(BF16)
        for c in range(SEQ // chunk):
            rows = slice(c * chunk, (c + 1) * chunk)
            ka_ref[a, rows, :] = jnp.dot(k_ref[0, rows, :], perm,
                                         preferred_element_type=F32).astype(BF16)
            vt_ref[a, :LANES, rows] = _dot_nt(perm, v_ref[0, rows, :]).astype(BF16)
            vt_ref[a, LANES:, rows] = ones_row

    n_blk = SEQ // BLK
    rep = C_HEADS // C_KV_HEADS
    left = lax.broadcasted_iota(jnp.int32, (1, 2 * BLK), 1) < BLK
    upper = lax.broadcasted_iota(jnp.int32, (LANES, BLK), 0) < HEAD_DIM

    def window(n):
        return pl.multiple_of(jnp.clip(n * BLK - C_HALF_WINDOW, 0, SEQ - C_KEYS), BLK)

    def scores(n, s_ref, m_ref):
        r0 = pl.multiple_of(n * BLK, BLK)
        ws = window(n)
        var = jnp.where(n == 0, 0, jnp.where(n == n_blk - 1, 2, 1))
        for a in range(2):
            qa = [q_ref[0, pl.ds(r0, BLK), c * LANES:(c + 1) * LANES] for c in range(rep // 2)]
            qm = jnp.concatenate(
                [jnp.where(lo, q, zero) if a == 0 else jnp.where(lo, zero, q) for q in qa], axis=0)
            st = _dot_nt(ka_ref[a, pl.ds(ws, C_KEYS), :], qm) + bias_ref[0, a, var]
            s_ref[a] = st
            m_ref[a] = jnp.max(st, axis=0, keepdims=True)
            yield

    def finish(n, s_ref, m_ref):
        r0 = pl.multiple_of(n * BLK, BLK)
        ws = window(n)
        yts = []
        for a in range(2):
            m = m_ref[a]
            p = jnp.exp2(s_ref[a] - m).astype(BF16)
            acct = jnp.dot(vt_ref[a, :, pl.ds(ws, C_KEYS)], p, preferred_element_type=F32)
            l = acct[LANES:LANES + 1]
            sk = jnp.where(left, sink_ref[grp * rep + a], sink_ref[grp * rep + a + 2]) * LOG2E
            mx = jnp.maximum(m, sk)
            e = jnp.exp2(m - mx)
            yts.append(acct[:LANES] * (e / (l * e + jnp.exp2(sk - mx))))
            yield
        for c in range(rep // 2):
            cols = slice(c * LANES, (c + 1) * LANES)
            y = jnp.where(upper, yts[0][:, cols], yts[1][:, cols]).T
            gate = _silu(g_ref[0, pl.ds(r0, BLK), cols].astype(F32))
            o_ref[0, pl.ds(r0, BLK), cols] = (y * gate).astype(BF16)

    bufs = [(s_refs[i], m_refs[i]) for i in range(2 * C_GROUP)]
    set_a, set_b = bufs[:C_GROUP], bufs[C_GROUP:]
    n_grp = n_blk // C_GROUP

    def scores_of(g, bset):
        return [scores(g * C_GROUP + i, *bset[i]) for i in range(C_GROUP)]

    def finish_of(g, bset):
        return [finish(g * C_GROUP + i, *bset[i]) for i in range(C_GROUP)]

    _interleave(*scores_of(0, set_a))

    def body(j, carry):
        _interleave(*finish_of(2 * j, set_a), *scores_of(2 * j + 1, set_b))
        _interleave(*finish_of(2 * j + 1, set_b), *scores_of(2 * j + 2, set_a))
        return carry

    lax.fori_loop(0, n_grp // 2 - 1, body, 0)
    _interleave(*finish_of(n_grp - 2, set_a), *scores_of(n_grp - 1, set_b))
    _interleave(*finish_of(n_grp - 1, set_b))


def _mixer_c(proj, sink, bias):
    width = C_HEADS // C_KV_HEADS * HEAD_DIM
    kb = C_HEADS * HEAD_DIM // LANES
    vb = kb + C_KV_HEADS * HEAD_DIM // LANES
    gb = (C_HEADS + 2 * C_KV_HEADS) * HEAD_DIM // width
    return pl.pallas_call(
        _c_kernel,
        grid=(C_KV_HEADS, BATCH),
        in_specs=[
            pl.BlockSpec(memory_space=pltpu.SMEM),
            pl.BlockSpec((1, SEQ, width), lambda g, b: (b, 0, g)),
            pl.BlockSpec((1, SEQ, LANES), lambda g, b: (b, 0, kb + g // 2)),
            pl.BlockSpec((1, SEQ, LANES), lambda g, b: (b, 0, vb + g // 2)),
            pl.BlockSpec((1, SEQ, width), lambda g, b: (b, 0, gb + g)),
            pl.BlockSpec((1, 2, 3, C_KEYS, 2 * BLK), lambda g, b: (g, 0, 0, 0, 0)),
        ],
        out_specs=pl.BlockSpec((1, SEQ, width), lambda g, b: (b, 0, g)),
        out_shape=jax.ShapeDtypeStruct((BATCH, SEQ, C_HEADS * HEAD_DIM), BF16),
        scratch_shapes=[pltpu.VMEM((2, SEQ, LANES), BF16),
                        pltpu.VMEM((2, LANES + ONES_ROWS, SEQ), BF16)]
        + [pltpu.VMEM((2, C_KEYS, 2 * BLK), F32)] * (2 * C_GROUP)
        + [pltpu.VMEM((2, 1, 2 * BLK), F32)] * (2 * C_GROUP),
        compiler_params=pltpu.CompilerParams(
            dimension_semantics=("arbitrary", "arbitrary"), vmem_limit_bytes=VMEM_LIMIT),
        name="mixer_c_windowed",
    )(sink, proj, proj, proj, proj, bias)


@functools.lru_cache(maxsize=None)
def _bias_tables():
    def stack_pairs(t, first, second):
        return np.concatenate([t[first], t[second]], axis=2)

    sa = _alibi_slopes(A_HEADS)
    offs = (0, A_HALF, 2 * A_HALF)
    ev, od = slice(0, None, 2), slice(1, None, 2)
    b12 = np.stack(
        [stack_pairs(_band_bias(sa, dil, A_HALF, BLK, A_KEYS, offs), ev, od)
         for _, dil in A_PATTERNS[:2]], axis=1)
    b3 = stack_pairs(_band_bias(sa, A_PATTERNS[2][1], A_HALF, BLK, BLK, (0,)), ev, od)[:, 0]
    sc = _alibi_slopes(C_HEADS)
    bc_heads = _band_bias(sc, 1, C_HALF_WINDOW, BLK, C_KEYS,
                          (0, C_HALF_WINDOW, 2 * C_HALF_WINDOW))
    rep = C_HEADS // C_KV_HEADS
    bc = np.stack([stack_pairs(bc_heads, slice(a, None, rep), slice(a + 2, None, rep))
                   for a in range(2)], axis=1)
    bc = np.ascontiguousarray(np.swapaxes(bc, -1, -2))
    sb = _alibi_slopes(B_HEADS)
    il = np.arange(B_TQ)[None, :]
    uu = np.arange(2 * SEQ - B_TQ)[:, None]
    dist = np.abs(il - uu + (SEQ - B_TQ)).astype(np.float32)
    ub = -sb[:, None, None] * dist[None] * np.float32(LOG2E)
    return b12, b3, bc, ub.astype(np.float32)


def kernel(x, c, ada_w, ada_b, norm_g, ab_w_in, ab_w_out, diff_lq1, diff_lk1, diff_lq2,
           diff_lk2, diff_subln_g, c_w_in, c_w_out, c_sink, final_g):
    b12, b3, bc, ub = (jnp.asarray(t) for t in _bias_tables())
    mod = _modulation(c, ada_w, ada_b)
    a_w = A_HEADS * HEAD_DIM
    q_even = ((0, a_w), (3 * a_w, 3 * a_w + B_HEADS * 2 * HEAD_DIM))
    q_odd = ((0, C_HEADS * HEAD_DIM),)
    for layer in range(DEPTH):
        shift = mod[layer, :, :D_MODEL]
        scale = mod[layer, :, D_MODEL:2 * D_MODEL]
        gate = mod[layer, :, 2 * D_MODEL:]
        j = layer // 2
        final = final_g if layer == DEPTH - 1 else None
        if layer % 2 == 0:
            lam_init = 0.8 - 0.6 * math.exp(-0.3 * layer)
            proj = _in_proj(x, norm_g[layer], scale, shift, ab_w_in[j].astype(BF16), q_even)
            za = _mixer_a(proj, b12, b3)
            zb = _mixer_b(proj, ub, diff_lq1[j], diff_lk1[j], diff_lq2[j], diff_lk2[j],
                          diff_subln_g[j], lam_init)
            x = _out_proj(x, gate, [za, zb], ab_w_out[j].astype(BF16), final)
        else:
            proj = _in_proj(x, norm_g[layer], scale, shift, c_w_in[j].astype(BF16), q_odd)
            zc = _mixer_c(proj, c_sink[j], bc)
            x = _out_proj(x, gate, [zc], c_w_out[j].astype(BF16), final)
    return x
```

```python
import functools
import math

import numpy as np
import jax
import jax.numpy as jnp
from jax import lax
from jax.experimental import pallas as pl
from jax.experimental.pallas import tpu as pltpu

D_MODEL = 1024
BATCH = 8
SEQ = 2048
DEPTH = 4
HEAD_DIM = 64
LANES = 128
A_HEADS = 8
A_PATTERNS = ((128, 1), (512, 4), (2048, 16))
A_HALF = 64
B_HEADS = 4
C_HEADS = 16
C_KV_HEADS = 4
C_HALF_WINDOW = 128
EVEN_IN = 4096
ODD_IN = 2560
EPS = 1e-6
NEG_INF = -1e30
LOG2E = math.log2(math.e)
Q_FOLD = HEAD_DIM ** -0.5 * LOG2E
ONES_ROWS = 16

F32 = jnp.float32
BF16 = jnp.bfloat16

ROW_TILE = 512
COL_TILE = 512
BLK = 128
A_KEYS = BLK + 2 * A_HALF
C_KEYS = BLK + 2 * C_HALF_WINDOW
A_UNROLL = 8
C_GROUP = 2
B_TQ = 256
B_TK = 256
VMEM_LIMIT = 56 * 1024 * 1024


def _silu(t):
    return t * (1.0 / (1.0 + jnp.exp(-t)))


def _dot_nt(a, b):
    return lax.dot_general(a, b, (((1,), (1,)), ((), ())), preferred_element_type=F32)


def _alibi_slopes(n):
    return (2.0 ** (-8.0 * np.arange(1, n + 1, dtype=np.float32) / n)).astype(np.float32)


def _band_bias(slopes, spacing, half, tq, tk, offsets):
    i = np.arange(tq)[:, None]
    c = np.arange(tk)[None, :]
    out = np.empty((len(slopes), len(offsets), tq, tk), np.float32)
    for v, off in enumerate(offsets):
        rel = np.abs(c - (i + off))
        dist = (rel * spacing).astype(np.float32)
        for h, m in enumerate(slopes):
            out[h, v] = np.where(rel <= half, -m * dist * np.float32(LOG2E), np.float32(NEG_INF))
    return out


def _mod_kernel(c_ref, w_ref, b_ref, o_ref):
    cs = _silu(c_ref[...])
    o_ref[0] = jnp.dot(cs, w_ref[0], preferred_element_type=F32,
                       precision=lax.Precision.HIGHEST) + b_ref[0]


def _modulation(c, ada_w, ada_b):
    nblk = 3 * D_MODEL // D_MODEL
    return pl.pallas_call(
        _mod_kernel,
        grid=(DEPTH, nblk),
        in_specs=[
            pl.BlockSpec((BATCH, D_MODEL), lambda l, j: (0, 0)),
            pl.BlockSpec((1, D_MODEL, D_MODEL), lambda l, j: (l, 0, j)),
            pl.BlockSpec((1, 1, D_MODEL), lambda l, j: (l, 0, j)),
        ],
        out_specs=pl.BlockSpec((1, BATCH, D_MODEL), lambda l, j: (l, 0, j)),
        out_shape=jax.ShapeDtypeStruct((DEPTH, BATCH, 3 * D_MODEL), F32),
        name="adaln_mod",
    )(c, ada_w, ada_b.reshape(DEPTH, 1, 3 * D_MODEL))


def _rms(x, g):
    ms = jnp.mean(x * x, axis=-1, keepdims=True)
    return x * lax.rsqrt(ms + EPS) * g


def _proj_kernel(*refs, n_z, final, n_out, q_cols):
    refs = iter(refs)
    x_ref = next(refs)
    if n_z:
        gate_ref = next(refs)
        z_refs = [next(refs) for _ in range(n_z)]
        wo_ref = next(refs)
    if final:
        fg_ref = next(refs)
    if n_out:
        g_ref, sc_ref, sh_ref, wi_ref = (next(refs) for _ in range(4))
    if n_z:
        xo_ref = next(refs)
    if n_out:
        proj_ref, h_ref = next(refs), next(refs)

    xn = x_ref[0]
    if n_z:
        kz = D_MODEL // n_z
        acc = jnp.dot(z_refs[0][0], wo_ref[0:kz, :], preferred_element_type=F32)
        for i in range(1, n_z):
            acc = acc + jnp.dot(z_refs[i][0], wo_ref[i * kz:(i + 1) * kz, :],
                                preferred_element_type=F32)
        xn = xn + gate_ref[0] * acc
        xo_ref[0] = _rms(xn, fg_ref[...]) if final else xn
    if n_out:
        h_ref[...] = (_rms(xn, g_ref[...]) * (1.0 + sc_ref[0]) + sh_ref[0]).astype(BF16)
        for j in range(n_out // COL_TILE):
            lo, hi = j * COL_TILE, (j + 1) * COL_TILE
            acc = jnp.dot(h_ref[...], wi_ref[:, lo:hi], preferred_element_type=F32)
            if any(q0 <= lo and hi <= q1 for q0, q1 in q_cols):
                acc = acc * Q_FOLD
            proj_ref[0, :, lo:hi] = acc.astype(BF16)


def _proj(x, out_stage=None, final_g=None, in_stage=None):
    row = pl.BlockSpec((1, ROW_TILE, D_MODEL), lambda b, i: (b, i, 0))
    per_batch = pl.BlockSpec((1, 1, D_MODEL), lambda b, i: (b, 0, 0))
    vec = pl.BlockSpec((1, D_MODEL), lambda b, i: (0, 0))
    whole = lambda shape: pl.BlockSpec(shape, lambda b, i: (0, 0))
    in_specs, args, out_specs, out_shape, scratch = [row], [x], [], [], []
    n_z = n_out = 0
    q_cols = ()
    assert final_g is None or out_stage is not None
    if out_stage is not None:
        gate, zs, w_out = out_stage
        n_z = len(zs)
        in_specs += [per_batch]
        in_specs += [pl.BlockSpec((1, ROW_TILE, D_MODEL // n_z), lambda b, i: (b, i, 0))] * n_z
        in_specs += [whole((D_MODEL, D_MODEL))]
        args += [gate.reshape(BATCH, 1, D_MODEL), *zs, w_out]
        out_specs += [row]
        out_shape += [jax.ShapeDtypeStruct((BATCH, SEQ, D_MODEL), F32)]
    if final_g is not None:
        in_specs += [vec]
        args += [final_g.reshape(1, D_MODEL)]
    if in_stage is not None:
        g, scale, shift, w_in, q_cols = in_stage
        n_out = w_in.shape[1]
        assert all(q0 % COL_TILE == 0 and q1 % COL_TILE == 0 for q0, q1 in q_cols)
        in_specs += [vec, per_batch, per_batch, whole((D_MODEL, n_out))]
        args += [g.reshape(1, D_MODEL), scale.reshape(BATCH, 1, D_MODEL),
                 shift.reshape(BATCH, 1, D_MODEL), w_in]
        out_specs += [pl.BlockSpec((1, ROW_TILE, n_out), lambda b, i: (b, i, 0))]
        out_shape += [jax.ShapeDtypeStruct((BATCH, SEQ, n_out), BF16)]
        scratch += [pltpu.VMEM((ROW_TILE, D_MODEL), BF16)]
    outs = pl.pallas_call(
        functools.partial(_proj_kernel, n_z=n_z, final=final_g is not None, n_out=n_out,
                          q_cols=q_cols),
        grid=(BATCH, SEQ // ROW_TILE),
        in_specs=in_specs,
        out_specs=out_specs,
        out_shape=out_shape,
        scratch_shapes=scratch,
        compiler_params=pltpu.CompilerParams(
            dimension_semantics=("arbitrary", "arbitrary"), vmem_limit_bytes=VMEM_LIMIT),
        name="proj",
    )(*args)
    outs = list(outs)
    x_new = outs.pop(0) if out_stage is not None else None
    proj = outs.pop(0) if in_stage is not None else None
    return x_new, proj


def _lane_lo(rows):
    return lax.broadcasted_iota(jnp.int32, (rows, LANES), 1) < HEAD_DIM


def _softmax_block(qm, kw, vw, bias):
    s = _dot_nt(qm, kw) + bias
    m = jnp.max(s, axis=-1, keepdims=True)
    p = jnp.exp2(s - m)
    l = jnp.sum(p, axis=-1, keepdims=True)
    acc = jnp.dot(p.astype(BF16), vw, preferred_element_type=F32)
    return acc, m, l


def _interleave(*stages):
    live = list(stages)
    while live:
        for g in list(live):
            try:
                next(g)
            except StopIteration:
                live.remove(g)


def _stack_heads(qb, lo):
    zero = jnp.zeros_like(qb)
    return jnp.concatenate([jnp.where(lo, qb, zero), jnp.where(lo, zero, qb)], axis=0)


def _a_kernel(q_ref, k_ref, v_ref, g_ref, b12_ref, b3_ref, o_ref,
              qf_ref, kf_ref, vf_ref, q4_ref, k4_ref, v4_ref, qd_ref, kd_ref, vd_ref,
              acc_ref, m_ref, l_ref):
    lo = _lane_lo(BLK)

    def pair_block(qb, kw, vw, bias):
        acc, m, l = _softmax_block(_stack_heads(qb, lo), kw, vw, bias)
        return (jnp.where(lo, acc[:BLK], acc[BLK:]), jnp.where(lo, m[:BLK], m[BLK:]),
                jnp.where(lo, l[:BLK], l[BLK:]))

    chunk = 512
    for c in range(SEQ // chunk):
        rows = slice(c * chunk, (c + 1) * chunk)
        qf_ref[rows, :] = q_ref[0, rows, :].astype(F32)
        kf_ref[rows, :] = k_ref[0, rows, :].astype(F32)
        vf_ref[rows, :] = v_ref[0, rows, :].astype(F32)

    n_blk = SEQ // BLK

    def variant(blk, last):
        return jnp.where(blk == 0, 0, jnp.where(blk == last, 2, 1))

    def p0_body(j, carry):
        r0 = pl.multiple_of(j * BLK, BLK)
        ws = pl.multiple_of(jnp.clip(j * BLK - A_HALF, 0, SEQ - A_KEYS), A_HALF)
        var = variant(j, n_blk - 1)
        acc, m, l = pair_block(q_ref[0, pl.ds(r0, BLK), :],
                               k_ref[0, pl.ds(ws, A_KEYS), :], v_ref[0, pl.ds(ws, A_KEYS), :],
                               b12_ref[0, 0, var])
        acc_ref[0, pl.ds(r0, BLK), :] = acc
        m_ref[0, pl.ds(r0, BLK), :] = m
        l_ref[0, pl.ds(r0, BLK), :] = l
        return carry

    lax.fori_loop(0, n_blk, p0_body, 0, unroll=A_UNROLL)

    dil1 = A_PATTERNS[1][1]
    seg1 = SEQ // dil1
    per1 = seg1 // BLK
    streams = ((qf_ref, q4_ref, qd_ref), (kf_ref, k4_ref, kd_ref), (vf_ref, v4_ref, vd_ref))
    for r in range(dil1):
        dst = slice(r * seg1, (r + 1) * seg1)
        for src_ref, mid_ref, dst_ref in streams:
            t = src_ref[pl.ds(r, seg1, stride=dil1), :]
            mid_ref[dst, :] = t
            dst_ref[dst, :] = t.astype(BF16)

    def p1_body(j, carry):
        r = j // per1
        blk = j % per1
        r0 = pl.multiple_of(j * BLK, BLK)
        ws = pl.multiple_of(r * seg1 + jnp.clip(blk * BLK - A_HALF, 0, seg1 - A_KEYS), A_HALF)
        var = variant(blk, per1 - 1)
        acc, m, l = pair_block(qd_ref[pl.ds(r0, BLK), :],
                               kd_ref[pl.ds(ws, A_KEYS), :], vd_ref[pl.ds(ws, A_KEYS), :],
                               b12_ref[0, 1, var])
        dst = pl.ds(blk * BLK * dil1 + r, BLK, stride=dil1)
        acc_ref[1, dst, :] = acc
        m_ref[1, dst, :] = m
        l_ref[1, dst, :] = l
        return carry

    lax.fori_loop(0, n_blk, p1_body, 0, unroll=A_UNROLL)

    dil2 = A_PATTERNS[2][1]
    sub = dil2 // dil1
    assert sub * dil1 == dil2
    seg2 = SEQ // dil2
    for r in range(dil1):
        for r2 in range(sub):
            res = dil1 * r2 + r
            dst = slice(res * seg2, (res + 1) * seg2)
            for _, mid_ref, dst_ref in streams:
                dst_ref[dst, :] = mid_ref[pl.ds(r * seg1 + r2, seg2, stride=sub), :].astype(BF16)

    def p2_body(j, carry):
        r0 = pl.multiple_of(j * BLK, BLK)
        acc, m, l = pair_block(qd_ref[pl.ds(r0, BLK), :],
                               kd_ref[pl.ds(r0, BLK), :], vd_ref[pl.ds(r0, BLK), :],
                               b3_ref[0])
        dst = pl.ds(j, BLK, stride=dil2)
        acc_ref[2, dst, :] = acc
        m_ref[2, dst, :] = m
        l_ref[2, dst, :] = l
        return carry

    lax.fori_loop(0, n_blk, p2_body, 0, unroll=A_UNROLL)

    mrows = 256

    def merge_body(c, carry):
        rows = pl.ds(pl.multiple_of(c * mrows, mrows), mrows)
        m0, m1, m2 = m_ref[0, rows, :], m_ref[1, rows, :], m_ref[2, rows, :]
        mx = jnp.maximum(jnp.maximum(m0, m1), m2)
        w0, w1, w2 = jnp.exp2(m0 - mx), jnp.exp2(m1 - mx), jnp.exp2(m2 - mx)
        num = w0 * acc_ref[0, rows, :] + w1 * acc_ref[1, rows, :] + w2 * acc_ref[2, rows, :]
        den = w0 * l_ref[0, rows, :] + w1 * l_ref[1, rows, :] + w2 * l_ref[2, rows, :]
        y = num / den
        o_ref[0, rows, :] = (y * _silu(g_ref[0, rows, :].astype(F32))).astype(BF16)
        return carry

    lax.fori_loop(0, SEQ // mrows, merge_body, 0)


def _mixer_a(proj, bias12, bias3):
    pairs = A_HEADS // 2
    blk = lambda off: pl.BlockSpec((1, SEQ, LANES), lambda hp, b: (b, 0, off + hp))
    return pl.pallas_call(
        _a_kernel,
        grid=(pairs, BATCH),
        in_specs=[
            blk(0), blk(pairs), blk(2 * pairs), blk(3072 // LANES),
            pl.BlockSpec((1, 2, 3, 2 * BLK, A_KEYS), lambda hp, b: (hp, 0, 0, 0, 0)),
            pl.BlockSpec((1, 2 * BLK, BLK), lambda hp, b: (hp, 0, 0)),
        ],
        out_specs=pl.BlockSpec((1, SEQ, LANES), lambda hp, b: (b, 0, hp)),
        out_shape=jax.ShapeDtypeStruct((BATCH, SEQ, A_HEADS * HEAD_DIM), BF16),
        scratch_shapes=[
            pltpu.VMEM((SEQ, LANES), F32), pltpu.VMEM((SEQ, LANES), F32),
            pltpu.VMEM((SEQ, LANES), F32), pltpu.VMEM((SEQ, LANES), F32),
            pltpu.VMEM((SEQ, LANES), F32),
            pltpu.VMEM((SEQ, LANES), F32),
            pltpu.VMEM((SEQ, LANES), BF16), pltpu.VMEM((SEQ, LANES), BF16),
            pltpu.VMEM((SEQ, LANES), BF16),
            pltpu.VMEM((3, SEQ, LANES), F32), pltpu.VMEM((3, SEQ, LANES), F32),
            pltpu.VMEM((3, SEQ, LANES), F32),
        ],
        compiler_params=pltpu.CompilerParams(
            dimension_semantics=("arbitrary", "arbitrary"), vmem_limit_bytes=VMEM_LIMIT),
        name="mixer_a_dilated",
    )(proj, proj, proj, proj, bias12, bias3)


def _b_kernel(q_ref, k_ref, v_ref, g_ref, u_ref, lq1_ref, lk1_ref, lq2_ref, lk2_ref,
              sg_ref, o_ref, vt_ref, sa_ref, sb_ref, ma_ref, mb_ref, *, lam_init):
    lo = _lane_lo(B_TQ)
    n_t = SEQ // B_TK
    n_q = SEQ // B_TQ
    lam = (jnp.exp(jnp.sum(lq1_ref[...] * lk1_ref[...], axis=-1, keepdims=True))
           - jnp.exp(jnp.sum(lq2_ref[...] * lk2_ref[...], axis=-1, keepdims=True)) + lam_init)

    eye = jnp.where(lax.broadcasted_iota(jnp.int32, (LANES, LANES), 0)
                    == lax.broadcasted_iota(jnp.int32, (LANES, LANES), 1), 1.0, 0.0).astype(BF16)
    ones_row = jnp.where(lax.broadcasted_iota(jnp.int32, (ONES_ROWS, B_TK), 0) == 0,
                         1.0, 0.0).astype(BF16)
    for t in range(n_t):
        rows = slice(t * B_TK, (t + 1) * B_TK)
        vt_ref[:LANES, rows] = _dot_nt(eye, v_ref[0, rows, :]).astype(BF16)
        vt_ref[LANES:, rows] = ones_row

    def scores(i, s_ref, m_ref):
        r0 = pl.multiple_of(i * B_TQ, B_TQ)
        qst = _stack_heads(q_ref[0, pl.ds(r0, B_TQ), :], lo)
        m = None
        for t in range(n_t):
            rows = slice(t * B_TK, (t + 1) * B_TK)
            off = pl.multiple_of(t * B_TK + (SEQ - B_TQ) - r0, LANES)
            ub = u_ref[0, pl.ds(off, B_TK), :]
            st = _dot_nt(k_ref[0, rows, :], qst)
            s1 = st[:, :B_TQ] + ub
            s2 = st[:, B_TQ:] + ub
            s_ref[rows, :B_TQ] = s1
            s_ref[rows, B_TQ:] = s2
            f = jnp.concatenate([jnp.max(s1, axis=0, keepdims=True),
                                 jnp.max(s2, axis=0, keepdims=True)], axis=1)
            m = f if m is None else jnp.maximum(m, f)
            yield
        m_ref[...] = m

    def finish(i, s_ref, m_ref):
        r0 = pl.multiple_of(i * B_TQ, B_TQ)
        m = m_ref[...]
        acct = None
        for t in range(n_t):
            rows = slice(t * B_TK, (t + 1) * B_TK)
            p = jnp.exp2(s_ref[rows, :] - m).astype(BF16)
            d = jnp.dot(vt_ref[:, rows], p, preferred_element_type=F32)
            acct = d if acct is None else acct + d
            yield
        ot = acct[:LANES] * (1.0 / acct[LANES:LANES + 1])
        outt = ot[:, :B_TQ] - lam * ot[:, B_TQ:]
        ms = jnp.mean(outt * outt, axis=0, keepdims=True)
        yt = outt * lax.rsqrt(ms + EPS) * sg_ref[...] * (1.0 - lam_init)
        gate = _silu(g_ref[0, pl.ds(r0, B_TQ), :].astype(F32))
        o_ref[0, pl.ds(r0, B_TQ), :] = (yt.T * gate).astype(BF16)

    buf_a = (sa_ref, ma_ref)
    buf_b = (sb_ref, mb_ref)
    _interleave(scores(0, *buf_a))

    def body(j, carry):
        _interleave(finish(2 * j, *buf_a), scores(2 * j + 1, *buf_b))
        _interleave(finish(2 * j + 1, *buf_b), scores(2 * j + 2, *buf_a))
        return carry

    lax.fori_loop(0, n_q // 2 - 1, body, 0)
    _interleave(finish(n_q - 2, *buf_a), scores(n_q - 1, *buf_b))
    _interleave(finish(n_q - 1, *buf_b))


def _mixer_b(proj, u, lq1, lk1, lq2, lk2, subln_g, lam_init):
    qb = 3 * A_HEADS * HEAD_DIM // LANES
    kb = qb + B_HEADS
    vb = kb + B_HEADS
    gb = 3072 // LANES + A_HEADS * HEAD_DIM // LANES
    vec = lambda n: pl.BlockSpec((1, n), lambda h, b: (0, 0))
    blk = lambda off: pl.BlockSpec((1, SEQ, LANES), lambda h, b: (b, 0, off + h))
    return pl.pallas_call(
        functools.partial(_b_kernel, lam_init=lam_init),
        grid=(B_HEADS, BATCH),
        in_specs=[
            blk(qb), blk(kb), blk(vb), blk(gb),
            pl.BlockSpec((1, 2 * SEQ - B_TQ, B_TQ), lambda h, b: (h, 0, 0)),
            vec(HEAD_DIM), vec(HEAD_DIM), vec(HEAD_DIM), vec(HEAD_DIM),
            pl.BlockSpec((2 * HEAD_DIM, 1), lambda h, b: (0, 0)),
        ],
        out_specs=pl.BlockSpec((1, SEQ, LANES), lambda h, b: (b, 0, h)),
        out_shape=jax.ShapeDtypeStruct((BATCH, SEQ, B_HEADS * 2 * HEAD_DIM), BF16),
        scratch_shapes=[pltpu.VMEM((LANES + ONES_ROWS, SEQ), BF16),
                        pltpu.VMEM((SEQ, 2 * B_TQ), F32), pltpu.VMEM((SEQ, 2 * B_TQ), F32),
                        pltpu.VMEM((1, 2 * B_TQ), F32), pltpu.VMEM((1, 2 * B_TQ), F32)],
        compiler_params=pltpu.CompilerParams(
            dimension_semantics=("arbitrary", "arbitrary"), vmem_limit_bytes=VMEM_LIMIT),
        name="mixer_b_differential",
    )(proj, proj, proj, proj, u, lq1.reshape(1, -1), lk1.reshape(1, -1),
      lq2.reshape(1, -1), lk2.reshape(1, -1), subln_g.reshape(-1, 1))


def _c_kernel(sink_ref, q_ref, k_ref, v_ref, g_ref, bias_ref, o_ref, ka_ref, vt_ref, *bufs):
    s_refs, m_refs = bufs[:2 * C_GROUP], bufs[2 * C_GROUP:]
    grp = pl.program_id(0)
    half = grp % 2
    lo = _lane_lo(BLK)
    zero = jnp.zeros((BLK, LANES), BF16)
    row = lax.broadcasted_iota(jnp.int32, (LANES, LANES), 0)
    col = lax.broadcasted_iota(jnp.int32, (LANES, LANES), 1)

    chunk = 512
    ones_row = jnp.where(lax.broadcasted_iota(jnp.int32, (ONES_ROWS, chunk), 0) == 0,
                         1.0, 0.0).astype(BF16)
    for a in range(2):
        shift = jnp.where(half == a, 0, HEAD_DIM)
        perm = jnp.where(col == (row + shift) % LANES, 1.0, 0.0).astype(BF16)
        for c in range(SEQ // chunk):
            rows = slice(c * chunk, (c + 1) * chunk)
            ka_ref[a, rows, :] = jnp.dot(k_ref[0, rows, :], perm,
                                         preferred_element_type=F32).astype(BF16)
            vt_ref[a, :LANES, rows] = _dot_nt(perm, v_ref[0, rows, :]).astype(BF16)
            vt_ref[a, LANES:, rows] = ones_row

    n_blk = SEQ // BLK
    rep = C_HEADS // C_KV_HEADS
    left = lax.broadcasted_iota(jnp.int32, (1, 2 * BLK), 1) < BLK
    upper = lax.broadcasted_iota(jnp.int32, (LANES, BLK), 0) < HEAD_DIM

    def window(n):
        return pl.multiple_of(jnp.clip(n * BLK - C_HALF_WINDOW, 0, SEQ - C_KEYS), BLK)

    def scores(n, s_ref, m_ref):
        r0 = pl.multiple_of(n * BLK, BLK)
        ws = window(n)
        var = jnp.where(n == 0, 0, jnp.where(n == n_blk - 1, 2, 1))
        for a in range(2):
            qa = [q_ref[0, pl.ds(r0, BLK), c * LANES:(c + 1) * LANES] for c in range(rep // 2)]
            qm = jnp.concatenate(
                [jnp.where(lo, q, zero) if a == 0 else jnp.where(lo, zero, q) for q in qa], axis=0)
            st = _dot_nt(ka_ref[a, pl.ds(ws, C_KEYS), :], qm) + bias_ref[0, a, var]
            s_ref[a] = st
            m_ref[a] = jnp.max(st, axis=0, keepdims=True)
            yield

    def finish(n, s_ref, m_ref):
        r0 = pl.multiple_of(n * BLK, BLK)
        ws = window(n)
        yts = []
        for a in range(2):
            m = m_ref[a]
            p = jnp.exp2(s_ref[a] - m).astype(BF16)
            acct = jnp.dot(vt_ref[a, :, pl.ds(ws, C_KEYS)], p, preferred_element_type=F32)
            l = acct[LANES:LANES + 1]
            sk = jnp.where(left, sink_ref[grp * rep + a], sink_ref[grp * rep + a + 2]) * LOG2E
            mx = jnp.maximum(m, sk)
            e = jnp.exp2(m - mx)
            yts.append(acct[:LANES] * (e / (l * e + jnp.exp2(sk - mx))))
            yield
        for c in range(rep // 2):
            cols = slice(c * LANES, (c + 1) * LANES)
            y = jnp.where(upper, yts[0][:, cols], yts[1][:, cols]).T
            gate = _silu(g_ref[0, pl.ds(r0, BLK), cols].astype(F32))
            o_ref[0, pl.ds(r0, BLK), cols] = (y * gate).astype(BF16)

    bufs = [(s_refs[i], m_refs[i]) for i in range(2 * C_GROUP)]
    set_a, set_b = bufs[:C_GROUP], bufs[C_GROUP:]
    n_grp = n_blk // C_GROUP

    def scores_of(g, bset):
        return [scores(g * C_GROUP + i, *bset[i]) for i in range(C_GROUP)]

    def finish_of(g, bset):
        return [finish(g * C_GROUP + i, *bset[i]) for i in range(C_GROUP)]

    _interleave(*scores_of(0, set_a))

    def body(j, carry):
        _interleave(*finish_of(2 * j, set_a), *scores_of(2 * j + 1, set_b))
        _interleave(*finish_of(2 * j + 1, set_b), *scores_of(2 * j + 2, set_a))
        return carry

    lax.fori_loop(0, n_grp // 2 - 1, body, 0)
    _interleave(*finish_of(n_grp - 2, set_a), *scores_of(n_grp - 1, set_b))
    _interleave(*finish_of(n_grp - 1, set_b))


def _mixer_c(proj, sink, bias):
    width = C_HEADS // C_KV_HEADS * HEAD_DIM
    kb = C_HEADS * HEAD_DIM // LANES
    vb = kb + C_KV_HEADS * HEAD_DIM // LANES
    gb = (C_HEADS + 2 * C_KV_HEADS) * HEAD_DIM // width
    return pl.pallas_call(
        _c_kernel,
        grid=(C_KV_HEADS, BATCH),
        in_specs=[
            pl.BlockSpec(memory_space=pltpu.SMEM),
            pl.BlockSpec((1, SEQ, width), lambda g, b: (b, 0, g)),
            pl.BlockSpec((1, SEQ, LANES), lambda g, b: (b, 0, kb + g // 2)),
            pl.BlockSpec((1, SEQ, LANES), lambda g, b: (b, 0, vb + g // 2)),
            pl.BlockSpec((1, SEQ, width), lambda g, b: (b, 0, gb + g)),
            pl.BlockSpec((1, 2, 3, C_KEYS, 2 * BLK), lambda g, b: (g, 0, 0, 0, 0)),
        ],
        out_specs=pl.BlockSpec((1, SEQ, width), lambda g, b: (b, 0, g)),
        out_shape=jax.ShapeDtypeStruct((BATCH, SEQ, C_HEADS * HEAD_DIM), BF16),
        scratch_shapes=[pltpu.VMEM((2, SEQ, LANES), BF16),
                        pltpu.VMEM((2, LANES + ONES_ROWS, SEQ), BF16)]
        + [pltpu.VMEM((2, C_KEYS, 2 * BLK), F32)] * (2 * C_GROUP)
        + [pltpu.VMEM((2, 1, 2 * BLK), F32)] * (2 * C_GROUP),
        compiler_params=pltpu.CompilerParams(
            dimension_semantics=("arbitrary", "arbitrary"), vmem_limit_bytes=VMEM_LIMIT),
        name="mixer_c_windowed",
    )(sink, proj, proj, proj, proj, bias)


@functools.lru_cache(maxsize=None)
def _bias_tables():
    def stack_pairs(t, first, second):
        return np.concatenate([t[first], t[second]], axis=2)

    sa = _alibi_slopes(A_HEADS)
    offs = (0, A_HALF, 2 * A_HALF)
    ev, od = slice(0, None, 2), slice(1, None, 2)
    b12 = np.stack(
        [stack_pairs(_band_bias(sa, dil, A_HALF, BLK, A_KEYS, offs), ev, od)
         for _, dil in A_PATTERNS[:2]], axis=1)
    b3 = stack_pairs(_band_bias(sa, A_PATTERNS[2][1], A_HALF, BLK, BLK, (0,)), ev, od)[:, 0]
    sc = _alibi_slopes(C_HEADS)
    bc_heads = _band_bias(sc, 1, C_HALF_WINDOW, BLK, C_KEYS,
                          (0, C_HALF_WINDOW, 2 * C_HALF_WINDOW))
    rep = C_HEADS // C_KV_HEADS
    bc = np.stack([stack_pairs(bc_heads, slice(a, None, rep), slice(a + 2, None, rep))
                   for a in range(2)], axis=1)
    bc = np.ascontiguousarray(np.swapaxes(bc, -1, -2))
    sb = _alibi_slopes(B_HEADS)
    il = np.arange(B_TQ)[None, :]
    uu = np.arange(2 * SEQ - B_TQ)[:, None]
    dist = np.abs(il - uu + (SEQ - B_TQ)).astype(np.float32)
    ub = -sb[:, None, None] * dist[None] * np.float32(LOG2E)
    return b12, b3, bc, ub.astype(np.float32)


def kernel(x, c, ada_w, ada_b, norm_g, ab_w_in, ab_w_out, diff_lq1, diff_lk1, diff_lq2,
           diff_lk2, diff_subln_g, c_w_in, c_w_out, c_sink, final_g):
    b12, b3, bc, ub = (jnp.asarray(t) for t in _bias_tables())
    mod = _modulation(c, ada_w, ada_b)
    a_w = A_HEADS * HEAD_DIM
    q_even = ((0, a_w), (3 * a_w, 3 * a_w + B_HEADS * 2 * HEAD_DIM))
    q_odd = ((0, C_HEADS * HEAD_DIM),)

    def in_stage(layer):
        shift = mod[layer, :, :D_MODEL]
        scale = mod[layer, :, D_MODEL:2 * D_MODEL]
        w, q_cols = (ab_w_in, q_even) if layer % 2 == 0 else (c_w_in, q_odd)
        return norm_g[layer], scale, shift, w[layer // 2].astype(BF16), q_cols

    _, proj = _proj(x, in_stage=in_stage(0))
    for layer in range(DEPTH):
        gate = mod[layer, :, 2 * D_MODEL:]
        j = layer // 2
        last = layer == DEPTH - 1
        if layer % 2 == 0:
            lam_init = 0.8 - 0.6 * math.exp(-0.3 * layer)
            za = _mixer_a(proj, b12, b3)
            zb = _mixer_b(proj, ub, diff_lq1[j], diff_lk1[j], diff_lq2[j], diff_lk2[j],
                          diff_subln_g[j], lam_init)
            zs, w_out = [za, zb], ab_w_out[j]
        else:
            zs, w_out = [_mixer_c(proj, c_sink[j], bc)], c_w_out[j]
        x, proj = _proj(x, out_stage=(gate, zs, w_out.astype(BF16)),
                        final_g=final_g if last else None,
                        in_stage=None if last else in_stage(layer + 1))
    return x
```

```python
import functools
import math

import numpy as np
import jax
import jax.numpy as jnp
from jax import lax
from jax.experimental import pallas as pl
from jax.experimental.pallas import tpu as pltpu

D_MODEL = 1024
BATCH = 8
SEQ = 2048
DEPTH = 4
HEAD_DIM = 64
LANES = 128
A_HEADS = 8
A_PATTERNS = ((128, 1), (512, 4), (2048, 16))
A_HALF = 64
B_HEADS = 4
C_HEADS = 16
C_KV_HEADS = 4
C_HALF_WINDOW = 128
EVEN_IN = 4096
ODD_IN = 2560
EPS = 1e-6
NEG_INF = -1e30
LOG2E = math.log2(math.e)
Q_FOLD = HEAD_DIM ** -0.5 * LOG2E
ONES_ROWS = 16

F32 = jnp.float32
BF16 = jnp.bfloat16

ROW_TILE = 512
COL_TILE = 512
BLK = 128
A_KEYS = BLK + 2 * A_HALF
C_KEYS = BLK + 2 * C_HALF_WINDOW
A_UNROLL = 16
C_GROUP = 2
B_TQ = 256
B_TK = 256
VMEM_LIMIT = 56 * 1024 * 1024


def _silu(t):
    return t * (1.0 / (1.0 + jnp.exp(-t)))


def _dot_nt(a, b):
    return lax.dot_general(a, b, (((1,), (1,)), ((), ())), preferred_element_type=F32)


def _alibi_slopes(n):
    return (2.0 ** (-8.0 * np.arange(1, n + 1, dtype=np.float32) / n)).astype(np.float32)


def _band_bias(slopes, spacing, half, tq, tk, offsets):
    i = np.arange(tq)[:, None]
    c = np.arange(tk)[None, :]
    out = np.empty((len(slopes), len(offsets), tq, tk), np.float32)
    for v, off in enumerate(offsets):
        rel = np.abs(c - (i + off))
        dist = (rel * spacing).astype(np.float32)
        for h, m in enumerate(slopes):
            out[h, v] = np.where(rel <= half, -m * dist * np.float32(LOG2E), np.float32(NEG_INF))
    return out


def _mod_kernel(c_ref, w_ref, b_ref, o_ref):
    cs = _silu(c_ref[...])
    o_ref[0] = jnp.dot(cs, w_ref[0], preferred_element_type=F32,
                       precision=lax.Precision.HIGHEST) + b_ref[0]


def _modulation(c, ada_w, ada_b):
    nblk = 3 * D_MODEL // D_MODEL
    return pl.pallas_call(
        _mod_kernel,
        grid=(DEPTH, nblk),
        in_specs=[
            pl.BlockSpec((BATCH, D_MODEL), lambda l, j: (0, 0)),
            pl.BlockSpec((1, D_MODEL, D_MODEL), lambda l, j: (l, 0, j)),
            pl.BlockSpec((1, 1, D_MODEL), lambda l, j: (l, 0, j)),
        ],
        out_specs=pl.BlockSpec((1, BATCH, D_MODEL), lambda l, j: (l, 0, j)),
        out_shape=jax.ShapeDtypeStruct((DEPTH, BATCH, 3 * D_MODEL), F32),
        name="adaln_mod",
    )(c, ada_w, ada_b.reshape(DEPTH, 1, 3 * D_MODEL))


def _rms(x, g):
    ms = jnp.mean(x * x, axis=-1, keepdims=True)
    return x * lax.rsqrt(ms + EPS) * g


def _proj_kernel(*refs, n_z, final, n_out, q_cols):
    refs = iter(refs)
    x_ref = next(refs)
    if n_z:
        gate_ref = next(refs)
        z_refs = [next(refs) for _ in range(n_z)]
        wo_ref = next(refs)
    if final:
        fg_ref = next(refs)
    if n_out:
        g_ref, sc_ref, sh_ref, wi_ref = (next(refs) for _ in range(4))
    if n_z:
        xo_ref = next(refs)
    if n_out:
        proj_ref, h_ref = next(refs), next(refs)

    xn = x_ref[0]
    if n_z:
        kz = D_MODEL // n_z
        acc = jnp.dot(z_refs[0][0], wo_ref[0:kz, :], preferred_element_type=F32)
        for i in range(1, n_z):
            acc = acc + jnp.dot(z_refs[i][0], wo_ref[i * kz:(i + 1) * kz, :],
                                preferred_element_type=F32)
        xn = xn + gate_ref[0] * acc
        xo_ref[0] = _rms(xn, fg_ref[...]) if final else xn
    if n_out:
        h_ref[...] = (_rms(xn, g_ref[...]) * (1.0 + sc_ref[0]) + sh_ref[0]).astype(BF16)
        for j in range(n_out // COL_TILE):
            lo, hi = j * COL_TILE, (j + 1) * COL_TILE
            acc = jnp.dot(h_ref[...], wi_ref[:, lo:hi], preferred_element_type=F32)
            if any(q0 <= lo and hi <= q1 for q0, q1 in q_cols):
                acc = acc * Q_FOLD
            proj_ref[0, :, lo:hi] = acc.astype(BF16)


def _proj(x, out_stage=None, final_g=None, in_stage=None):
    row = pl.BlockSpec((1, ROW_TILE, D_MODEL), lambda b, i: (b, i, 0))
    per_batch = pl.BlockSpec((1, 1, D_MODEL), lambda b, i: (b, 0, 0))
    vec = pl.BlockSpec((1, D_MODEL), lambda b, i: (0, 0))
    whole = lambda shape: pl.BlockSpec(shape, lambda b, i: (0, 0))
    in_specs, args, out_specs, out_shape, scratch = [row], [x], [], [], []
    n_z = n_out = 0
    q_cols = ()
    assert final_g is None or out_stage is not None
    if out_stage is not None:
        gate, zs, w_out = out_stage
        n_z = len(zs)
        in_specs += [per_batch]
        in_specs += [pl.BlockSpec((1, ROW_TILE, D_MODEL // n_z), lambda b, i: (b, i, 0))] * n_z
        in_specs += [whole((D_MODEL, D_MODEL))]
        args += [gate.reshape(BATCH, 1, D_MODEL), *zs, w_out]
        out_specs += [row]
        out_shape += [jax.ShapeDtypeStruct((BATCH, SEQ, D_MODEL), F32)]
    if final_g is not None:
        in_specs += [vec]
        args += [final_g.reshape(1, D_MODEL)]
    if in_stage is not None:
        g, scale, shift, w_in, q_cols = in_stage
        n_out = w_in.shape[1]
        assert all(q0 % COL_TILE == 0 and q1 % COL_TILE == 0 for q0, q1 in q_cols)
        in_specs += [vec, per_batch, per_batch, whole((D_MODEL, n_out))]
        args += [g.reshape(1, D_MODEL), scale.reshape(BATCH, 1, D_MODEL),
                 shift.reshape(BATCH, 1, D_MODEL), w_in]
        out_specs += [pl.BlockSpec((1, ROW_TILE, n_out), lambda b, i: (b, i, 0))]
        out_shape += [jax.ShapeDtypeStruct((BATCH, SEQ, n_out), BF16)]
        scratch += [pltpu.VMEM((ROW_TILE, D_MODEL), BF16)]
    outs = pl.pallas_call(
        functools.partial(_proj_kernel, n_z=n_z, final=final_g is not None, n_out=n_out,
                          q_cols=q_cols),
        grid=(BATCH, SEQ // ROW_TILE),
        in_specs=in_specs,
        out_specs=out_specs,
        out_shape=out_shape,
        scratch_shapes=scratch,
        compiler_params=pltpu.CompilerParams(
            dimension_semantics=("arbitrary", "arbitrary"), vmem_limit_bytes=VMEM_LIMIT),
        name="proj",
    )(*args)
    outs = list(outs)
    x_new = outs.pop(0) if out_stage is not None else None
    proj = outs.pop(0) if in_stage is not None else None
    return x_new, proj


def _lane_lo(rows):
    return lax.broadcasted_iota(jnp.int32, (rows, LANES), 1) < HEAD_DIM


def _softmax_block(qm, kw, vw, bias):
    s = _dot_nt(qm, kw) + bias
    m = jnp.max(s, axis=-1, keepdims=True)
    p = jnp.exp2(s - m)
    l = jnp.sum(p, axis=-1, keepdims=True)
    acc = jnp.dot(p.astype(BF16), vw, preferred_element_type=F32)
    return acc, m, l


def _interleave(*stages):
    live = list(stages)
    while live:
        for g in list(live):
            try:
                next(g)
            except StopIteration:
                live.remove(g)


def _stack_heads(qb, lo):
    zero = jnp.zeros_like(qb)
    return jnp.concatenate([jnp.where(lo, qb, zero), jnp.where(lo, zero, qb)], axis=0)


def _a_kernel(q_ref, k_ref, v_ref, g_ref, b12_ref, b3_ref, o_ref,
              qf_ref, kf_ref, vf_ref, q4_ref, k4_ref, v4_ref, qd_ref, kd_ref, vd_ref,
              qe_ref, ke_ref, ve_ref, acc_ref, m_ref, l_ref):
    lo = _lane_lo(BLK)

    def pair_block(qb, kw, vw, bias):
        acc, m, l = _softmax_block(_stack_heads(qb, lo), kw, vw, bias)
        return (jnp.where(lo, acc[:BLK], acc[BLK:]), jnp.where(lo, m[:BLK], m[BLK:]),
                jnp.where(lo, l[:BLK], l[BLK:]))

    chunk = 512
    for c in range(SEQ // chunk):
        rows = slice(c * chunk, (c + 1) * chunk)
        qf_ref[rows, :] = q_ref[0, rows, :].astype(F32)
        kf_ref[rows, :] = k_ref[0, rows, :].astype(F32)
        vf_ref[rows, :] = v_ref[0, rows, :].astype(F32)

    n_blk = SEQ // BLK

    def variant(blk, last):
        return jnp.where(blk == 0, 0, jnp.where(blk == last, 2, 1))

    def p0_body(j, carry):
        r0 = pl.multiple_of(j * BLK, BLK)
        ws = pl.multiple_of(jnp.clip(j * BLK - A_HALF, 0, SEQ - A_KEYS), A_HALF)
        var = variant(j, n_blk - 1)
        acc, m, l = pair_block(q_ref[0, pl.ds(r0, BLK), :],
                               k_ref[0, pl.ds(ws, A_KEYS), :], v_ref[0, pl.ds(ws, A_KEYS), :],
                               b12_ref[0, 0, var])
        acc_ref[0, pl.ds(r0, BLK), :] = acc
        m_ref[0, pl.ds(r0, BLK), :] = m
        l_ref[0, pl.ds(r0, BLK), :] = l
        return carry

    lax.fori_loop(0, n_blk, p0_body, 0, unroll=A_UNROLL)

    dil1 = A_PATTERNS[1][1]
    seg1 = SEQ // dil1
    per1 = seg1 // BLK
    streams = ((qf_ref, q4_ref, qd_ref, qe_ref), (kf_ref, k4_ref, kd_ref, ke_ref),
               (vf_ref, v4_ref, vd_ref, ve_ref))
    for r in range(dil1):
        dst = slice(r * seg1, (r + 1) * seg1)
        for src_ref, mid_ref, dst_ref, _ in streams:
            t = src_ref[pl.ds(r, seg1, stride=dil1), :]
            mid_ref[dst, :] = t
            dst_ref[dst, :] = t.astype(BF16)

    def p1_body(j, carry):
        r = j // per1
        blk = j % per1
        r0 = pl.multiple_of(j * BLK, BLK)
        ws = pl.multiple_of(r * seg1 + jnp.clip(blk * BLK - A_HALF, 0, seg1 - A_KEYS), A_HALF)
        var = variant(blk, per1 - 1)
        acc, m, l = pair_block(qd_ref[pl.ds(r0, BLK), :],
                               kd_ref[pl.ds(ws, A_KEYS), :], vd_ref[pl.ds(ws, A_KEYS), :],
                               b12_ref[0, 1, var])
        dst = pl.ds(blk * BLK * dil1 + r, BLK, stride=dil1)
        acc_ref[1, dst, :] = acc
        m_ref[1, dst, :] = m
        l_ref[1, dst, :] = l
        return carry

    lax.fori_loop(0, n_blk, p1_body, 0, unroll=A_UNROLL)

    dil2 = A_PATTERNS[2][1]
    sub = dil2 // dil1
    assert sub * dil1 == dil2
    seg2 = SEQ // dil2
    for r in range(dil1):
        for r2 in range(sub):
            res = dil1 * r2 + r
            dst = slice(res * seg2, (res + 1) * seg2)
            for _, mid_ref, _, dst_ref in streams:
                dst_ref[dst, :] = mid_ref[pl.ds(r * seg1 + r2, seg2, stride=sub), :].astype(BF16)

    def p2_body(j, carry):
        r0 = pl.multiple_of(j * BLK, BLK)
        acc, m, l = pair_block(qe_ref[pl.ds(r0, BLK), :],
                               ke_ref[pl.ds(r0, BLK), :], ve_ref[pl.ds(r0, BLK), :],
                               b3_ref[0])
        dst = pl.ds(j, BLK, stride=dil2)
        acc_ref[2, dst, :] = acc
        m_ref[2, dst, :] = m
        l_ref[2, dst, :] = l
        return carry

    lax.fori_loop(0, n_blk, p2_body, 0, unroll=A_UNROLL)

    mrows = 256

    def merge_body(c, carry):
        rows = pl.ds(pl.multiple_of(c * mrows, mrows), mrows)
        m0, m1, m2 = m_ref[0, rows, :], m_ref[1, rows, :], m_ref[2, rows, :]
        mx = jnp.maximum(jnp.maximum(m0, m1), m2)
        w0, w1, w2 = jnp.exp2(m0 - mx), jnp.exp2(m1 - mx), jnp.exp2(m2 - mx)
        num = w0 * acc_ref[0, rows, :] + w1 * acc_ref[1, rows, :] + w2 * acc_ref[2, rows, :]
        den = w0 * l_ref[0, rows, :] + w1 * l_ref[1, rows, :] + w2 * l_ref[2, rows, :]
        y = num / den
        o_ref[0, rows, :] = (y * _silu(g_ref[0, rows, :].astype(F32))).astype(BF16)
        return carry

    lax.fori_loop(0, SEQ // mrows, merge_body, 0)


def _mixer_a(proj, bias12, bias3):
    pairs = A_HEADS // 2
    blk = lambda off: pl.BlockSpec((1, SEQ, LANES), lambda hp, b: (b, 0, off + hp))
    return pl.pallas_call(
        _a_kernel,
        grid=(pairs, BATCH),
        in_specs=[
            blk(0), blk(pairs), blk(2 * pairs), blk(3072 // LANES),
            pl.BlockSpec((1, 2, 3, 2 * BLK, A_KEYS), lambda hp, b: (hp, 0, 0, 0, 0)),
            pl.BlockSpec((1, 2 * BLK, BLK), lambda hp, b: (hp, 0, 0)),
        ],
        out_specs=pl.BlockSpec((1, SEQ, LANES), lambda hp, b: (b, 0, hp)),
        out_shape=jax.ShapeDtypeStruct((BATCH, SEQ, A_HEADS * HEAD_DIM), BF16),
        scratch_shapes=[
            pltpu.VMEM((SEQ, LANES), F32), pltpu.VMEM((SEQ, LANES), F32),
            pltpu.VMEM((SEQ, LANES), F32), pltpu.VMEM((SEQ, LANES), F32),
            pltpu.VMEM((SEQ, LANES), F32),
            pltpu.VMEM((SEQ, LANES), F32),
            pltpu.VMEM((SEQ, LANES), BF16), pltpu.VMEM((SEQ, LANES), BF16),
            pltpu.VMEM((SEQ, LANES), BF16),
            pltpu.VMEM((SEQ, LANES), BF16), pltpu.VMEM((SEQ, LANES), BF16),
            pltpu.VMEM((SEQ, LANES), BF16),
            pltpu.VMEM((3, SEQ, LANES), F32), pltpu.VMEM((3, SEQ, LANES), F32),
            pltpu.VMEM((3, SEQ, LANES), F32),
        ],
        compiler_params=pltpu.CompilerParams(
            dimension_semantics=("arbitrary", "arbitrary"), vmem_limit_bytes=VMEM_LIMIT),
        name="mixer_a_dilated",
    )(proj, proj, proj, proj, bias12, bias3)


def _b_kernel(q_ref, k_ref, v_ref, g_ref, u_ref, lq1_ref, lk1_ref, lq2_ref, lk2_ref,
              sg_ref, o_ref, vt_ref, sa_ref, sb_ref, ma_ref, mb_ref, acca_ref, accb_ref,
              *, lam_init):
    lo = _lane_lo(B_TQ)
    n_t = SEQ // B_TK
    n_q = SEQ // B_TQ
    lam = (jnp.exp(jnp.sum(lq1_ref[...] * lk1_ref[...], axis=-1, keepdims=True))
           - jnp.exp(jnp.sum(lq2_ref[...] * lk2_ref[...], axis=-1, keepdims=True)) + lam_init)

    eye = jnp.where(lax.broadcasted_iota(jnp.int32, (LANES, LANES), 0)
                    == lax.broadcasted_iota(jnp.int32, (LANES, LANES), 1), 1.0, 0.0).astype(BF16)
    ones_row = jnp.where(lax.broadcasted_iota(jnp.int32, (ONES_ROWS, B_TK), 0) == 0,
                         1.0, 0.0).astype(BF16)
    for t in range(n_t):
        rows = slice(t * B_TK, (t + 1) * B_TK)
        vt_ref[:LANES, rows] = _dot_nt(eye, v_ref[0, rows, :]).astype(BF16)
        vt_ref[LANES:, rows] = ones_row

    def scores(i, s_ref, m_ref):
        r0 = pl.multiple_of(i * B_TQ, B_TQ)
        qst = _stack_heads(q_ref[0, pl.ds(r0, B_TQ), :], lo)
        m = None
        for t in range(n_t):
            rows = slice(t * B_TK, (t + 1) * B_TK)
            off = pl.multiple_of(t * B_TK + (SEQ - B_TQ) - r0, LANES)
            ub = u_ref[0, pl.ds(off, B_TK), :]
            st = _dot_nt(k_ref[0, rows, :], qst)
            s1 = st[:, :B_TQ] + ub
            s2 = st[:, B_TQ:] + ub
            s_ref[rows, :B_TQ] = s1
            s_ref[rows, B_TQ:] = s2
            f = jnp.concatenate([jnp.max(s1, axis=0, keepdims=True),
                                 jnp.max(s2, axis=0, keepdims=True)], axis=1)
            m = f if m is None else jnp.maximum(m, f)
            yield
        m_ref[...] = m

    def softmax_pv(s_ref, m_ref, acc_ref):
        m = m_ref[...]
        acct = None
        for t in range(n_t):
            rows = slice(t * B_TK, (t + 1) * B_TK)
            p = jnp.exp2(s_ref[rows, :] - m).astype(BF16)
            d = jnp.dot(vt_ref[:, rows], p, preferred_element_type=F32)
            acct = d if acct is None else acct + d
            yield
        acc_ref[...] = acct

    def epilogue(i, acc_ref):
        r0 = pl.multiple_of(i * B_TQ, B_TQ)
        ot = acc_ref[:LANES, :] * (1.0 / acc_ref[LANES:LANES + 1, :])
        outt = ot[:, :B_TQ] - lam * ot[:, B_TQ:]
        ms = jnp.mean(outt * outt, axis=0, keepdims=True)
        yt = outt * lax.rsqrt(ms + EPS) * sg_ref[...] * (1.0 - lam_init)
        yield
        gate = _silu(g_ref[0, pl.ds(r0, B_TQ), :].astype(F32))
        o_ref[0, pl.ds(r0, B_TQ), :] = (yt.T * gate).astype(BF16)

    sa, ma, acca = sa_ref, ma_ref, acca_ref
    sb, mb, accb = sb_ref, mb_ref, accb_ref
    _interleave(scores(0, sa, ma))
    _interleave(softmax_pv(sa, ma, acca), scores(1, sb, mb))

    def body(j, carry):
        _interleave(epilogue(2 * j, acca), softmax_pv(sb, mb, accb), scores(2 * j + 2, sa, ma))
        _interleave(epilogue(2 * j + 1, accb), softmax_pv(sa, ma, acca), scores(2 * j + 3, sb, mb))
        return carry

    lax.fori_loop(0, n_q // 2 - 1, body, 0)
    _interleave(epilogue(n_q - 2, acca), softmax_pv(sb, mb, accb))
    _interleave(epilogue(n_q - 1, accb))


def _mixer_b(proj, u, lq1, lk1, lq2, lk2, subln_g, lam_init):
    qb = 3 * A_HEADS * HEAD_DIM // LANES
    kb = qb + B_HEADS
    vb = kb + B_HEADS
    gb = 3072 // LANES + A_HEADS * HEAD_DIM // LANES
    vec = lambda n: pl.BlockSpec((1, n), lambda h, b: (0, 0))
    blk = lambda off: pl.BlockSpec((1, SEQ, LANES), lambda h, b: (b, 0, off + h))
    return pl.pallas_call(
        functools.partial(_b_kernel, lam_init=lam_init),
        grid=(B_HEADS, BATCH),
        in_specs=[
            blk(qb), blk(kb), blk(vb), blk(gb),
            pl.BlockSpec((1, 2 * SEQ - B_TQ, B_TQ), lambda h, b: (h, 0, 0)),
            vec(HEAD_DIM), vec(HEAD_DIM), vec(HEAD_DIM), vec(HEAD_DIM),
            pl.BlockSpec((2 * HEAD_DIM, 1), lambda h, b: (0, 0)),
        ],
        out_specs=pl.BlockSpec((1, SEQ, LANES), lambda h, b: (b, 0, h)),
        out_shape=jax.ShapeDtypeStruct((BATCH, SEQ, B_HEADS * 2 * HEAD_DIM), BF16),
        scratch_shapes=[pltpu.VMEM((LANES + ONES_ROWS, SEQ), BF16),
                        pltpu.VMEM((SEQ, 2 * B_TQ), F32), pltpu.VMEM((SEQ, 2 * B_TQ), F32),
                        pltpu.VMEM((1, 2 * B_TQ), F32), pltpu.VMEM((1, 2 * B_TQ), F32),
                        pltpu.VMEM((LANES + ONES_ROWS, 2 * B_TQ), F32),
                        pltpu.VMEM((LANES + ONES_ROWS, 2 * B_TQ), F32)],
        compiler_params=pltpu.CompilerParams(
            dimension_semantics=("arbitrary", "arbitrary"), vmem_limit_bytes=VMEM_LIMIT),
        name="mixer_b_differential",
    )(proj, proj, proj, proj, u, lq1.reshape(1, -1), lk1.reshape(1, -1),
      lq2.reshape(1, -1), lk2.reshape(1, -1), subln_g.reshape(-1, 1))


def _c_kernel(sink_ref, q_ref, k_ref, v_ref, g_ref, bias_ref, o_ref, ka_ref, vt_ref, *bufs):
    s_refs, m_refs = bufs[:2 * C_GROUP], bufs[2 * C_GROUP:]
    grp = pl.program_id(0)
    half = grp % 2
    lo = _lane_lo(BLK)
    zero = jnp.zeros((BLK, LANES), BF16)
    row = lax.broadcasted_iota(jnp.int32, (LANES, LANES), 0)
    col = lax.broadcasted_iota(jnp.int32, (LANES, LANES), 1)

    chunk = 512
    ones_row = jnp.where(lax.broadcasted_iota(jnp.int32, (ONES_ROWS, chunk), 0) == 0,
                         1.0, 0.0).astype(BF16)
    for a in range(2):
        shift = jnp.where(half == a, 0, HEAD_DIM)
        perm = jnp.where(col == (row + shift) % LANES, 1.0, 0.0).astype(BF16)
        for c in range(SEQ // chunk):
            rows = slice(c * chunk, (c + 1) * chunk)
            ka_ref[a, rows, :] = jnp.dot(k_ref[0, rows, :], perm,
                                         preferred_element_type=F32).astype(BF16)
            vt_ref[a, :LANES, rows] = _dot_nt(perm, v_ref[0, rows, :]).astype(BF16)
            vt_ref[a, LANES:, rows] = ones_row

    n_blk = SEQ // BLK
    rep = C_HEADS // C_KV_HEADS
    left = lax.broadcasted_iota(jnp.int32, (1, 2 * BLK), 1) < BLK
    upper = lax.broadcasted_iota(jnp.int32, (LANES, BLK), 0) < HEAD_DIM

    def window(n):
        return pl.multiple_of(jnp.clip(n * BLK - C_HALF_WINDOW, 0, SEQ - C_KEYS), BLK)

    def scores(n, s_ref, m_ref):
        r0 = pl.multiple_of(n * BLK, BLK)
        ws = window(n)
        var = jnp.where(n == 0, 0, jnp.where(n == n_blk - 1, 2, 1))
        for a in range(2):
            qa = [q_ref[0, pl.ds(r0, BLK), c * LANES:(c + 1) * LANES] for c in range(rep // 2)]
            qm = jnp.concatenate(
                [jnp.where(lo, q, zero) if a == 0 else jnp.where(lo, zero, q) for q in qa], axis=0)
            st = _dot_nt(ka_ref[a, pl.ds(ws, C_KEYS), :], qm) + bias_ref[0, a, var]
            s_ref[a] = st
            m_ref[a] = jnp.max(st, axis=0, keepdims=True)
            yield

    def finish(n, s_ref, m_ref):
        r0 = pl.multiple_of(n * BLK, BLK)
        ws = window(n)
        yts = []
        for a in range(2):
            m = m_ref[a]
            p = jnp.exp2(s_ref[a] - m).astype(BF16)
            acct = jnp.dot(vt_ref[a, :, pl.ds(ws, C_KEYS)], p, preferred_element_type=F32)
            l = acct[LANES:LANES + 1]
            sk = jnp.where(left, sink_ref[grp * rep + a], sink_ref[grp * rep + a + 2]) * LOG2E
            mx = jnp.maximum(m, sk)
            e = jnp.exp2(m - mx)
            yts.append(acct[:LANES] * (e / (l * e + jnp.exp2(sk - mx))))
            yield
        for c in range(rep // 2):
            cols = slice(c * LANES, (c + 1) * LANES)
            y = jnp.where(upper, yts[0][:, cols], yts[1][:, cols]).T
            gate = _silu(g_ref[0, pl.ds(r0, BLK), cols].astype(F32))
            o_ref[0, pl.ds(r0, BLK), cols] = (y * gate).astype(BF16)

    bufs = [(s_refs[i], m_refs[i]) for i in range(2 * C_GROUP)]
    set_a, set_b = bufs[:C_GROUP], bufs[C_GROUP:]
    n_grp = n_blk // C_GROUP

    def scores_of(g, bset):
        return [scores(g * C_GROUP + i, *bset[i]) for i in range(C_GROUP)]

    def finish_of(g, bset):
        return [finish(g * C_GROUP + i, *bset[i]) for i in range(C_GROUP)]

    _interleave(*scores_of(0, set_a))

    def body(j, carry):
        _interleave(*finish_of(2 * j, set_a), *scores_of(2 * j + 1, set_b))
        _interleave(*finish_of(2 * j + 1, set_b), *scores_of(2 * j + 2, set_a))
        return carry

    lax.fori_loop(0, n_grp // 2 - 1, body, 0)
    _interleave(*finish_of(n_grp - 2, set_a), *scores_of(n_grp - 1, set_b))
    _interleave(*finish_of(n_grp - 1, set_b))


def _mixer_c(proj, sink, bias):
    width = C_HEADS // C_KV_HEADS * HEAD_DIM
    kb = C_HEADS * HEAD_DIM // LANES
    vb = kb + C_KV_HEADS * HEAD_DIM // LANES
    gb = (C_HEADS + 2 * C_KV_HEADS) * HEAD_DIM // width
    return pl.pallas_call(
        _c_kernel,
        grid=(C_KV_HEADS, BATCH),
        in_specs=[
            pl.BlockSpec(memory_space=pltpu.SMEM),
            pl.BlockSpec((1, SEQ, width), lambda g, b: (b, 0, g)),
            pl.BlockSpec((1, SEQ, LANES), lambda g, b: (b, 0, kb + g // 2)),
            pl.BlockSpec((1, SEQ, LANES), lambda g, b: (b, 0, vb + g // 2)),
            pl.BlockSpec((1, SEQ, width), lambda g, b: (b, 0, gb + g)),
            pl.BlockSpec((1, 2, 3, C_KEYS, 2 * BLK), lambda g, b: (g, 0, 0, 0, 0)),
        ],
        out_specs=pl.BlockSpec((1, SEQ, width), lambda g, b: (b, 0, g)),
        out_shape=jax.ShapeDtypeStruct((BATCH, SEQ, C_HEADS * HEAD_DIM), BF16),
        scratch_shapes=[pltpu.VMEM((2, SEQ, LANES), BF16),
                        pltpu.VMEM((2, LANES + ONES_ROWS, SEQ), BF16)]
        + [pltpu.VMEM((2, C_KEYS, 2 * BLK), F32)] * (2 * C_GROUP)
        + [pltpu.VMEM((2, 1, 2 * BLK), F32)] * (2 * C_GROUP),
        compiler_params=pltpu.CompilerParams(
            dimension_semantics=("arbitrary", "arbitrary"), vmem_limit_bytes=VMEM_LIMIT),
        name="mixer_c_windowed",
    )(sink, proj, proj, proj, proj, bias)


@functools.lru_cache(maxsize=None)
def _bias_tables():
    def stack_pairs(t, first, second):
        return np.concatenate([t[first], t[second]], axis=2)

    sa = _alibi_slopes(A_HEADS)
    offs = (0, A_HALF, 2 * A_HALF)
    ev, od = slice(0, None, 2), slice(1, None, 2)
    b12 = np.stack(
        [stack_pairs(_band_bias(sa, dil, A_HALF, BLK, A_KEYS, offs), ev, od)
         for _, dil in A_PATTERNS[:2]], axis=1)
    b3 = stack_pairs(_band_bias(sa, A_PATTERNS[2][1], A_HALF, BLK, BLK, (0,)), ev, od)[:, 0]
    sc = _alibi_slopes(C_HEADS)
    bc_heads = _band_bias(sc, 1, C_HALF_WINDOW, BLK, C_KEYS,
                          (0, C_HALF_WINDOW, 2 * C_HALF_WINDOW))
    rep = C_HEADS // C_KV_HEADS
    bc = np.stack([stack_pairs(bc_heads, slice(a, None, rep), slice(a + 2, None, rep))
                   for a in range(2)], axis=1)
    bc = np.ascontiguousarray(np.swapaxes(bc, -1, -2))
    sb = _alibi_slopes(B_HEADS)
    il = np.arange(B_TQ)[None, :]
    uu = np.arange(2 * SEQ - B_TQ)[:, None]
    dist = np.abs(il - uu + (SEQ - B_TQ)).astype(np.float32)
    ub = -sb[:, None, None] * dist[None] * np.float32(LOG2E)
    return b12, b3, bc, ub.astype(np.float32)


def kernel(x, c, ada_w, ada_b, norm_g, ab_w_in, ab_w_out, diff_lq1, diff_lk1, diff_lq2,
           diff_lk2, diff_subln_g, c_w_in, c_w_out, c_sink, final_g):
    b12, b3, bc, ub = (jnp.asarray(t) for t in _bias_tables())
    mod = _modulation(c, ada_w, ada_b)
    a_w = A_HEADS * HEAD_DIM
    q_even = ((0, a_w), (3 * a_w, 3 * a_w + B_HEADS * 2 * HEAD_DIM))
    q_odd = ((0, C_HEADS * HEAD_DIM),)

    def in_stage(layer):
        shift = mod[layer, :, :D_MODEL]
        scale = mod[layer, :, D_MODEL:2 * D_MODEL]
        w, q_cols = (ab_w_in, q_even) if layer % 2 == 0 else (c_w_in, q_odd)
        return norm_g[layer], scale, shift, w[layer // 2].astype(BF16), q_cols

    _, proj = _proj(x, in_stage=in_stage(0))
    for layer in range(DEPTH):
        gate = mod[layer, :, 2 * D_MODEL:]
        j = layer // 2
        last = layer == DEPTH - 1
        if layer % 2 == 0:
            lam_init = 0.8 - 0.6 * math.exp(-0.3 * layer)
            za = _mixer_a(proj, b12, b3)
            zb = _mixer_b(proj, ub, diff_lq1[j], diff_lk1[j], diff_lq2[j], diff_lk2[j],
                          diff_subln_g[j], lam_init)
            zs, w_out = [za, zb], ab_w_out[j]
        else:
            zs, w_out = [_mixer_c(proj, c_sink[j], bc)], c_w_out[j]
        x, proj = _proj(x, out_stage=(gate, zs, w_out.astype(BF16)),
                        final_g=final_g if last else None,
                        in_stage=None if last else in_stage(layer + 1))
    return x
```

```python
import functools
import math

import numpy as np
import jax
import jax.numpy as jnp
from jax import lax
from jax.experimental import pallas as pl
from jax.experimental.pallas import tpu as pltpu

D_MODEL = 1024
BATCH = 8
SEQ = 2048
DEPTH = 4
HEAD_DIM = 64
LANES = 128
A_HEADS = 8
A_PATTERNS = ((128, 1), (512, 4), (2048, 16))
A_HALF = 64
B_HEADS = 4
C_HEADS = 16
C_KV_HEADS = 4
C_HALF_WINDOW = 128
EVEN_IN = 4096
ODD_IN = 2560
EPS = 1e-6
NEG_INF = -1e30
LOG2E = math.log2(math.e)
Q_FOLD = HEAD_DIM ** -0.5 * LOG2E
ONES_ROWS = 16

F32 = jnp.float32
BF16 = jnp.bfloat16

ROW_TILE = 512
COL_TILE = 512
BLK = 128
A_KEYS = BLK + 2 * A_HALF
C_KEYS = BLK + 2 * C_HALF_WINDOW
A_UNROLL = 16
C_GROUP = 2
B_TQ = 256
B_TK = 256
VMEM_LIMIT = 56 * 1024 * 1024


def _silu(t):
    return t * (1.0 / (1.0 + jnp.exp(-t)))


def _dot_nt(a, b):
    return lax.dot_general(a, b, (((1,), (1,)), ((), ())), preferred_element_type=F32)


def _alibi_slopes(n):
    return (2.0 ** (-8.0 * np.arange(1, n + 1, dtype=np.float32) / n)).astype(np.float32)


def _band_bias(slopes, spacing, half, tq, tk, offsets):
    i = np.arange(tq)[:, None]
    c = np.arange(tk)[None, :]
    out = np.empty((len(slopes), len(offsets), tq, tk), np.float32)
    for v, off in enumerate(offsets):
        rel = np.abs(c - (i + off))
        dist = (rel * spacing).astype(np.float32)
        for h, m in enumerate(slopes):
            out[h, v] = np.where(rel <= half, -m * dist * np.float32(LOG2E), np.float32(NEG_INF))
    return out


def _mod_kernel(c_ref, w_ref, b_ref, o_ref):
    cs = _silu(c_ref[...])
    o_ref[0] = jnp.dot(cs, w_ref[0], preferred_element_type=F32,
                       precision=lax.Precision.HIGHEST) + b_ref[0]


def _modulation(c, ada_w, ada_b):
    nblk = 3 * D_MODEL // D_MODEL
    return pl.pallas_call(
        _mod_kernel,
        grid=(DEPTH, nblk),
        in_specs=[
            pl.BlockSpec((BATCH, D_MODEL), lambda l, j: (0, 0)),
            pl.BlockSpec((1, D_MODEL, D_MODEL), lambda l, j: (l, 0, j)),
            pl.BlockSpec((1, 1, D_MODEL), lambda l, j: (l, 0, j)),
        ],
        out_specs=pl.BlockSpec((1, BATCH, D_MODEL), lambda l, j: (l, 0, j)),
        out_shape=jax.ShapeDtypeStruct((DEPTH, BATCH, 3 * D_MODEL), F32),
        name="adaln_mod",
    )(c, ada_w, ada_b.reshape(DEPTH, 1, 3 * D_MODEL))


def _rms(x, g):
    ms = jnp.mean(x * x, axis=-1, keepdims=True)
    return x * lax.rsqrt(ms + EPS) * g


def _proj_kernel(*refs, n_z, final, n_out, q_cols):
    refs = iter(refs)
    x_ref = next(refs)
    if n_z:
        gate_ref = next(refs)
        z_refs = [next(refs) for _ in range(n_z)]
        wo_ref = next(refs)
    if final:
        fg_ref = next(refs)
    if n_out:
        g_ref, sc_ref, sh_ref, wi_ref = (next(refs) for _ in range(4))
    if n_z:
        xo_ref = next(refs)
    if n_out:
        proj_ref, h_ref = next(refs), next(refs)

    xn = x_ref[0]
    if n_z:
        kz = D_MODEL // n_z
        acc = jnp.dot(z_refs[0][0], wo_ref[0:kz, :], preferred_element_type=F32)
        for i in range(1, n_z):
            acc = acc + jnp.dot(z_refs[i][0], wo_ref[i * kz:(i + 1) * kz, :],
                                preferred_element_type=F32)
        xn = xn + gate_ref[0] * acc
        xo_ref[0] = _rms(xn, fg_ref[...]) if final else xn
    if n_out:
        h_ref[...] = (_rms(xn, g_ref[...]) * (1.0 + sc_ref[0]) + sh_ref[0]).astype(BF16)
        for j in range(n_out // COL_TILE):
            lo, hi = j * COL_TILE, (j + 1) * COL_TILE
            acc = jnp.dot(h_ref[...], wi_ref[:, lo:hi], preferred_element_type=F32)
            if any(q0 <= lo and hi <= q1 for q0, q1 in q_cols):
                acc = acc * Q_FOLD
            proj_ref[0, :, lo:hi] = acc.astype(BF16)


def _proj(x, out_stage=None, final_g=None, in_stage=None):
    row = pl.BlockSpec((1, ROW_TILE, D_MODEL), lambda b, i: (b, i, 0))
    per_batch = pl.BlockSpec((1, 1, D_MODEL), lambda b, i: (b, 0, 0))
    vec = pl.BlockSpec((1, D_MODEL), lambda b, i: (0, 0))
    whole = lambda shape: pl.BlockSpec(shape, lambda b, i: (0, 0))
    in_specs, args, out_specs, out_shape, scratch = [row], [x], [], [], []
    n_z = n_out = 0
    q_cols = ()
    assert final_g is None or out_stage is not None
    if out_stage is not None:
        gate, zs, w_out = out_stage
        n_z = len(zs)
        in_specs += [per_batch]
        in_specs += [pl.BlockSpec((1, ROW_TILE, D_MODEL // n_z), lambda b, i: (b, i, 0))] * n_z
        in_specs += [whole((D_MODEL, D_MODEL))]
        args += [gate.reshape(BATCH, 1, D_MODEL), *zs, w_out]
        out_specs += [row]
        out_shape += [jax.ShapeDtypeStruct((BATCH, SEQ, D_MODEL), F32)]
    if final_g is not None:
        in_specs += [vec]
        args += [final_g.reshape(1, D_MODEL)]
    if in_stage is not None:
        g, scale, shift, w_in, q_cols = in_stage
        n_out = w_in.shape[1]
        assert all(q0 % COL_TILE == 0 and q1 % COL_TILE == 0 for q0, q1 in q_cols)
        in_specs += [vec, per_batch, per_batch, whole((D_MODEL, n_out))]
        args += [g.reshape(1, D_MODEL), scale.reshape(BATCH, 1, D_MODEL),
                 shift.reshape(BATCH, 1, D_MODEL), w_in]
        out_specs += [pl.BlockSpec((1, ROW_TILE, n_out), lambda b, i: (b, i, 0))]
        out_shape += [jax.ShapeDtypeStruct((BATCH, SEQ, n_out), BF16)]
        scratch += [pltpu.VMEM((ROW_TILE, D_MODEL), BF16)]
    outs = pl.pallas_call(
        functools.partial(_proj_kernel, n_z=n_z, final=final_g is not None, n_out=n_out,
                          q_cols=q_cols),
        grid=(BATCH, SEQ // ROW_TILE),
        in_specs=in_specs,
        out_specs=out_specs,
        out_shape=out_shape,
        scratch_shapes=scratch,
        compiler_params=pltpu.CompilerParams(
            dimension_semantics=("arbitrary", "arbitrary"), vmem_limit_bytes=VMEM_LIMIT),
        name="proj",
    )(*args)
    outs = list(outs)
    x_new = outs.pop(0) if out_stage is not None else None
    proj = outs.pop(0) if in_stage is not None else None
    return x_new, proj


def _lane_lo(rows):
    return lax.broadcasted_iota(jnp.int32, (rows, LANES), 1) < HEAD_DIM


def _softmax_block(qm, kw, vw, bias):
    s = _dot_nt(qm, kw) + bias
    m = jnp.max(s, axis=-1, keepdims=True)
    p = jnp.exp2(s - m)
    l = jnp.sum(p, axis=-1, keepdims=True)
    acc = jnp.dot(p.astype(BF16), vw, preferred_element_type=F32)
    return acc, m, l


def _interleave(*stages):
    live = list(stages)
    while live:
        for g in list(live):
            try:
                next(g)
            except StopIteration:
                live.remove(g)


def _stack_heads(qb, lo):
    zero = jnp.zeros_like(qb)
    return jnp.concatenate([jnp.where(lo, qb, zero), jnp.where(lo, zero, qb)], axis=0)


def _a_kernel(q_ref, k_ref, v_ref, g_ref, b12_ref, b3_ref, o_ref,
              qf_ref, kf_ref, vf_ref, q4_ref, k4_ref, v4_ref, qd_ref, kd_ref, vd_ref,
              qe_ref, ke_ref, ve_ref, acc_ref, m_ref, l_ref):
    lo = _lane_lo(BLK)

    def pair_block(qb, kw, vw, bias):
        acc, m, l = _softmax_block(_stack_heads(qb, lo), kw, vw, bias)
        return (jnp.where(lo, acc[:BLK], acc[BLK:]), jnp.where(lo, m[:BLK], m[BLK:]),
                jnp.where(lo, l[:BLK], l[BLK:]))

    chunk = 512
    for c in range(SEQ // chunk):
        rows = slice(c * chunk, (c + 1) * chunk)
        qf_ref[rows, :] = q_ref[0, rows, :].astype(F32)
        kf_ref[rows, :] = k_ref[0, rows, :].astype(F32)
        vf_ref[rows, :] = v_ref[0, rows, :].astype(F32)

    n_blk = SEQ // BLK

    def variant(blk, last):
        return jnp.where(blk == 0, 0, jnp.where(blk == last, 2, 1))

    def p0_body(j, carry):
        r0 = pl.multiple_of(j * BLK, BLK)
        ws = pl.multiple_of(jnp.clip(j * BLK - A_HALF, 0, SEQ - A_KEYS), A_HALF)
        var = variant(j, n_blk - 1)
        acc, m, l = pair_block(q_ref[0, pl.ds(r0, BLK), :],
                               k_ref[0, pl.ds(ws, A_KEYS), :], v_ref[0, pl.ds(ws, A_KEYS), :],
                               b12_ref[0, 0, var])
        acc_ref[0, pl.ds(r0, BLK), :] = acc
        m_ref[0, pl.ds(r0, BLK), :] = m
        l_ref[0, pl.ds(r0, BLK), :] = l
        return carry

    lax.fori_loop(0, n_blk, p0_body, 0, unroll=A_UNROLL)

    dil1 = A_PATTERNS[1][1]
    seg1 = SEQ // dil1
    per1 = seg1 // BLK
    streams = ((qf_ref, q4_ref, qd_ref, qe_ref), (kf_ref, k4_ref, kd_ref, ke_ref),
               (vf_ref, v4_ref, vd_ref, ve_ref))
    for r in range(dil1):
        dst = slice(r * seg1, (r + 1) * seg1)
        for src_ref, mid_ref, dst_ref, _ in streams:
            t = src_ref[pl.ds(r, seg1, stride=dil1), :]
            mid_ref[dst, :] = t
            dst_ref[dst, :] = t.astype(BF16)

    def p1_body(j, carry):
        r = j // per1
        blk = j % per1
        r0 = pl.multiple_of(j * BLK, BLK)
        ws = pl.multiple_of(r * seg1 + jnp.clip(blk * BLK - A_HALF, 0, seg1 - A_KEYS), A_HALF)
        var = variant(blk, per1 - 1)
        acc, m, l = pair_block(qd_ref[pl.ds(r0, BLK), :],
                               kd_ref[pl.ds(ws, A_KEYS), :], vd_ref[pl.ds(ws, A_KEYS), :],
                               b12_ref[0, 1, var])
        dst = pl.ds(blk * BLK * dil1 + r, BLK, stride=dil1)
        acc_ref[1, dst, :] = acc
        m_ref[1, dst, :] = m
        l_ref[1, dst, :] = l
        return carry

    lax.fori_loop(0, n_blk, p1_body, 0, unroll=A_UNROLL)

    dil2 = A_PATTERNS[2][1]
    sub = dil2 // dil1
    assert sub * dil1 == dil2
    seg2 = SEQ // dil2
    for r in range(dil1):
        for r2 in range(sub):
            res = dil1 * r2 + r
            dst = slice(res * seg2, (res + 1) * seg2)
            for _, mid_ref, _, dst_ref in streams:
                dst_ref[dst, :] = mid_ref[pl.ds(r * seg1 + r2, seg2, stride=sub), :].astype(BF16)

    def p2_body(j, carry):
        r0 = pl.multiple_of(j * BLK, BLK)
        acc, m, l = pair_block(qe_ref[pl.ds(r0, BLK), :],
                               ke_ref[pl.ds(r0, BLK), :], ve_ref[pl.ds(r0, BLK), :],
                               b3_ref[0])
        dst = pl.ds(j, BLK, stride=dil2)
        acc_ref[2, dst, :] = acc
        m_ref[2, dst, :] = m
        l_ref[2, dst, :] = l
        return carry

    lax.fori_loop(0, n_blk, p2_body, 0, unroll=A_UNROLL)

    mrows = 256

    def merge_body(c, carry):
        rows = pl.ds(pl.multiple_of(c * mrows, mrows), mrows)
        m0, m1, m2 = m_ref[0, rows, :], m_ref[1, rows, :], m_ref[2, rows, :]
        mx = jnp.maximum(jnp.maximum(m0, m1), m2)
        w0, w1, w2 = jnp.exp2(m0 - mx), jnp.exp2(m1 - mx), jnp.exp2(m2 - mx)
        num = w0 * acc_ref[0, rows, :] + w1 * acc_ref[1, rows, :] + w2 * acc_ref[2, rows, :]
        den = w0 * l_ref[0, rows, :] + w1 * l_ref[1, rows, :] + w2 * l_ref[2, rows, :]
        y = num / den
        o_ref[0, rows, :] = (y * _silu(g_ref[0, rows, :].astype(F32))).astype(BF16)
        return carry

    lax.fori_loop(0, SEQ // mrows, merge_body, 0)


def _mixer_a(proj, bias12, bias3):
    pairs = A_HEADS // 2
    blk = lambda off: pl.BlockSpec((1, SEQ, LANES), lambda hp, b: (b, 0, off + hp))
    return pl.pallas_call(
        _a_kernel,
        grid=(pairs, BATCH),
        in_specs=[
            blk(0), blk(pairs), blk(2 * pairs), blk(3072 // LANES),
            pl.BlockSpec((1, 2, 3, 2 * BLK, A_KEYS), lambda hp, b: (hp, 0, 0, 0, 0)),
            pl.BlockSpec((1, 2 * BLK, BLK), lambda hp, b: (hp, 0, 0)),
        ],
        out_specs=pl.BlockSpec((1, SEQ, LANES), lambda hp, b: (b, 0, hp)),
        out_shape=jax.ShapeDtypeStruct((BATCH, SEQ, A_HEADS * HEAD_DIM), BF16),
        scratch_shapes=[
            pltpu.VMEM((SEQ, LANES), F32), pltpu.VMEM((SEQ, LANES), F32),
            pltpu.VMEM((SEQ, LANES), F32), pltpu.VMEM((SEQ, LANES), F32),
            pltpu.VMEM((SEQ, LANES), F32),
            pltpu.VMEM((SEQ, LANES), F32),
            pltpu.VMEM((SEQ, LANES), BF16), pltpu.VMEM((SEQ, LANES), BF16),
            pltpu.VMEM((SEQ, LANES), BF16),
            pltpu.VMEM((SEQ, LANES), BF16), pltpu.VMEM((SEQ, LANES), BF16),
            pltpu.VMEM((SEQ, LANES), BF16),
            pltpu.VMEM((3, SEQ, LANES), F32), pltpu.VMEM((3, SEQ, LANES), F32),
            pltpu.VMEM((3, SEQ, LANES), F32),
        ],
        compiler_params=pltpu.CompilerParams(
            dimension_semantics=("arbitrary", "arbitrary"), vmem_limit_bytes=VMEM_LIMIT),
        name="mixer_a_dilated",
    )(proj, proj, proj, proj, bias12, bias3)


def _b_kernel(q_ref, k_ref, v_ref, g_ref, u_ref, lq1_ref, lk1_ref, lq2_ref, lk2_ref,
              sg_ref, o_ref, vt_ref, sa_ref, sb_ref, ma_ref, mb_ref, acca_ref, accb_ref,
              *, lam_init):
    lo = _lane_lo(B_TQ)
    n_t = SEQ // B_TK
    n_q = SEQ // B_TQ
    lam = (jnp.exp(jnp.sum(lq1_ref[...] * lk1_ref[...], axis=-1, keepdims=True))
           - jnp.exp(jnp.sum(lq2_ref[...] * lk2_ref[...], axis=-1, keepdims=True)) + lam_init)

    eye = jnp.where(lax.broadcasted_iota(jnp.int32, (LANES, LANES), 0)
                    == lax.broadcasted_iota(jnp.int32, (LANES, LANES), 1), 1.0, 0.0).astype(BF16)
    ones_row = jnp.where(lax.broadcasted_iota(jnp.int32, (ONES_ROWS, B_TK), 0) == 0,
                         1.0, 0.0).astype(BF16)
    for t in range(n_t):
        rows = slice(t * B_TK, (t + 1) * B_TK)
        vt_ref[:LANES, rows] = _dot_nt(eye, v_ref[0, rows, :]).astype(BF16)
        vt_ref[LANES:, rows] = ones_row

    def scores(i, s_ref, m_ref):
        r0 = pl.multiple_of(i * B_TQ, B_TQ)
        qst = _stack_heads(q_ref[0, pl.ds(r0, B_TQ), :], lo)
        m = None
        for t in range(n_t):
            rows = slice(t * B_TK, (t + 1) * B_TK)
            off = pl.multiple_of(t * B_TK + (SEQ - B_TQ) - r0, LANES)
            ub = u_ref[0, pl.ds(off, B_TK), :]
            st = _dot_nt(k_ref[0, rows, :], qst)
            s1 = st[:, :B_TQ] + ub
            s2 = st[:, B_TQ:] + ub
            s_ref[rows, :B_TQ] = s1
            s_ref[rows, B_TQ:] = s2
            f = jnp.concatenate([jnp.max(s1, axis=0, keepdims=True),
                                 jnp.max(s2, axis=0, keepdims=True)], axis=1)
            m = f if m is None else jnp.maximum(m, f)
            yield
        m_ref[...] = m

    def softmax_pv(s_ref, m_ref, acc_ref):
        m = m_ref[...]
        acct = None
        for t in range(n_t):
            rows = slice(t * B_TK, (t + 1) * B_TK)
            p = jnp.exp2(s_ref[rows, :] - m).astype(BF16)
            d = jnp.dot(vt_ref[:, rows], p, preferred_element_type=F32)
            acct = d if acct is None else acct + d
            yield
        acc_ref[...] = acct

    def epilogue(i, acc_ref):
        r0 = pl.multiple_of(i * B_TQ, B_TQ)
        ot = acc_ref[:LANES, :] * (1.0 / acc_ref[LANES:LANES + 1, :])
        outt = ot[:, :B_TQ] - lam * ot[:, B_TQ:]
        ms = jnp.mean(outt * outt, axis=0, keepdims=True)
        yt = outt * lax.rsqrt(ms + EPS) * sg_ref[...] * (1.0 - lam_init)
        yield
        gate = _silu(g_ref[0, pl.ds(r0, B_TQ), :].astype(F32))
        o_ref[0, pl.ds(r0, B_TQ), :] = (yt.T * gate).astype(BF16)

    sa, ma, acca = sa_ref, ma_ref, acca_ref
    sb, mb, accb = sb_ref, mb_ref, accb_ref
    _interleave(scores(0, sa, ma))
    _interleave(softmax_pv(sa, ma, acca), scores(1, sb, mb))

    def body(j, carry):
        _interleave(epilogue(2 * j, acca), softmax_pv(sb, mb, accb), scores(2 * j + 2, sa, ma))
        _interleave(epilogue(2 * j + 1, accb), softmax_pv(sa, ma, acca), scores(2 * j + 3, sb, mb))
        return carry

    lax.fori_loop(0, n_q // 2 - 1, body, 0)
    _interleave(epilogue(n_q - 2, acca), softmax_pv(sb, mb, accb))
    _interleave(epilogue(n_q - 1, accb))


def _mixer_b(proj, u, lq1, lk1, lq2, lk2, subln_g, lam_init):
    qb = 3 * A_HEADS * HEAD_DIM // LANES
    kb = qb + B_HEADS
    vb = kb + B_HEADS
    gb = 3072 // LANES + A_HEADS * HEAD_DIM // LANES
    vec = lambda n: pl.BlockSpec((1, n), lambda h, b: (0, 0))
    blk = lambda off: pl.BlockSpec((1, SEQ, LANES), lambda h, b: (b, 0, off + h))
    return pl.pallas_call(
        functools.partial(_b_kernel, lam_init=lam_init),
        grid=(B_HEADS, BATCH),
        in_specs=[
            blk(qb), blk(kb), blk(vb), blk(gb),
            pl.BlockSpec((1, 2 * SEQ - B_TQ, B_TQ), lambda h, b: (h, 0, 0)),
            vec(HEAD_DIM), vec(HEAD_DIM), vec(HEAD_DIM), vec(HEAD_DIM),
            pl.BlockSpec((2 * HEAD_DIM, 1), lambda h, b: (0, 0)),
        ],
        out_specs=pl.BlockSpec((1, SEQ, LANES), lambda h, b: (b, 0, h)),
        out_shape=jax.ShapeDtypeStruct((BATCH, SEQ, B_HEADS * 2 * HEAD_DIM), BF16),
        scratch_shapes=[pltpu.VMEM((LANES + ONES_ROWS, SEQ), BF16),
                        pltpu.VMEM((SEQ, 2 * B_TQ), F32), pltpu.VMEM((SEQ, 2 * B_TQ), F32),
                        pltpu.VMEM((1, 2 * B_TQ), F32), pltpu.VMEM((1, 2 * B_TQ), F32),
                        pltpu.VMEM((LANES + ONES_ROWS, 2 * B_TQ), F32),
                        pltpu.VMEM((LANES + ONES_ROWS, 2 * B_TQ), F32)],
        compiler_params=pltpu.CompilerParams(
            dimension_semantics=("arbitrary", "arbitrary"), vmem_limit_bytes=VMEM_LIMIT),
        name="mixer_b_differential",
    )(proj, proj, proj, proj, u, lq1.reshape(1, -1), lk1.reshape(1, -1),
      lq2.reshape(1, -1), lk2.reshape(1, -1), subln_g.reshape(-1, 1))


def _c_kernel(sink_ref, q_ref, k_ref, v_ref, g_ref, bias_ref, o_ref, ka_ref, vt_ref, *bufs):
    s_refs, m_refs = bufs[:2 * C_GROUP], bufs[2 * C_GROUP:]
    grp = pl.program_id(0)
    half = grp % 2
    lo = _lane_lo(BLK)
    zero = jnp.zeros((BLK, LANES), BF16)
    row = lax.broadcasted_iota(jnp.int32, (LANES, LANES), 0)
    col = lax.broadcasted_iota(jnp.int32, (LANES, LANES), 1)

    chunk = 512
    ones_row = jnp.where(lax.broadcasted_iota(jnp.int32, (ONES_ROWS, chunk), 0) == 0,
                         1.0, 0.0).astype(BF16)
    for a in range(2):
        shift = jnp.where(half == a, 0, HEAD_DIM)
        perm = jnp.where(col == (row + shift) % LANES, 1.0, 0.0).astype(BF16)
        for c in range(SEQ // chunk):
            rows = slice(c * chunk, (c + 1) * chunk)
            ka_ref[a, rows, :] = jnp.dot(k_ref[0, rows, :], perm,
                                         preferred_element_type=F32).astype(BF16)
            vt_ref[a, :LANES, rows] = _dot_nt(perm, v_ref[0, rows, :]).astype(BF16)
            vt_ref[a, LANES:, rows] = ones_row

    n_blk = SEQ // BLK
    rep = C_HEADS // C_KV_HEADS
    left = lax.broadcasted_iota(jnp.int32, (1, 2 * BLK), 1) < BLK
    upper = lax.broadcasted_iota(jnp.int32, (LANES, BLK), 0) < HEAD_DIM

    def window(n):
        return pl.multiple_of(jnp.clip(n * BLK - C_HALF_WINDOW, 0, SEQ - C_KEYS), BLK)

    def scores(n, s_ref, m_ref):
        r0 = pl.multiple_of(n * BLK, BLK)
        ws = window(n)
        var = jnp.where(n == 0, 0, jnp.where(n == n_blk - 1, 2, 1))
        for a in range(2):
            qa = [q_ref[0, pl.ds(r0, BLK), c * LANES:(c + 1) * LANES] for c in range(rep // 2)]
            qm = jnp.concatenate(
                [jnp.where(lo, q, zero) if a == 0 else jnp.where(lo, zero, q) for q in qa], axis=0)
            st = _dot_nt(ka_ref[a, pl.ds(ws, C_KEYS), :], qm) + bias_ref[0, a, var]
            s_ref[a] = st
            m_ref[a] = jnp.max(st, axis=0, keepdims=True)
            yield

    def finish(n, s_ref, m_ref):
        r0 = pl.multiple_of(n * BLK, BLK)
        ws = window(n)
        yts = []
        for a in range(2):
            m = m_ref[a]
            p = jnp.exp2(s_ref[a] - m).astype(BF16)
            acct = jnp.dot(vt_ref[a, :, pl.ds(ws, C_KEYS)], p, preferred_element_type=F32)
            l = acct[LANES:LANES + 1]
            sk = jnp.where(left, sink_ref[grp * rep + a], sink_ref[grp * rep + a + 2]) * LOG2E
            mx = jnp.maximum(m, sk)
            e = jnp.exp2(m - mx)
            yts.append(acct[:LANES] * (e / (l * e + jnp.exp2(sk - mx))))
            yield
        for c in range(rep // 2):
            cols = slice(c * LANES, (c + 1) * LANES)
            y = jnp.where(upper, yts[0][:, cols], yts[1][:, cols]).T
            gate = _silu(g_ref[0, pl.ds(r0, BLK), cols].astype(F32))
            o_ref[0, pl.ds(r0, BLK), cols] = (y * gate).astype(BF16)

    bufs = [(s_refs[i], m_refs[i]) for i in range(2 * C_GROUP)]
    set_a, set_b = bufs[:C_GROUP], bufs[C_GROUP:]
    n_grp = n_blk // C_GROUP

    def scores_of(g, bset):
        return [scores(g * C_GROUP + i, *bset[i]) for i in range(C_GROUP)]

    def finish_of(g, bset):
        return [finish(g * C_GROUP + i, *bset[i]) for i in range(C_GROUP)]

    _interleave(*scores_of(0, set_a))

    def body(j, carry):
        _interleave(*finish_of(2 * j, set_a), *scores_of(2 * j + 1, set_b))
        _interleave(*finish_of(2 * j + 1, set_b), *scores_of(2 * j + 2, set_a))
        return carry

    lax.fori_loop(0, n_grp // 2 - 1, body, 0, unroll=True)
    _interleave(*finish_of(n_grp - 2, set_a), *scores_of(n_grp - 1, set_b))
    _interleave(*finish_of(n_grp - 1, set_b))


def _mixer_c(proj, sink, bias):
    width = C_HEADS // C_KV_HEADS * HEAD_DIM
    kb = C_HEADS * HEAD_DIM // LANES
    vb = kb + C_KV_HEADS * HEAD_DIM // LANES
    gb = (C_HEADS + 2 * C_KV_HEADS) * HEAD_DIM // width
    return pl.pallas_call(
        _c_kernel,
        grid=(C_KV_HEADS, BATCH),
        in_specs=[
            pl.BlockSpec(memory_space=pltpu.SMEM),
            pl.BlockSpec((1, SEQ, width), lambda g, b: (b, 0, g)),
            pl.BlockSpec((1, SEQ, LANES), lambda g, b: (b, 0, kb + g // 2)),
            pl.BlockSpec((1, SEQ, LANES), lambda g, b: (b, 0, vb + g // 2)),
            pl.BlockSpec((1, SEQ, width), lambda g, b: (b, 0, gb + g)),
            pl.BlockSpec((1, 2, 3, C_KEYS, 2 * BLK), lambda g, b: (g, 0, 0, 0, 0)),
        ],
        out_specs=pl.BlockSpec((1, SEQ, width), lambda g, b: (b, 0, g)),
        out_shape=jax.ShapeDtypeStruct((BATCH, SEQ, C_HEADS * HEAD_DIM), BF16),
        scratch_shapes=[pltpu.VMEM((2, SEQ, LANES), BF16),
                        pltpu.VMEM((2, LANES + ONES_ROWS, SEQ), BF16)]
        + [pltpu.VMEM((2, C_KEYS, 2 * BLK), F32)] * (2 * C_GROUP)
        + [pltpu.VMEM((2, 1, 2 * BLK), F32)] * (2 * C_GROUP),
        compiler_params=pltpu.CompilerParams(
            dimension_semantics=("arbitrary", "arbitrary"), vmem_limit_bytes=VMEM_LIMIT),
        name="mixer_c_windowed",
    )(sink, proj, proj, proj, proj, bias)


@functools.lru_cache(maxsize=None)
def _bias_tables():
    def stack_pairs(t, first, second):
        return np.concatenate([t[first], t[second]], axis=2)

    sa = _alibi_slopes(A_HEADS)
    offs = (0, A_HALF, 2 * A_HALF)
    ev, od = slice(0, None, 2), slice(1, None, 2)
    b12 = np.stack(
        [stack_pairs(_band_bias(sa, dil, A_HALF, BLK, A_KEYS, offs), ev, od)
         for _, dil in A_PATTERNS[:2]], axis=1)
    b3 = stack_pairs(_band_bias(sa, A_PATTERNS[2][1], A_HALF, BLK, BLK, (0,)), ev, od)[:, 0]
    sc = _alibi_slopes(C_HEADS)
    bc_heads = _band_bias(sc, 1, C_HALF_WINDOW, BLK, C_KEYS,
                          (0, C_HALF_WINDOW, 2 * C_HALF_WINDOW))
    rep = C_HEADS // C_KV_HEADS
    bc = np.stack([stack_pairs(bc_heads, slice(a, None, rep), slice(a + 2, None, rep))
                   for a in range(2)], axis=1)
    bc = np.ascontiguousarray(np.swapaxes(bc, -1, -2))
    sb = _alibi_slopes(B_HEADS)
    il = np.arange(B_TQ)[None, :]
    uu = np.arange(2 * SEQ - B_TQ)[:, None]
    dist = np.abs(il - uu + (SEQ - B_TQ)).astype(np.float32)
    ub = -sb[:, None, None] * dist[None] * np.float32(LOG2E)
    return b12, b3, bc, ub.astype(np.float32)


def kernel(x, c, ada_w, ada_b, norm_g, ab_w_in, ab_w_out, diff_lq1, diff_lk1, diff_lq2,
           diff_lk2, diff_subln_g, c_w_in, c_w_out, c_sink, final_g):
    b12, b3, bc, ub = (jnp.asarray(t) for t in _bias_tables())
    mod = _modulation(c, ada_w, ada_b)
    a_w = A_HEADS * HEAD_DIM
    q_even = ((0, a_w), (3 * a_w, 3 * a_w + B_HEADS * 2 * HEAD_DIM))
    q_odd = ((0, C_HEADS * HEAD_DIM),)

    def in_stage(layer):
        shift = mod[layer, :, :D_MODEL]
        scale = mod[layer, :, D_MODEL:2 * D_MODEL]
        w, q_cols = (ab_w_in, q_even) if layer % 2 == 0 else (c_w_in, q_odd)
        return norm_g[layer], scale, shift, w[layer // 2].astype(BF16), q_cols

    _, proj = _proj(x, in_stage=in_stage(0))
    for layer in range(DEPTH):
        gate = mod[layer, :, 2 * D_MODEL:]
        j = layer // 2
        last = layer == DEPTH - 1
        if layer % 2 == 0:
            lam_init = 0.8 - 0.6 * math.exp(-0.3 * layer)
            za = _mixer_a(proj, b12, b3)
            zb = _mixer_b(proj, ub, diff_lq1[j], diff_lk1[j], diff_lq2[j], diff_lk2[j],
                          diff_subln_g[j], lam_init)
            zs, w_out = [za, zb], ab_w_out[j]
        else:
            zs, w_out = [_mixer_c(proj, c_sink[j], bc)], c_w_out[j]
        x, proj = _proj(x, out_stage=(gate, zs, w_out.astype(BF16)),
                        final_g=final_g if last else None,
                        in_stage=None if last else in_stage(layer + 1))
    return x
```

```python
import functools
import math

import numpy as np
import jax
import jax.numpy as jnp
from jax import lax
from jax.experimental import pallas as pl
from jax.experimental.pallas import tpu as pltpu

D_MODEL = 1024
BATCH = 8
SEQ = 2048
DEPTH = 4
HEAD_DIM = 64
LANES = 128
A_HEADS = 8
A_PATTERNS = ((128, 1), (512, 4), (2048, 16))
A_HALF = 64
B_HEADS = 4
C_HEADS = 16
C_KV_HEADS = 4
C_HALF_WINDOW = 128
EVEN_IN = 4096
ODD_IN = 2560
EPS = 1e-6
NEG_INF = -1e30
LOG2E = math.log2(math.e)
Q_FOLD = HEAD_DIM ** -0.5 * LOG2E
ONES_ROWS = 16

F32 = jnp.float32
BF16 = jnp.bfloat16

ROW_TILE = 512
COL_TILE = 512
BLK = 128
A_KEYS = BLK + 2 * A_HALF
C_KEYS = BLK + 2 * C_HALF_WINDOW
A_MERGE_ROWS = 256
C_GROUP = 2
B_TQ = 256
B_TK = 256
B_NB = 2
VMEM_LIMIT = 56 * 1024 * 1024


def _silu(t):
    return t * (1.0 / (1.0 + jnp.exp(-t)))


def _dot_nt(a, b):
    return lax.dot_general(a, b, (((1,), (1,)), ((), ())), preferred_element_type=F32)


def _alibi_slopes(n):
    return (2.0 ** (-8.0 * np.arange(1, n + 1, dtype=np.float32) / n)).astype(np.float32)


def _band_bias(slopes, spacing, half, tq, tk, offsets):
    i = np.arange(tq)[:, None]
    c = np.arange(tk)[None, :]
    out = np.empty((len(slopes), len(offsets), tq, tk), np.float32)
    for v, off in enumerate(offsets):
        rel = np.abs(c - (i + off))
        dist = (rel * spacing).astype(np.float32)
        for h, m in enumerate(slopes):
            out[h, v] = np.where(rel <= half, -m * dist * np.float32(LOG2E), np.float32(NEG_INF))
    return out


def _mod_kernel(c_ref, w_ref, b_ref, o_ref):
    cs = _silu(c_ref[...])
    o_ref[0] = jnp.dot(cs, w_ref[0], preferred_element_type=F32,
                       precision=lax.Precision.HIGHEST) + b_ref[0]


def _modulation(c, ada_w, ada_b):
    nblk = 3 * D_MODEL // D_MODEL
    return pl.pallas_call(
        _mod_kernel,
        grid=(DEPTH, nblk),
        in_specs=[
            pl.BlockSpec((BATCH, D_MODEL), lambda l, j: (0, 0)),
            pl.BlockSpec((1, D_MODEL, D_MODEL), lambda l, j: (l, 0, j)),
            pl.BlockSpec((1, 1, D_MODEL), lambda l, j: (l, 0, j)),
        ],
        out_specs=pl.BlockSpec((1, BATCH, D_MODEL), lambda l, j: (l, 0, j)),
        out_shape=jax.ShapeDtypeStruct((DEPTH, BATCH, 3 * D_MODEL), F32),
        name="adaln_mod",
    )(c, ada_w, ada_b.reshape(DEPTH, 1, 3 * D_MODEL))


def _rms(x, g):
    ms = jnp.mean(x * x, axis=-1, keepdims=True)
    return x * lax.rsqrt(ms + EPS) * g


def _proj_kernel(*refs, n_z, final, n_out, q_cols):
    refs = iter(refs)
    x_ref = next(refs)
    if n_z:
        gate_ref = next(refs)
        z_refs = [next(refs) for _ in range(n_z)]
        wo_ref = next(refs)
    if final:
        fg_ref = next(refs)
    if n_out:
        g_ref, sc_ref, sh_ref, wi_ref = (next(refs) for _ in range(4))
    if n_z:
        xo_ref = next(refs)
    if n_out:
        proj_ref, h_ref = next(refs), next(refs)

    xn = x_ref[0]
    if n_z:
        kz = D_MODEL // n_z
        acc = jnp.dot(z_refs[0][0], wo_ref[0:kz, :], preferred_element_type=F32)
        for i in range(1, n_z):
            acc = acc + jnp.dot(z_refs[i][0], wo_ref[i * kz:(i + 1) * kz, :],
                                preferred_element_type=F32)
        xn = xn + gate_ref[0] * acc
        xo_ref[0] = _rms(xn, fg_ref[...]) if final else xn
    if n_out:
        h_ref[...] = (_rms(xn, g_ref[...]) * (1.0 + sc_ref[0]) + sh_ref[0]).astype(BF16)
        for j in range(n_out // COL_TILE):
            lo, hi = j * COL_TILE, (j + 1) * COL_TILE
            acc = jnp.dot(h_ref[...], wi_ref[:, lo:hi], preferred_element_type=F32)
            if any(q0 <= lo and hi <= q1 for q0, q1 in q_cols):
                acc = acc * Q_FOLD
            proj_ref[0, :, lo:hi] = acc.astype(BF16)


def _proj(x, out_stage=None, final_g=None, in_stage=None):
    row = pl.BlockSpec((1, ROW_TILE, D_MODEL), lambda b, i: (b, i, 0))
    per_batch = pl.BlockSpec((1, 1, D_MODEL), lambda b, i: (b, 0, 0))
    vec = pl.BlockSpec((1, D_MODEL), lambda b, i: (0, 0))
    whole = lambda shape: pl.BlockSpec(shape, lambda b, i: (0, 0))
    in_specs, args, out_specs, out_shape, scratch = [row], [x], [], [], []
    n_z = n_out = 0
    q_cols = ()
    assert final_g is None or out_stage is not None
    if out_stage is not None:
        gate, zs, w_out = out_stage
        n_z = len(zs)
        in_specs += [per_batch]
        in_specs += [pl.BlockSpec((1, ROW_TILE, D_MODEL // n_z), lambda b, i: (b, i, 0))] * n_z
        in_specs += [whole((D_MODEL, D_MODEL))]
        args += [gate.reshape(BATCH, 1, D_MODEL), *zs, w_out]
        out_specs += [row]
        out_shape += [jax.ShapeDtypeStruct((BATCH, SEQ, D_MODEL), F32)]
    if final_g is not None:
        in_specs += [vec]
        args += [final_g.reshape(1, D_MODEL)]
    if in_stage is not None:
        g, scale, shift, w_in, q_cols = in_stage
        n_out = w_in.shape[1]
        assert all(q0 % COL_TILE == 0 and q1 % COL_TILE == 0 for q0, q1 in q_cols)
        in_specs += [vec, per_batch, per_batch, whole((D_MODEL, n_out))]
        args += [g.reshape(1, D_MODEL), scale.reshape(BATCH, 1, D_MODEL),
                 shift.reshape(BATCH, 1, D_MODEL), w_in]
        out_specs += [pl.BlockSpec((1, ROW_TILE, n_out), lambda b, i: (b, i, 0))]
        out_shape += [jax.ShapeDtypeStruct((BATCH, SEQ, n_out), BF16)]
        scratch += [pltpu.VMEM((ROW_TILE, D_MODEL), BF16)]
    outs = pl.pallas_call(
        functools.partial(_proj_kernel, n_z=n_z, final=final_g is not None, n_out=n_out,
                          q_cols=q_cols),
        grid=(BATCH, SEQ // ROW_TILE),
        in_specs=in_specs,
        out_specs=out_specs,
        out_shape=out_shape,
        scratch_shapes=scratch,
        compiler_params=pltpu.CompilerParams(
            dimension_semantics=("arbitrary", "arbitrary"), vmem_limit_bytes=VMEM_LIMIT),
        name="proj",
    )(*args)
    outs = list(outs)
    x_new = outs.pop(0) if out_stage is not None else None
    proj = outs.pop(0) if in_stage is not None else None
    return x_new, proj


def _lane_lo(rows):
    return lax.broadcasted_iota(jnp.int32, (rows, LANES), 1) < HEAD_DIM


def _softmax_block(qm, kw, vw, bias):
    s = _dot_nt(qm, kw) + bias
    m = jnp.max(s, axis=-1, keepdims=True)
    p = jnp.exp2(s - m)
    l = jnp.sum(p, axis=-1, keepdims=True)
    acc = jnp.dot(p.astype(BF16), vw, preferred_element_type=F32)
    return acc, m, l


def _interleave(*stages):
    live = list(stages)
    while live:
        for g in list(live):
            try:
                next(g)
            except StopIteration:
                live.remove(g)


def _stack_heads(qb, lo):
    zero = jnp.zeros_like(qb)
    return jnp.concatenate([jnp.where(lo, qb, zero), jnp.where(lo, zero, qb)], axis=0)


A_SCRATCH = (
    [pltpu.VMEM((SEQ, LANES), F32)] * 6
    + [pltpu.VMEM((SEQ, LANES), BF16)] * 6
    + [pltpu.VMEM((3, SEQ, LANES), F32)] * 3
)


def _a_stages(q_ref, k_ref, v_ref, g_ref, b12_ref, b3_ref, o_ref,
              qf_ref, kf_ref, vf_ref, q4_ref, k4_ref, v4_ref, qd_ref, kd_ref, vd_ref,
              qe_ref, ke_ref, ve_ref, acc_ref, m_ref, l_ref):
    lo = _lane_lo(BLK)

    def pair_block(qb, kw, vw, bias):
        acc, m, l = _softmax_block(_stack_heads(qb, lo), kw, vw, bias)
        return (jnp.where(lo, acc[:BLK], acc[BLK:]), jnp.where(lo, m[:BLK], m[BLK:]),
                jnp.where(lo, l[:BLK], l[BLK:]))

    n_blk = SEQ // BLK
    dil1 = A_PATTERNS[1][1]
    seg1 = SEQ // dil1
    per1 = seg1 // BLK
    dil2 = A_PATTERNS[2][1]
    sub = dil2 // dil1
    assert sub * dil1 == dil2
    seg2 = SEQ // dil2
    streams = ((qf_ref, q4_ref, qd_ref, qe_ref), (kf_ref, k4_ref, kd_ref, ke_ref),
               (vf_ref, v4_ref, vd_ref, ve_ref))

    def prologue():
        chunk = 512
        for c in range(SEQ // chunk):
            rows = slice(c * chunk, (c + 1) * chunk)
            qf_ref[rows, :] = q_ref[0, rows, :].astype(F32)
            kf_ref[rows, :] = k_ref[0, rows, :].astype(F32)
            vf_ref[rows, :] = v_ref[0, rows, :].astype(F32)
            yield
        for r in range(dil1):
            dst = slice(r * seg1, (r + 1) * seg1)
            for src_ref, mid_ref, dst_ref, _ in streams:
                t = src_ref[pl.ds(r, seg1, stride=dil1), :]
                mid_ref[dst, :] = t
                dst_ref[dst, :] = t.astype(BF16)
            yield
        for r in range(dil1):
            for r2 in range(sub):
                res = dil1 * r2 + r
                dst = slice(res * seg2, (res + 1) * seg2)
                for _, mid_ref, _, dst_ref in streams:
                    dst_ref[dst, :] = (
                        mid_ref[pl.ds(r * seg1 + r2, seg2, stride=sub), :].astype(BF16))
            yield

    def aligned(x, m):
        return x if isinstance(x, int) else pl.multiple_of(x, m)

    def clamp(x, hi):
        return min(max(x, 0), hi) if isinstance(x, int) else jnp.clip(x, 0, hi)

    def variant(blk, last):
        if isinstance(blk, int):
            return 0 if blk == 0 else (2 if blk == last else 1)
        return jnp.where(blk == 0, 0, jnp.where(blk == last, 2, 1))

    def p0_body(j):
        r0 = aligned(j * BLK, BLK)
        ws = aligned(clamp(j * BLK - A_HALF, SEQ - A_KEYS), A_HALF)
        var = variant(j, n_blk - 1)
        acc, m, l = pair_block(q_ref[0, pl.ds(r0, BLK), :],
                               k_ref[0, pl.ds(ws, A_KEYS), :], v_ref[0, pl.ds(ws, A_KEYS), :],
                               b12_ref[0, 0, var])
        acc_ref[0, pl.ds(r0, BLK), :] = acc
        m_ref[0, pl.ds(r0, BLK), :] = m
        l_ref[0, pl.ds(r0, BLK), :] = l

    def p1_body(j):
        r = j // per1
        blk = j % per1
        r0 = aligned(j * BLK, BLK)
        ws = aligned(r * seg1 + clamp(blk * BLK - A_HALF, seg1 - A_KEYS), A_HALF)
        var = variant(blk, per1 - 1)
        acc, m, l = pair_block(qd_ref[pl.ds(r0, BLK), :],
                               kd_ref[pl.ds(ws, A_KEYS), :], vd_ref[pl.ds(ws, A_KEYS), :],
                               b12_ref[0, 1, var])
        dst = pl.ds(blk * BLK * dil1 + r, BLK, stride=dil1)
        acc_ref[1, dst, :] = acc
        m_ref[1, dst, :] = m
        l_ref[1, dst, :] = l

    def p2_body(j):
        r0 = aligned(j * BLK, BLK)
        acc, m, l = pair_block(qe_ref[pl.ds(r0, BLK), :],
                               ke_ref[pl.ds(r0, BLK), :], ve_ref[pl.ds(r0, BLK), :],
                               b3_ref[0])
        dst = pl.ds(j, BLK, stride=dil2)
        acc_ref[2, dst, :] = acc
        m_ref[2, dst, :] = m
        l_ref[2, dst, :] = l

    bodies = (p0_body, p1_body, p2_body)

    def blocks(work):
        for pattern, j in work:
            bodies[pattern](j)
            yield

    def merge(c):
        rows = pl.ds(aligned(c * A_MERGE_ROWS, A_MERGE_ROWS), A_MERGE_ROWS)
        m0, m1, m2 = m_ref[0, rows, :], m_ref[1, rows, :], m_ref[2, rows, :]
        mx = jnp.maximum(jnp.maximum(m0, m1), m2)
        w0, w1, w2 = jnp.exp2(m0 - mx), jnp.exp2(m1 - mx), jnp.exp2(m2 - mx)
        num = w0 * acc_ref[0, rows, :] + w1 * acc_ref[1, rows, :] + w2 * acc_ref[2, rows, :]
        den = w0 * l_ref[0, rows, :] + w1 * l_ref[1, rows, :] + w2 * l_ref[2, rows, :]
        y = num / den
        o_ref[0, rows, :] = (y * _silu(g_ref[0, rows, :].astype(F32))).astype(BF16)

    return prologue, blocks, merge


B_SCRATCH = (
    [pltpu.VMEM((B_NB, LANES + ONES_ROWS, SEQ), BF16)]
    + [pltpu.VMEM((SEQ, 2 * B_TQ), F32)] * 2
    + [pltpu.VMEM((1, 2 * B_TQ), F32)] * 2
    + [pltpu.VMEM((LANES + ONES_ROWS, 2 * B_TQ), F32)] * 2
)


def _b_stages(q_ref, k_ref, v_ref, g_ref, u_ref, lq1_ref, lk1_ref, lq2_ref, lk2_ref,
              sg_ref, o_ref, vt_ref, sa_ref, sb_ref, ma_ref, mb_ref, acca_ref, accb_ref,
              *, lam_init):
    lo = _lane_lo(B_TQ)
    n_t = SEQ // B_TK
    lam = (jnp.exp(jnp.sum(lq1_ref[...] * lk1_ref[...], axis=-1, keepdims=True))
           - jnp.exp(jnp.sum(lq2_ref[...] * lk2_ref[...], axis=-1, keepdims=True)) + lam_init)

    eye = jnp.where(lax.broadcasted_iota(jnp.int32, (LANES, LANES), 0)
                    == lax.broadcasted_iota(jnp.int32, (LANES, LANES), 1), 1.0, 0.0).astype(BF16)
    ones_row = jnp.where(lax.broadcasted_iota(jnp.int32, (ONES_ROWS, B_TK), 0) == 0,
                         1.0, 0.0).astype(BF16)
    for bi in range(B_NB):
        for t in range(n_t):
            rows = slice(t * B_TK, (t + 1) * B_TK)
            vt_ref[bi, :LANES, rows] = _dot_nt(eye, v_ref[bi, rows, :]).astype(BF16)
            vt_ref[bi, LANES:, rows] = ones_row

    n_q = SEQ // B_TQ

    def locate(i):
        if isinstance(i, int):
            return i // n_q, (i % n_q) * B_TQ
        return i // n_q, pl.multiple_of((i % n_q) * B_TQ, B_TQ)

    def scores(i, s_ref, m_ref):
        bi, r0 = locate(i)
        qst = _stack_heads(q_ref[bi, pl.ds(r0, B_TQ), :], lo)
        m = None
        for t in range(n_t):
            rows = slice(t * B_TK, (t + 1) * B_TK)
            off = pl.multiple_of(t * B_TK + (SEQ - B_TQ) - r0, LANES)
            ub = u_ref[0, pl.ds(off, B_TK), :]
            st = _dot_nt(k_ref[bi, rows, :], qst)
            s1 = st[:, :B_TQ] + ub
            s2 = st[:, B_TQ:] + ub
            s_ref[rows, :B_TQ] = s1
            s_ref[rows, B_TQ:] = s2
            f = jnp.concatenate([jnp.max(s1, axis=0, keepdims=True),
                                 jnp.max(s2, axis=0, keepdims=True)], axis=1)
            m = f if m is None else jnp.maximum(m, f)
            yield
        m_ref[...] = m

    def softmax_pv(i, s_ref, m_ref, acc_ref):
        bi, _ = locate(i)
        m = m_ref[...]
        acct = None
        for t in range(n_t):
            rows = slice(t * B_TK, (t + 1) * B_TK)
            p = jnp.exp2(s_ref[rows, :] - m).astype(BF16)
            d = jnp.dot(vt_ref[bi, :, rows], p, preferred_element_type=F32)
            acct = d if acct is None else acct + d
            yield
        acc_ref[...] = acct

    def epilogue(i, acc_ref):
        bi, r0 = locate(i)
        ot = acc_ref[:LANES, :] * (1.0 / acc_ref[LANES:LANES + 1, :])
        outt = ot[:, :B_TQ] - lam * ot[:, B_TQ:]
        ms = jnp.mean(outt * outt, axis=0, keepdims=True)
        yt = outt * lax.rsqrt(ms + EPS) * sg_ref[...] * (1.0 - lam_init)
        yield
        gate = _silu(g_ref[bi, pl.ds(r0, B_TQ), :].astype(F32))
        o_ref[bi, pl.ds(r0, B_TQ), :] = (yt.T * gate).astype(BF16)

    return scores, softmax_pv, epilogue, (sa_ref, ma_ref, acca_ref), (sb_ref, mb_ref, accb_ref)


def _b_kernel(*refs, lam_init):
    scores, softmax_pv, epilogue, (sa, ma, acca), (sb, mb, accb) = _b_stages(
        *refs, lam_init=lam_init)
    n = B_NB * (SEQ // B_TQ)
    _interleave(scores(0, sa, ma))
    _interleave(scores(1, sb, mb), softmax_pv(0, sa, ma, acca))

    def body(j, carry):
        i = 2 * j
        _interleave(scores(i + 2, sa, ma), softmax_pv(i + 1, sb, mb, accb), epilogue(i, acca))
        _interleave(scores(i + 3, sb, mb), softmax_pv(i + 2, sa, ma, acca), epilogue(i + 1, accb))
        return carry

    lax.fori_loop(0, n // 2 - 1, body, 0)
    _interleave(softmax_pv(n - 1, sb, mb, accb), epilogue(n - 2, acca))
    _interleave(epilogue(n - 1, accb))


def _a_kernel(*refs):
    prologue, blocks, merge = _a_stages(*refs)
    _interleave(prologue())
    _interleave(blocks([(p, j) for p in range(len(A_PATTERNS)) for j in range(SEQ // BLK)]))

    def merge_body(c, carry):
        merge(c)
        return carry

    lax.fori_loop(0, SEQ // A_MERGE_ROWS, merge_body, 0)


def _mixer_ab(proj, bias12, bias3, u, lq1, lk1, lq2, lk2, subln_g, lam_init):
    pairs = A_HEADS // 2
    a_w = A_HEADS * HEAD_DIM // LANES
    b_q = 3 * a_w
    g0 = (3 * A_HEADS * HEAD_DIM + 3 * B_HEADS * 2 * HEAD_DIM) // LANES
    blk = lambda off: pl.BlockSpec((1, SEQ, LANES), lambda i, b: (b, 0, off + i))
    vec = lambda n: pl.BlockSpec((1, n), lambda i, b: (0, 0))
    out = pl.BlockSpec((1, SEQ, LANES), lambda i, b: (b, 0, i))
    params = pltpu.CompilerParams(
        dimension_semantics=("arbitrary", "arbitrary"), vmem_limit_bytes=VMEM_LIMIT)
    za = pl.pallas_call(
        _a_kernel,
        grid=(pairs, BATCH),
        in_specs=[
            blk(0), blk(a_w), blk(2 * a_w), blk(g0),
            pl.BlockSpec((1, 2, 3, 2 * BLK, A_KEYS), lambda i, b: (i, 0, 0, 0, 0)),
            pl.BlockSpec((1, 2 * BLK, BLK), lambda i, b: (i, 0, 0)),
        ],
        out_specs=out,
        out_shape=jax.ShapeDtypeStruct((BATCH, SEQ, A_HEADS * HEAD_DIM), BF16),
        scratch_shapes=A_SCRATCH,
        compiler_params=params,
        name="mixer_a_dilated",
    )(proj, proj, proj, proj, bias12, bias3)
    blk = lambda off: pl.BlockSpec((B_NB, SEQ, LANES), lambda i, b: (b, 0, off + i))
    zb = pl.pallas_call(
        functools.partial(_b_kernel, lam_init=lam_init),
        grid=(B_HEADS, BATCH // B_NB),
        in_specs=[
            blk(b_q), blk(b_q + B_HEADS), blk(b_q + 2 * B_HEADS), blk(g0 + a_w),
            pl.BlockSpec((1, 2 * SEQ - B_TQ, B_TQ), lambda i, b: (i, 0, 0)),
            vec(HEAD_DIM), vec(HEAD_DIM), vec(HEAD_DIM), vec(HEAD_DIM),
            pl.BlockSpec((2 * HEAD_DIM, 1), lambda i, b: (0, 0)),
        ],
        out_specs=blk(0),
        out_shape=jax.ShapeDtypeStruct((BATCH, SEQ, B_HEADS * 2 * HEAD_DIM), BF16),
        scratch_shapes=B_SCRATCH,
        compiler_params=params,
        name="mixer_b_differential",
    )(proj, proj, proj, proj, u, lq1.reshape(1, -1), lk1.reshape(1, -1),
      lq2.reshape(1, -1), lk2.reshape(1, -1), subln_g.reshape(-1, 1))
    return za, zb


def _c_kernel(sink_ref, q_ref, k_ref, v_ref, g_ref, bias_ref, o_ref, ka_ref, vt_ref, *bufs):
    s_refs, m_refs = bufs[:2 * C_GROUP], bufs[2 * C_GROUP:]
    grp = pl.program_id(0)
    half = grp % 2
    lo = _lane_lo(BLK)
    zero = jnp.zeros((BLK, LANES), BF16)
    row = lax.broadcasted_iota(jnp.int32, (LANES, LANES), 0)
    col = lax.broadcasted_iota(jnp.int32, (LANES, LANES), 1)

    chunk = 512
    ones_row = jnp.where(lax.broadcasted_iota(jnp.int32, (ONES_ROWS, chunk), 0) == 0,
                         1.0, 0.0).astype(BF16)
    for a in range(2):
        shift = jnp.where(half == a, 0, HEAD_DIM)
        perm = jnp.where(col == (row + shift) % LANES, 1.0, 0.0).astype(BF16)
        for c in range(SEQ // chunk):
            rows = slice(c * chunk, (c + 1) * chunk)
            ka_ref[a, rows, :] = jnp.dot(k_ref[0, rows, :], perm,
                                         preferred_element_type=F32).astype(BF16)
            vt_ref[a, :LANES, rows] = _dot_nt(perm, v_ref[0, rows, :]).astype(BF16)
            vt_ref[a, LANES:, rows] = ones_row

    n_blk = SEQ // BLK
    rep = C_HEADS // C_KV_HEADS
    left = lax.broadcasted_iota(jnp.int32, (1, 2 * BLK), 1) < BLK
    upper = lax.broadcasted_iota(jnp.int32, (LANES, BLK), 0) < HEAD_DIM

    def window(n):
        return pl.multiple_of(jnp.clip(n * BLK - C_HALF_WINDOW, 0, SEQ - C_KEYS), BLK)

    def scores(n, s_ref, m_ref):
        r0 = pl.multiple_of(n * BLK, BLK)
        ws = window(n)
        var = jnp.where(n == 0, 0, jnp.where(n == n_blk - 1, 2, 1))
        for a in range(2):
            qa = [q_ref[0, pl.ds(r0, BLK), c * LANES:(c + 1) * LANES] for c in range(rep // 2)]
            qm = jnp.concatenate(
                [jnp.where(lo, q, zero) if a == 0 else jnp.where(lo, zero, q) for q in qa], axis=0)
            st = _dot_nt(ka_ref[a, pl.ds(ws, C_KEYS), :], qm) + bias_ref[0, a, var]
            s_ref[a] = st
            m_ref[a] = jnp.max(st, axis=0, keepdims=True)
            yield

    def finish(n, s_ref, m_ref):
        r0 = pl.multiple_of(n * BLK, BLK)
        ws = window(n)
        yts = []
        for a in range(2):
            m = m_ref[a]
            p = jnp.exp2(s_ref[a] - m).astype(BF16)
            acct = jnp.dot(vt_ref[a, :, pl.ds(ws, C_KEYS)], p, preferred_element_type=F32)
            l = acct[LANES:LANES + 1]
            sk = jnp.where(left, sink_ref[grp * rep + a], sink_ref[grp * rep + a + 2]) * LOG2E
            mx = jnp.maximum(m, sk)
            e = jnp.exp2(m - mx)
            yts.append(acct[:LANES] * (e / (l * e + jnp.exp2(sk - mx))))
            yield
        for c in range(rep // 2):
            cols = slice(c * LANES, (c + 1) * LANES)
            y = jnp.where(upper, yts[0][:, cols], yts[1][:, cols]).T
            gate = _silu(g_ref[0, pl.ds(r0, BLK), cols].astype(F32))
            o_ref[0, pl.ds(r0, BLK), cols] = (y * gate).astype(BF16)

    bufs = [(s_refs[i], m_refs[i]) for i in range(2 * C_GROUP)]
    set_a, set_b = bufs[:C_GROUP], bufs[C_GROUP:]
    n_grp = n_blk // C_GROUP

    def scores_of(g, bset):
        return [scores(g * C_GROUP + i, *bset[i]) for i in range(C_GROUP)]

    def finish_of(g, bset):
        return [finish(g * C_GROUP + i, *bset[i]) for i in range(C_GROUP)]

    _interleave(*scores_of(0, set_a))

    def body(j, carry):
        _interleave(*finish_of(2 * j, set_a), *scores_of(2 * j + 1, set_b))
        _interleave(*finish_of(2 * j + 1, set_b), *scores_of(2 * j + 2, set_a))
        return carry

    lax.fori_loop(0, n_grp // 2 - 1, body, 0, unroll=True)
    _interleave(*finish_of(n_grp - 2, set_a), *scores_of(n_grp - 1, set_b))
    _interleave(*finish_of(n_grp - 1, set_b))


def _mixer_c(proj, sink, bias):
    width = C_HEADS // C_KV_HEADS * HEAD_DIM
    kb = C_HEADS * HEAD_DIM // LANES
    vb = kb + C_KV_HEADS * HEAD_DIM // LANES
    gb = (C_HEADS + 2 * C_KV_HEADS) * HEAD_DIM // width
    return pl.pallas_call(
        _c_kernel,
        grid=(C_KV_HEADS, BATCH),
        in_specs=[
            pl.BlockSpec(memory_space=pltpu.SMEM),
            pl.BlockSpec((1, SEQ, width), lambda g, b: (b, 0, g)),
            pl.BlockSpec((1, SEQ, LANES), lambda g, b: (b, 0, kb + g // 2)),
            pl.BlockSpec((1, SEQ, LANES), lambda g, b: (b, 0, vb + g // 2)),
            pl.BlockSpec((1, SEQ, width), lambda g, b: (b, 0, gb + g)),
            pl.BlockSpec((1, 2, 3, C_KEYS, 2 * BLK), lambda g, b: (g, 0, 0, 0, 0)),
        ],
        out_specs=pl.BlockSpec((1, SEQ, width), lambda g, b: (b, 0, g)),
        out_shape=jax.ShapeDtypeStruct((BATCH, SEQ, C_HEADS * HEAD_DIM), BF16),
        scratch_shapes=[pltpu.VMEM((2, SEQ, LANES), BF16),
                        pltpu.VMEM((2, LANES + ONES_ROWS, SEQ), BF16)]
        + [pltpu.VMEM((2, C_KEYS, 2 * BLK), F32)] * (2 * C_GROUP)
        + [pltpu.VMEM((2, 1, 2 * BLK), F32)] * (2 * C_GROUP),
        compiler_params=pltpu.CompilerParams(
            dimension_semantics=("arbitrary", "arbitrary"), vmem_limit_bytes=VMEM_LIMIT),
        name="mixer_c_windowed",
    )(sink, proj, proj, proj, proj, bias)


@functools.lru_cache(maxsize=None)
def _bias_tables():
    def stack_pairs(t, first, second):
        return np.concatenate([t[first], t[second]], axis=2)

    sa = _alibi_slopes(A_HEADS)
    offs = (0, A_HALF, 2 * A_HALF)
    ev, od = slice(0, None, 2), slice(1, None, 2)
    b12 = np.stack(
        [stack_pairs(_band_bias(sa, dil, A_HALF, BLK, A_KEYS, offs), ev, od)
         for _, dil in A_PATTERNS[:2]], axis=1)
    b3 = stack_pairs(_band_bias(sa, A_PATTERNS[2][1], A_HALF, BLK, BLK, (0,)), ev, od)[:, 0]
    sc = _alibi_slopes(C_HEADS)
    bc_heads = _band_bias(sc, 1, C_HALF_WINDOW, BLK, C_KEYS,
                          (0, C_HALF_WINDOW, 2 * C_HALF_WINDOW))
    rep = C_HEADS // C_KV_HEADS
    bc = np.stack([stack_pairs(bc_heads, slice(a, None, rep), slice(a + 2, None, rep))
                   for a in range(2)], axis=1)
    bc = np.ascontiguousarray(np.swapaxes(bc, -1, -2))
    sb = _alibi_slopes(B_HEADS)
    il = np.arange(B_TQ)[None, :]
    uu = np.arange(2 * SEQ - B_TQ)[:, None]
    dist = np.abs(il - uu + (SEQ - B_TQ)).astype(np.float32)
    ub = -sb[:, None, None] * dist[None] * np.float32(LOG2E)
    return b12, b3, bc, ub.astype(np.float32)


def kernel(x, c, ada_w, ada_b, norm_g, ab_w_in, ab_w_out, diff_lq1, diff_lk1, diff_lq2,
           diff_lk2, diff_subln_g, c_w_in, c_w_out, c_sink, final_g):
    b12, b3, bc, ub = (jnp.asarray(t) for t in _bias_tables())
    mod = _modulation(c, ada_w, ada_b)
    a_w = A_HEADS * HEAD_DIM
    q_even = ((0, a_w), (3 * a_w, 3 * a_w + B_HEADS * 2 * HEAD_DIM))
    q_odd = ((0, C_HEADS * HEAD_DIM),)

    def in_stage(layer):
        shift = mod[layer, :, :D_MODEL]
        scale = mod[layer, :, D_MODEL:2 * D_MODEL]
        w, q_cols = (ab_w_in, q_even) if layer % 2 == 0 else (c_w_in, q_odd)
        return norm_g[layer], scale, shift, w[layer // 2].astype(BF16), q_cols

    _, proj = _proj(x, in_stage=in_stage(0))
    for layer in range(DEPTH):
        gate = mod[layer, :, 2 * D_MODEL:]
        j = layer // 2
        last = layer == DEPTH - 1
        if layer % 2 == 0:
            lam_init = 0.8 - 0.6 * math.exp(-0.3 * layer)
            za, zb = _mixer_ab(proj, b12, b3, ub, diff_lq1[j], diff_lk1[j], diff_lq2[j],
                               diff_lk2[j], diff_subln_g[j], lam_init)
            zs, w_out = [za, zb], ab_w_out[j]
        else:
            zs, w_out = [_mixer_c(proj, c_sink[j], bc)], c_w_out[j]
        x, proj = _proj(x, out_stage=(gate, zs, w_out.astype(BF16)),
                        final_g=final_g if last else None,
                        in_stage=None if last else in_stage(layer + 1))
    return x
```

```python
import functools
import math

import numpy as np
import jax
import jax.numpy as jnp
from jax import lax
from jax.experimental import pallas as pl
from jax.experimental.pallas import tpu as pltpu

D_MODEL = 1024
BATCH = 8
SEQ = 2048
DEPTH = 4
HEAD_DIM = 64
LANES = 128
A_HEADS = 8
A_PATTERNS = ((128, 1), (512, 4), (2048, 16))
A_HALF = 64
B_HEADS = 4
C_HEADS = 16
C_KV_HEADS = 4
C_HALF_WINDOW = 128
EVEN_IN = 4096
ODD_IN = 2560
EPS = 1e-6
NEG_INF = -1e30
LOG2E = math.log2(math.e)
Q_FOLD = HEAD_DIM ** -0.5 * LOG2E
ONES_ROWS = 16

F32 = jnp.float32
BF16 = jnp.bfloat16

ROW_TILE = 512
COL_TILE = 512
BLK = 128
A_KEYS = BLK + 2 * A_HALF
C_KEYS = BLK + 2 * C_HALF_WINDOW
A_MERGE_ROWS = 256
C_GROUP = 2
C_NB = 2
B_TQ = 256
B_TK = 256
B_NB = 4
VMEM_LIMIT = 56 * 1024 * 1024


def _silu(t):
    return t * (1.0 / (1.0 + jnp.exp(-t)))


def _dot_nt(a, b):
    return lax.dot_general(a, b, (((1,), (1,)), ((), ())), preferred_element_type=F32)


def _alibi_slopes(n):
    return (2.0 ** (-8.0 * np.arange(1, n + 1, dtype=np.float32) / n)).astype(np.float32)


def _band_bias(slopes, spacing, half, tq, tk, offsets):
    i = np.arange(tq)[:, None]
    c = np.arange(tk)[None, :]
    out = np.empty((len(slopes), len(offsets), tq, tk), np.float32)
    for v, off in enumerate(offsets):
        rel = np.abs(c - (i + off))
        dist = (rel * spacing).astype(np.float32)
        for h, m in enumerate(slopes):
            out[h, v] = np.where(rel <= half, -m * dist * np.float32(LOG2E), np.float32(NEG_INF))
    return out


def _mod_kernel(c_ref, w_ref, b_ref, o_ref):
    cs = _silu(c_ref[...])
    o_ref[0] = jnp.dot(cs, w_ref[0], preferred_element_type=F32,
                       precision=lax.Precision.HIGHEST) + b_ref[0]


def _modulation(c, ada_w, ada_b):
    nblk = 3 * D_MODEL // D_MODEL
    return pl.pallas_call(
        _mod_kernel,
        grid=(DEPTH, nblk),
        in_specs=[
            pl.BlockSpec((BATCH, D_MODEL), lambda l, j: (0, 0)),
            pl.BlockSpec((1, D_MODEL, D_MODEL), lambda l, j: (l, 0, j)),
            pl.BlockSpec((1, 1, D_MODEL), lambda l, j: (l, 0, j)),
        ],
        out_specs=pl.BlockSpec((1, BATCH, D_MODEL), lambda l, j: (l, 0, j)),
        out_shape=jax.ShapeDtypeStruct((DEPTH, BATCH, 3 * D_MODEL), F32),
        name="adaln_mod",
    )(c, ada_w, ada_b.reshape(DEPTH, 1, 3 * D_MODEL))


def _rms(x, g):
    ms = jnp.mean(x * x, axis=-1, keepdims=True)
    return x * lax.rsqrt(ms + EPS) * g


def _proj_kernel(*refs, n_z, final, n_out, q_cols):
    refs = iter(refs)
    x_ref = next(refs)
    if n_z:
        gate_ref = next(refs)
        z_refs = [next(refs) for _ in range(n_z)]
        wo_ref = next(refs)
    if final:
        fg_ref = next(refs)
    if n_out:
        g_ref, sc_ref, sh_ref, wi_ref = (next(refs) for _ in range(4))
    if n_z:
        xo_ref = next(refs)
    if n_out:
        proj_ref, h_ref = next(refs), next(refs)

    xn = x_ref[0]
    if n_z:
        kz = D_MODEL // n_z
        acc = jnp.dot(z_refs[0][0], wo_ref[0:kz, :], preferred_element_type=F32)
        for i in range(1, n_z):
            acc = acc + jnp.dot(z_refs[i][0], wo_ref[i * kz:(i + 1) * kz, :],
                                preferred_element_type=F32)
        xn = xn + gate_ref[0] * acc
        xo_ref[0] = _rms(xn, fg_ref[...]) if final else xn
    if n_out:
        h_ref[...] = (_rms(xn, g_ref[...]) * (1.0 + sc_ref[0]) + sh_ref[0]).astype(BF16)
        for j in range(n_out // COL_TILE):
            lo, hi = j * COL_TILE, (j + 1) * COL_TILE
            acc = jnp.dot(h_ref[...], wi_ref[:, lo:hi], preferred_element_type=F32)
            if any(q0 <= lo and hi <= q1 for q0, q1 in q_cols):
                acc = acc * Q_FOLD
            proj_ref[0, :, lo:hi] = acc.astype(BF16)


def _proj(x, out_stage=None, final_g=None, in_stage=None):
    row = pl.BlockSpec((1, ROW_TILE, D_MODEL), lambda b, i: (b, i, 0))
    per_batch = pl.BlockSpec((1, 1, D_MODEL), lambda b, i: (b, 0, 0))
    vec = pl.BlockSpec((1, D_MODEL), lambda b, i: (0, 0))
    whole = lambda shape: pl.BlockSpec(shape, lambda b, i: (0, 0))
    in_specs, args, out_specs, out_shape, scratch = [row], [x], [], [], []
    n_z = n_out = 0
    q_cols = ()
    assert final_g is None or out_stage is not None
    if out_stage is not None:
        gate, zs, w_out = out_stage
        n_z = len(zs)
        in_specs += [per_batch]
        in_specs += [pl.BlockSpec((1, ROW_TILE, D_MODEL // n_z), lambda b, i: (b, i, 0))] * n_z
        in_specs += [whole((D_MODEL, D_MODEL))]
        args += [gate.reshape(BATCH, 1, D_MODEL), *zs, w_out]
        out_specs += [row]
        out_shape += [jax.ShapeDtypeStruct((BATCH, SEQ, D_MODEL), F32)]
    if final_g is not None:
        in_specs += [vec]
        args += [final_g.reshape(1, D_MODEL)]
    if in_stage is not None:
        g, scale, shift, w_in, q_cols = in_stage
        n_out = w_in.shape[1]
        assert all(q0 % COL_TILE == 0 and q1 % COL_TILE == 0 for q0, q1 in q_cols)
        in_specs += [vec, per_batch, per_batch, whole((D_MODEL, n_out))]
        args += [g.reshape(1, D_MODEL), scale.reshape(BATCH, 1, D_MODEL),
                 shift.reshape(BATCH, 1, D_MODEL), w_in]
        out_specs += [pl.BlockSpec((1, ROW_TILE, n_out), lambda b, i: (b, i, 0))]
        out_shape += [jax.ShapeDtypeStruct((BATCH, SEQ, n_out), BF16)]
        scratch += [pltpu.VMEM((ROW_TILE, D_MODEL), BF16)]
    outs = pl.pallas_call(
        functools.partial(_proj_kernel, n_z=n_z, final=final_g is not None, n_out=n_out,
                          q_cols=q_cols),
        grid=(BATCH, SEQ // ROW_TILE),
        in_specs=in_specs,
        out_specs=out_specs,
        out_shape=out_shape,
        scratch_shapes=scratch,
        compiler_params=pltpu.CompilerParams(
            dimension_semantics=("arbitrary", "arbitrary"), vmem_limit_bytes=VMEM_LIMIT),
        name="proj",
    )(*args)
    outs = list(outs)
    x_new = outs.pop(0) if out_stage is not None else None
    proj = outs.pop(0) if in_stage is not None else None
    return x_new, proj


def _lane_lo(rows):
    return lax.broadcasted_iota(jnp.int32, (rows, LANES), 1) < HEAD_DIM


def _softmax_block(qm, kw, vw, bias):
    s = _dot_nt(qm, kw) + bias
    m = jnp.max(s, axis=-1, keepdims=True)
    p = jnp.exp2(s - m)
    l = jnp.sum(p, axis=-1, keepdims=True)
    acc = jnp.dot(p.astype(BF16), vw, preferred_element_type=F32)
    return acc, m, l


def _interleave(*stages):
    live = list(stages)
    while live:
        for g in list(live):
            try:
                next(g)
            except StopIteration:
                live.remove(g)


def _stack_heads(qb, lo):
    zero = jnp.zeros_like(qb)
    return jnp.concatenate([jnp.where(lo, qb, zero), jnp.where(lo, zero, qb)], axis=0)


A_SCRATCH = (
    [pltpu.VMEM((SEQ, LANES), F32)] * 6
    + [pltpu.VMEM((SEQ, LANES), BF16)] * 6
    + [pltpu.VMEM((3, SEQ, LANES), F32)] * 3
)


def _a_stages(q_ref, k_ref, v_ref, g_ref, b12_ref, b3_ref, o_ref,
              qf_ref, kf_ref, vf_ref, q4_ref, k4_ref, v4_ref, qd_ref, kd_ref, vd_ref,
              qe_ref, ke_ref, ve_ref, acc_ref, m_ref, l_ref):
    lo = _lane_lo(BLK)

    def pair_block(qb, kw, vw, bias):
        acc, m, l = _softmax_block(_stack_heads(qb, lo), kw, vw, bias)
        return (jnp.where(lo, acc[:BLK], acc[BLK:]), jnp.where(lo, m[:BLK], m[BLK:]),
                jnp.where(lo, l[:BLK], l[BLK:]))

    n_blk = SEQ // BLK
    dil1 = A_PATTERNS[1][1]
    seg1 = SEQ // dil1
    per1 = seg1 // BLK
    dil2 = A_PATTERNS[2][1]
    sub = dil2 // dil1
    assert sub * dil1 == dil2
    seg2 = SEQ // dil2
    streams = ((qf_ref, q4_ref, qd_ref, qe_ref), (kf_ref, k4_ref, kd_ref, ke_ref),
               (vf_ref, v4_ref, vd_ref, ve_ref))

    def prologue():
        chunk = 512
        for c in range(SEQ // chunk):
            rows = slice(c * chunk, (c + 1) * chunk)
            qf_ref[rows, :] = q_ref[0, rows, :].astype(F32)
            kf_ref[rows, :] = k_ref[0, rows, :].astype(F32)
            vf_ref[rows, :] = v_ref[0, rows, :].astype(F32)
            yield
        for r in range(dil1):
            dst = slice(r * seg1, (r + 1) * seg1)
            for src_ref, mid_ref, dst_ref, _ in streams:
                t = src_ref[pl.ds(r, seg1, stride=dil1), :]
                mid_ref[dst, :] = t
                dst_ref[dst, :] = t.astype(BF16)
            yield
        for r in range(dil1):
            for r2 in range(sub):
                res = dil1 * r2 + r
                dst = slice(res * seg2, (res + 1) * seg2)
                for _, mid_ref, _, dst_ref in streams:
                    dst_ref[dst, :] = (
                        mid_ref[pl.ds(r * seg1 + r2, seg2, stride=sub), :].astype(BF16))
            yield

    def aligned(x, m):
        return x if isinstance(x, int) else pl.multiple_of(x, m)

    def clamp(x, hi):
        return min(max(x, 0), hi) if isinstance(x, int) else jnp.clip(x, 0, hi)

    def variant(blk, last):
        if isinstance(blk, int):
            return 0 if blk == 0 else (2 if blk == last else 1)
        return jnp.where(blk == 0, 0, jnp.where(blk == last, 2, 1))

    def p0_body(j):
        r0 = aligned(j * BLK, BLK)
        ws = aligned(clamp(j * BLK - A_HALF, SEQ - A_KEYS), A_HALF)
        var = variant(j, n_blk - 1)
        acc, m, l = pair_block(q_ref[0, pl.ds(r0, BLK), :],
                               k_ref[0, pl.ds(ws, A_KEYS), :], v_ref[0, pl.ds(ws, A_KEYS), :],
                               b12_ref[0, 0, var])
        acc_ref[0, pl.ds(r0, BLK), :] = acc
        m_ref[0, pl.ds(r0, BLK), :] = m
        l_ref[0, pl.ds(r0, BLK), :] = l

    def p1_body(j):
        r = j // per1
        blk = j % per1
        r0 = aligned(j * BLK, BLK)
        ws = aligned(r * seg1 + clamp(blk * BLK - A_HALF, seg1 - A_KEYS), A_HALF)
        var = variant(blk, per1 - 1)
        acc, m, l = pair_block(qd_ref[pl.ds(r0, BLK), :],
                               kd_ref[pl.ds(ws, A_KEYS), :], vd_ref[pl.ds(ws, A_KEYS), :],
                               b12_ref[0, 1, var])
        dst = pl.ds(blk * BLK * dil1 + r, BLK, stride=dil1)
        acc_ref[1, dst, :] = acc
        m_ref[1, dst, :] = m
        l_ref[1, dst, :] = l

    def p2_body(j):
        r0 = aligned(j * BLK, BLK)
        acc, m, l = pair_block(qe_ref[pl.ds(r0, BLK), :],
                               ke_ref[pl.ds(r0, BLK), :], ve_ref[pl.ds(r0, BLK), :],
                               b3_ref[0])
        dst = pl.ds(j, BLK, stride=dil2)
        acc_ref[2, dst, :] = acc
        m_ref[2, dst, :] = m
        l_ref[2, dst, :] = l

    bodies = (p0_body, p1_body, p2_body)

    def blocks(work):
        for pattern, j in work:
            bodies[pattern](j)
            yield

    def merge(c):
        rows = pl.ds(aligned(c * A_MERGE_ROWS, A_MERGE_ROWS), A_MERGE_ROWS)
        m0, m1, m2 = m_ref[0, rows, :], m_ref[1, rows, :], m_ref[2, rows, :]
        mx = jnp.maximum(jnp.maximum(m0, m1), m2)
        w0, w1, w2 = jnp.exp2(m0 - mx), jnp.exp2(m1 - mx), jnp.exp2(m2 - mx)
        num = w0 * acc_ref[0, rows, :] + w1 * acc_ref[1, rows, :] + w2 * acc_ref[2, rows, :]
        den = w0 * l_ref[0, rows, :] + w1 * l_ref[1, rows, :] + w2 * l_ref[2, rows, :]
        y = num / den
        o_ref[0, rows, :] = (y * _silu(g_ref[0, rows, :].astype(F32))).astype(BF16)

    return prologue, blocks, merge


B_SCRATCH = (
    [pltpu.VMEM((B_NB, LANES + ONES_ROWS, SEQ), BF16)]
    + [pltpu.VMEM((SEQ, 2 * B_TQ), F32)] * 2
    + [pltpu.VMEM((1, 2 * B_TQ), F32)] * 2
    + [pltpu.VMEM((LANES + ONES_ROWS, 2 * B_TQ), F32)] * 2
)


def _b_stages(q_ref, k_ref, v_ref, g_ref, u_ref, lq1_ref, lk1_ref, lq2_ref, lk2_ref,
              sg_ref, o_ref, vt_ref, sa_ref, sb_ref, ma_ref, mb_ref, acca_ref, accb_ref,
              *, lam_init):
    lo = _lane_lo(B_TQ)
    n_t = SEQ // B_TK
    lam = (jnp.exp(jnp.sum(lq1_ref[...] * lk1_ref[...], axis=-1, keepdims=True))
           - jnp.exp(jnp.sum(lq2_ref[...] * lk2_ref[...], axis=-1, keepdims=True)) + lam_init)

    eye = jnp.where(lax.broadcasted_iota(jnp.int32, (LANES, LANES), 0)
                    == lax.broadcasted_iota(jnp.int32, (LANES, LANES), 1), 1.0, 0.0).astype(BF16)
    ones_row = jnp.where(lax.broadcasted_iota(jnp.int32, (ONES_ROWS, B_TK), 0) == 0,
                         1.0, 0.0).astype(BF16)
    for bi in range(B_NB):
        for t in range(n_t):
            rows = slice(t * B_TK, (t + 1) * B_TK)
            vt_ref[bi, :LANES, rows] = _dot_nt(eye, v_ref[bi, rows, :]).astype(BF16)
            vt_ref[bi, LANES:, rows] = ones_row

    n_q = SEQ // B_TQ

    def locate(i):
        if isinstance(i, int):
            return i // n_q, (i % n_q) * B_TQ
        return i // n_q, pl.multiple_of((i % n_q) * B_TQ, B_TQ)

    def scores(i, s_ref, m_ref):
        bi, r0 = locate(i)
        qst = _stack_heads(q_ref[bi, pl.ds(r0, B_TQ), :], lo)
        m = None
        for t in range(n_t):
            rows = slice(t * B_TK, (t + 1) * B_TK)
            off = pl.multiple_of(t * B_TK + (SEQ - B_TQ) - r0, LANES)
            ub = u_ref[0, pl.ds(off, B_TK), :]
            st = _dot_nt(k_ref[bi, rows, :], qst)
            s1 = st[:, :B_TQ] + ub
            s2 = st[:, B_TQ:] + ub
            s_ref[rows, :B_TQ] = s1
            s_ref[rows, B_TQ:] = s2
            f = jnp.concatenate([jnp.max(s1, axis=0, keepdims=True),
                                 jnp.max(s2, axis=0, keepdims=True)], axis=1)
            m = f if m is None else jnp.maximum(m, f)
            yield
        m_ref[...] = m

    def softmax_pv(i, s_ref, m_ref, acc_ref):
        bi, _ = locate(i)
        m = m_ref[...]
        acct = None
        for t in range(n_t):
            rows = slice(t * B_TK, (t + 1) * B_TK)
            p = jnp.exp2(s_ref[rows, :] - m).astype(BF16)
            d = jnp.dot(vt_ref[bi, :, rows], p, preferred_element_type=F32)
            acct = d if acct is None else acct + d
            yield
        acc_ref[...] = acct

    def epilogue(i, acc_ref):
        bi, r0 = locate(i)
        ot = acc_ref[:LANES, :] * (1.0 / acc_ref[LANES:LANES + 1, :])
        outt = ot[:, :B_TQ] - lam * ot[:, B_TQ:]
        ms = jnp.mean(outt * outt, axis=0, keepdims=True)
        yt = outt * lax.rsqrt(ms + EPS) * sg_ref[...] * (1.0 - lam_init)
        yield
        gate = _silu(g_ref[bi, pl.ds(r0, B_TQ), :].astype(F32))
        o_ref[bi, pl.ds(r0, B_TQ), :] = (yt.T * gate).astype(BF16)

    return scores, softmax_pv, epilogue, (sa_ref, ma_ref, acca_ref), (sb_ref, mb_ref, accb_ref)


def _b_kernel(*refs, lam_init):
    scores, softmax_pv, epilogue, (sa, ma, acca), (sb, mb, accb) = _b_stages(
        *refs, lam_init=lam_init)
    n = B_NB * (SEQ // B_TQ)
    _interleave(scores(0, sa, ma))
    _interleave(scores(1, sb, mb), softmax_pv(0, sa, ma, acca))

    def body(j, carry):
        i = 2 * j
        _interleave(scores(i + 2, sa, ma), softmax_pv(i + 1, sb, mb, accb), epilogue(i, acca))
        _interleave(scores(i + 3, sb, mb), softmax_pv(i + 2, sa, ma, acca), epilogue(i + 1, accb))
        return carry

    lax.fori_loop(0, n // 2 - 1, body, 0)
    _interleave(softmax_pv(n - 1, sb, mb, accb), epilogue(n - 2, acca))
    _interleave(epilogue(n - 1, accb))


def _a_kernel(*refs):
    prologue, blocks, merge = _a_stages(*refs)
    _interleave(prologue())
    _interleave(blocks([(p, j) for p in range(len(A_PATTERNS)) for j in range(SEQ // BLK)]))

    def merge_body(c, carry):
        merge(c)
        return carry

    lax.fori_loop(0, SEQ // A_MERGE_ROWS, merge_body, 0)


def _mixer_ab(proj, bias12, bias3, u, lq1, lk1, lq2, lk2, subln_g, lam_init):
    pairs = A_HEADS // 2
    a_w = A_HEADS * HEAD_DIM // LANES
    b_q = 3 * a_w
    g0 = (3 * A_HEADS * HEAD_DIM + 3 * B_HEADS * 2 * HEAD_DIM) // LANES
    blk = lambda off: pl.BlockSpec((1, SEQ, LANES), lambda i, b: (b, 0, off + i))
    vec = lambda n: pl.BlockSpec((1, n), lambda i, b: (0, 0))
    out = pl.BlockSpec((1, SEQ, LANES), lambda i, b: (b, 0, i))
    params = pltpu.CompilerParams(
        dimension_semantics=("arbitrary", "arbitrary"), vmem_limit_bytes=VMEM_LIMIT)
    za = pl.pallas_call(
        _a_kernel,
        grid=(pairs, BATCH),
        in_specs=[
            blk(0), blk(a_w), blk(2 * a_w), blk(g0),
            pl.BlockSpec((1, 2, 3, 2 * BLK, A_KEYS), lambda i, b: (i, 0, 0, 0, 0)),
            pl.BlockSpec((1, 2 * BLK, BLK), lambda i, b: (i, 0, 0)),
        ],
        out_specs=out,
        out_shape=jax.ShapeDtypeStruct((BATCH, SEQ, A_HEADS * HEAD_DIM), BF16),
        scratch_shapes=A_SCRATCH,
        compiler_params=params,
        name="mixer_a_dilated",
    )(proj, proj, proj, proj, bias12, bias3)
    blk = lambda off: pl.BlockSpec((B_NB, SEQ, LANES), lambda i, b: (b, 0, off + i))
    zb = pl.pallas_call(
        functools.partial(_b_kernel, lam_init=lam_init),
        grid=(B_HEADS, BATCH // B_NB),
        in_specs=[
            blk(b_q), blk(b_q + B_HEADS), blk(b_q + 2 * B_HEADS), blk(g0 + a_w),
            pl.BlockSpec((1, 2 * SEQ - B_TQ, B_TQ), lambda i, b: (i, 0, 0)),
            vec(HEAD_DIM), vec(HEAD_DIM), vec(HEAD_DIM), vec(HEAD_DIM),
            pl.BlockSpec((2 * HEAD_DIM, 1), lambda i, b: (0, 0)),
        ],
        out_specs=blk(0),
        out_shape=jax.ShapeDtypeStruct((BATCH, SEQ, B_HEADS * 2 * HEAD_DIM), BF16),
        scratch_shapes=B_SCRATCH,
        compiler_params=params,
        name="mixer_b_differential",
    )(proj, proj, proj, proj, u, lq1.reshape(1, -1), lk1.reshape(1, -1),
      lq2.reshape(1, -1), lk2.reshape(1, -1), subln_g.reshape(-1, 1))
    return za, zb


def _c_kernel(sink_ref, q_ref, k_ref, v_ref, g_ref, bias_ref, o_ref, ka_ref, vt_ref, *bufs):
    s_refs, m_refs = bufs[:2 * C_GROUP], bufs[2 * C_GROUP:]
    grp = pl.program_id(0)
    half = grp % 2
    lo = _lane_lo(BLK)
    zero = jnp.zeros((BLK, LANES), BF16)
    row = lax.broadcasted_iota(jnp.int32, (LANES, LANES), 0)
    col = lax.broadcasted_iota(jnp.int32, (LANES, LANES), 1)

    chunk = 512
    ones_row = jnp.where(lax.broadcasted_iota(jnp.int32, (ONES_ROWS, chunk), 0) == 0,
                         1.0, 0.0).astype(BF16)
    for a in range(2):
        shift = jnp.where(half == a, 0, HEAD_DIM)
        perm = jnp.where(col == (row + shift) % LANES, 1.0, 0.0).astype(BF16)
        for bi in range(C_NB):
            for c in range(SEQ // chunk):
                rows = slice(c * chunk, (c + 1) * chunk)
                ka_ref[bi, a, rows, :] = jnp.dot(k_ref[bi, rows, :], perm,
                                                 preferred_element_type=F32).astype(BF16)
                vt_ref[bi, a, :LANES, rows] = _dot_nt(perm, v_ref[bi, rows, :]).astype(BF16)
                vt_ref[bi, a, LANES:, rows] = ones_row

    n_blk = SEQ // BLK
    rep = C_HEADS // C_KV_HEADS
    left = lax.broadcasted_iota(jnp.int32, (1, 2 * BLK), 1) < BLK
    upper = lax.broadcasted_iota(jnp.int32, (LANES, BLK), 0) < HEAD_DIM

    def locate(n):
        bi, nl = n // n_blk, n % n_blk
        r0 = pl.multiple_of(nl * BLK, BLK)
        ws = pl.multiple_of(jnp.clip(nl * BLK - C_HALF_WINDOW, 0, SEQ - C_KEYS), BLK)
        return bi, nl, r0, ws

    def scores(n, s_ref, m_ref):
        bi, nl, r0, ws = locate(n)
        var = jnp.where(nl == 0, 0, jnp.where(nl == n_blk - 1, 2, 1))
        for a in range(2):
            qa = [q_ref[bi, pl.ds(r0, BLK), c * LANES:(c + 1) * LANES] for c in range(rep // 2)]
            qm = jnp.concatenate(
                [jnp.where(lo, q, zero) if a == 0 else jnp.where(lo, zero, q) for q in qa], axis=0)
            st = _dot_nt(ka_ref[bi, a, pl.ds(ws, C_KEYS), :], qm) + bias_ref[0, a, var]
            s_ref[a] = st
            m_ref[a] = jnp.max(st, axis=0, keepdims=True)
            yield

    def finish(n, s_ref, m_ref):
        bi, _, r0, ws = locate(n)
        yts = []
        for a in range(2):
            m = m_ref[a]
            p = jnp.exp2(s_ref[a] - m).astype(BF16)
            acct = jnp.dot(vt_ref[bi, a, :, pl.ds(ws, C_KEYS)], p, preferred_element_type=F32)
            l = acct[LANES:LANES + 1]
            sk = jnp.where(left, sink_ref[grp * rep + a], sink_ref[grp * rep + a + 2]) * LOG2E
            mx = jnp.maximum(m, sk)
            e = jnp.exp2(m - mx)
            yts.append(acct[:LANES] * (e / (l * e + jnp.exp2(sk - mx))))
            yield
        for c in range(rep // 2):
            cols = slice(c * LANES, (c + 1) * LANES)
            y = jnp.where(upper, yts[0][:, cols], yts[1][:, cols]).T
            gate = _silu(g_ref[bi, pl.ds(r0, BLK), cols].astype(F32))
            o_ref[bi, pl.ds(r0, BLK), cols] = (y * gate).astype(BF16)

    bufs = [(s_refs[i], m_refs[i]) for i in range(2 * C_GROUP)]
    set_a, set_b = bufs[:C_GROUP], bufs[C_GROUP:]
    n_grp = C_NB * n_blk // C_GROUP

    def scores_of(g, bset):
        return [scores(g * C_GROUP + i, *bset[i]) for i in range(C_GROUP)]

    def finish_of(g, bset):
        return [finish(g * C_GROUP + i, *bset[i]) for i in range(C_GROUP)]

    _interleave(*scores_of(0, set_a))

    def body(j, carry):
        _interleave(*finish_of(2 * j, set_a), *scores_of(2 * j + 1, set_b))
        _interleave(*finish_of(2 * j + 1, set_b), *scores_of(2 * j + 2, set_a))
        return carry

    lax.fori_loop(0, n_grp // 2 - 1, body, 0, unroll=True)
    _interleave(*finish_of(n_grp - 2, set_a), *scores_of(n_grp - 1, set_b))
    _interleave(*finish_of(n_grp - 1, set_b))


def _mixer_c(proj, sink, bias):
    width = C_HEADS // C_KV_HEADS * HEAD_DIM
    kb = C_HEADS * HEAD_DIM // LANES
    vb = kb + C_KV_HEADS * HEAD_DIM // LANES
    gb = (C_HEADS + 2 * C_KV_HEADS) * HEAD_DIM // width
    return pl.pallas_call(
        _c_kernel,
        grid=(C_KV_HEADS, BATCH // C_NB),
        in_specs=[
            pl.BlockSpec(memory_space=pltpu.SMEM),
            pl.BlockSpec((C_NB, SEQ, width), lambda g, b: (b, 0, g)),
            pl.BlockSpec((C_NB, SEQ, LANES), lambda g, b: (b, 0, kb + g // 2)),
            pl.BlockSpec((C_NB, SEQ, LANES), lambda g, b: (b, 0, vb + g // 2)),
            pl.BlockSpec((C_NB, SEQ, width), lambda g, b: (b, 0, gb + g)),
            pl.BlockSpec((1, 2, 3, C_KEYS, 2 * BLK), lambda g, b: (g, 0, 0, 0, 0)),
        ],
        out_specs=pl.BlockSpec((C_NB, SEQ, width), lambda g, b: (b, 0, g)),
        out_shape=jax.ShapeDtypeStruct((BATCH, SEQ, C_HEADS * HEAD_DIM), BF16),
        scratch_shapes=[pltpu.VMEM((C_NB, 2, SEQ, LANES), BF16),
                        pltpu.VMEM((C_NB, 2, LANES + ONES_ROWS, SEQ), BF16)]
        + [pltpu.VMEM((2, C_KEYS, 2 * BLK), F32)] * (2 * C_GROUP)
        + [pltpu.VMEM((2, 1, 2 * BLK), F32)] * (2 * C_GROUP),
        compiler_params=pltpu.CompilerParams(
            dimension_semantics=("arbitrary", "arbitrary"), vmem_limit_bytes=VMEM_LIMIT),
        name="mixer_c_windowed",
    )(sink, proj, proj, proj, proj, bias)


@functools.lru_cache(maxsize=None)
def _bias_tables():
    def stack_pairs(t, first, second):
        return np.concatenate([t[first], t[second]], axis=2)

    sa = _alibi_slopes(A_HEADS)
    offs = (0, A_HALF, 2 * A_HALF)
    ev, od = slice(0, None, 2), slice(1, None, 2)
    b12 = np.stack(
        [stack_pairs(_band_bias(sa, dil, A_HALF, BLK, A_KEYS, offs), ev, od)
         for _, dil in A_PATTERNS[:2]], axis=1)
    b3 = stack_pairs(_band_bias(sa, A_PATTERNS[2][1], A_HALF, BLK, BLK, (0,)), ev, od)[:, 0]
    sc = _alibi_slopes(C_HEADS)
    bc_heads = _band_bias(sc, 1, C_HALF_WINDOW, BLK, C_KEYS,
                          (0, C_HALF_WINDOW, 2 * C_HALF_WINDOW))
    rep = C_HEADS // C_KV_HEADS
    bc = np.stack([stack_pairs(bc_heads, slice(a, None, rep), slice(a + 2, None, rep))
                   for a in range(2)], axis=1)
    bc = np.ascontiguousarray(np.swapaxes(bc, -1, -2))
    sb = _alibi_slopes(B_HEADS)
    il = np.arange(B_TQ)[None, :]
    uu = np.arange(2 * SEQ - B_TQ)[:, None]
    dist = np.abs(il - uu + (SEQ - B_TQ)).astype(np.float32)
    ub = -sb[:, None, None] * dist[None] * np.float32(LOG2E)
    return b12, b3, bc, ub.astype(np.float32)


def kernel(x, c, ada_w, ada_b, norm_g, ab_w_in, ab_w_out, diff_lq1, diff_lk1, diff_lq2,
           diff_lk2, diff_subln_g, c_w_in, c_w_out, c_sink, final_g):
    b12, b3, bc, ub = (jnp.asarray(t) for t in _bias_tables())
    mod = _modulation(c, ada_w, ada_b)
    a_w = A_HEADS * HEAD_DIM
    q_even = ((0, a_w), (3 * a_w, 3 * a_w + B_HEADS * 2 * HEAD_DIM))
    q_odd = ((0, C_HEADS * HEAD_DIM),)

    def in_stage(layer):
        shift = mod[layer, :, :D_MODEL]
        scale = mod[layer, :, D_MODEL:2 * D_MODEL]
        w, q_cols = (ab_w_in, q_even) if layer % 2 == 0 else (c_w_in, q_odd)
        return norm_g[layer], scale, shift, w[layer // 2].astype(BF16), q_cols

    _, proj = _proj(x, in_stage=in_stage(0))
    for layer in range(DEPTH):
        gate = mod[layer, :, 2 * D_MODEL:]
        j = layer // 2
        last = layer == DEPTH - 1
        if layer % 2 == 0:
            lam_init = 0.8 - 0.6 * math.exp(-0.3 * layer)
            za, zb = _mixer_ab(proj, b12, b3, ub, diff_lq1[j], diff_lk1[j], diff_lq2[j],
                               diff_lk2[j], diff_subln_g[j], lam_init)
            zs, w_out = [za, zb], ab_w_out[j]
        else:
            zs, w_out = [_mixer_c(proj, c_sink[j], bc)], c_w_out[j]
        x, proj = _proj(x, out_stage=(gate, zs, w_out.astype(BF16)),
                        final_g=final_g if last else None,
                        in_stage=None if last else in_stage(layer + 1))
    return x
```

```python
import functools
import math

import numpy as np
import jax
import jax.numpy as jnp
from jax import lax
from jax.experimental import pallas as pl
from jax.experimental.pallas import tpu as pltpu

D_MODEL = 1024
BATCH = 8
SEQ = 2048
DEPTH = 4
HEAD_DIM = 64
LANES = 128
A_HEADS = 8
A_PATTERNS = ((128, 1), (512, 4), (2048, 16))
A_HALF = 64
B_HEADS = 4
C_HEADS = 16
C_KV_HEADS = 4
C_HALF_WINDOW = 128
EVEN_IN = 4096
ODD_IN = 2560
EPS = 1e-6
NEG_INF = -1e30
LOG2E = math.log2(math.e)
Q_FOLD = HEAD_DIM ** -0.5 * LOG2E
ONES_ROWS = 16

F32 = jnp.float32
BF16 = jnp.bfloat16

ROW_TILE = 512
COL_TILE = 512
W_CAST_ROWS = 128
BLK = 128
A_KEYS = BLK + 2 * A_HALF
C_KEYS = BLK + 2 * C_HALF_WINDOW
A_MERGE_ROWS = 256
C_GROUP = 2
C_NB = 1
B_TQ = 256
B_TK = 256
B_NB = 2
VMEM_LIMIT = 56 * 1024 * 1024


def _silu(t):
    return t * (1.0 / (1.0 + jnp.exp(-t)))


def _dot_nt(a, b):
    return lax.dot_general(a, b, (((1,), (1,)), ((), ())), preferred_element_type=F32)


def _alibi_slopes(n):
    return (2.0 ** (-8.0 * np.arange(1, n + 1, dtype=np.float32) / n)).astype(np.float32)


def _band_bias(slopes, spacing, half, tq, tk, offsets):
    i = np.arange(tq)[:, None]
    c = np.arange(tk)[None, :]
    out = np.empty((len(slopes), len(offsets), tq, tk), np.float32)
    for v, off in enumerate(offsets):
        rel = np.abs(c - (i + off))
        dist = (rel * spacing).astype(np.float32)
        for h, m in enumerate(slopes):
            out[h, v] = np.where(rel <= half, -m * dist * np.float32(LOG2E), np.float32(NEG_INF))
    return out


def _split_bf16(a):
    hi = a.astype(BF16)
    return hi, (a - hi.astype(F32)).astype(BF16)


def _mod_kernel(c_ref, w_ref, b_ref, o_ref):
    c_hi, c_lo = _split_bf16(_silu(c_ref[...]))
    w_hi, w_lo = _split_bf16(w_ref[0])
    dot = functools.partial(jnp.dot, preferred_element_type=F32)
    o_ref[0] = dot(c_hi, w_hi) + (dot(c_hi, w_lo) + dot(c_lo, w_hi)) + b_ref[0]


def _modulation(c, ada_w, ada_b):
    nblk = 3 * D_MODEL // D_MODEL
    return pl.pallas_call(
        _mod_kernel,
        grid=(DEPTH, nblk),
        in_specs=[
            pl.BlockSpec((BATCH, D_MODEL), lambda l, j: (0, 0)),
            pl.BlockSpec((1, D_MODEL, D_MODEL), lambda l, j: (l, 0, j)),
            pl.BlockSpec((1, 1, D_MODEL), lambda l, j: (l, 0, j)),
        ],
        out_specs=pl.BlockSpec((1, BATCH, D_MODEL), lambda l, j: (l, 0, j)),
        out_shape=jax.ShapeDtypeStruct((DEPTH, BATCH, 3 * D_MODEL), F32),
        name="adaln_mod",
    )(c, ada_w, ada_b.reshape(DEPTH, 1, 3 * D_MODEL))


def _rms(x, g):
    ms = jnp.mean(x * x, axis=-1, keepdims=True)
    return x * lax.rsqrt(ms + EPS) * g


def _proj_kernel(*refs, n_z, final, n_out, q_cols):
    refs = iter(refs)
    x_ref = next(refs)
    if n_z:
        gate_ref = next(refs)
        z_refs = [next(refs) for _ in range(n_z)]
        wo_ref = next(refs)
    if final:
        fg_ref = next(refs)
    if n_out:
        g_ref, sc_ref, sh_ref, wi_ref = (next(refs) for _ in range(4))
    if n_z:
        xo_ref = next(refs)
    if n_out:
        proj_ref = next(refs)
    if n_z:
        wob_ref = next(refs)
    if n_out:
        wib_ref, h_ref = next(refs), next(refs)

    @pl.when((pl.program_id(0) == 0) & (pl.program_id(1) == 0))
    def _cast_weights():
        for src_ref, dst_ref in ([(wo_ref, wob_ref)] if n_z else []) + (
                [(wi_ref, wib_ref)] if n_out else []):
            for r in range(0, D_MODEL, W_CAST_ROWS):
                dst_ref[r:r + W_CAST_ROWS, :] = src_ref[r:r + W_CAST_ROWS, :].astype(BF16)

    xn = x_ref[0]
    if n_z:
        kz = D_MODEL // n_z
        acc = jnp.dot(z_refs[0][0], wob_ref[0:kz, :], preferred_element_type=F32)
        for i in range(1, n_z):
            acc = acc + jnp.dot(z_refs[i][0], wob_ref[i * kz:(i + 1) * kz, :],
                                preferred_element_type=F32)
        xn = xn + gate_ref[0] * acc
        xo_ref[0] = _rms(xn, fg_ref[...]) if final else xn
    if n_out:
        h_ref[...] = (_rms(xn, g_ref[...]) * (1.0 + sc_ref[0]) + sh_ref[0]).astype(BF16)
        for j in range(n_out // COL_TILE):
            lo, hi = j * COL_TILE, (j + 1) * COL_TILE
            acc = jnp.dot(h_ref[...], wib_ref[:, lo:hi], preferred_element_type=F32)
            if any(q0 <= lo and hi <= q1 for q0, q1 in q_cols):
                acc = acc * Q_FOLD
            proj_ref[0, :, lo:hi] = acc.astype(BF16)


def _proj(x, out_stage=None, final_g=None, in_stage=None):
    row = pl.BlockSpec((1, ROW_TILE, D_MODEL), lambda b, i: (b, i, 0))
    per_batch = pl.BlockSpec((1, 1, D_MODEL), lambda b, i: (b, 0, 0))
    vec = pl.BlockSpec((1, D_MODEL), lambda b, i: (0, 0))
    def weight(stack, j):
        return pl.BlockSpec((None,) + stack.shape[1:], lambda b, i: (j, 0, 0),
                            pipeline_mode=pl.Buffered(1))
    in_specs, args, out_specs, out_shape, scratch = [row], [x], [], [], []
    n_z = n_out = 0
    q_cols = ()
    assert final_g is None or out_stage is not None
    if out_stage is not None:
        gate, zs, w_out, j_out = out_stage
        n_z = len(zs)
        in_specs += [per_batch]
        in_specs += [pl.BlockSpec((1, ROW_TILE, D_MODEL // n_z), lambda b, i: (b, i, 0))] * n_z
        in_specs += [weight(w_out, j_out)]
        args += [gate.reshape(BATCH, 1, D_MODEL), *zs, w_out]
        out_specs += [row]
        out_shape += [jax.ShapeDtypeStruct((BATCH, SEQ, D_MODEL), F32)]
        scratch += [pltpu.VMEM((D_MODEL, D_MODEL), BF16)]
    if final_g is not None:
        in_specs += [vec]
        args += [final_g.reshape(1, D_MODEL)]
    if in_stage is not None:
        g, scale, shift, w_in, j_in, q_cols = in_stage
        n_out = w_in.shape[2]
        assert all(q0 % COL_TILE == 0 and q1 % COL_TILE == 0 for q0, q1 in q_cols)
        in_specs += [vec, per_batch, per_batch, weight(w_in, j_in)]
        args += [g.reshape(1, D_MODEL), scale.reshape(BATCH, 1, D_MODEL),
                 shift.reshape(BATCH, 1, D_MODEL), w_in]
        out_specs += [pl.BlockSpec((1, ROW_TILE, n_out), lambda b, i: (b, i, 0))]
        out_shape += [jax.ShapeDtypeStruct((BATCH, SEQ, n_out), BF16)]
        scratch += [pltpu.VMEM((D_MODEL, n_out), BF16), pltpu.VMEM((ROW_TILE, D_MODEL), BF16)]
    outs = pl.pallas_call(
        functools.partial(_proj_kernel, n_z=n_z, final=final_g is not None, n_out=n_out,
                          q_cols=q_cols),
        grid=(BATCH, SEQ // ROW_TILE),
        in_specs=in_specs,
        out_specs=out_specs,
        out_shape=out_shape,
        scratch_shapes=scratch,
        compiler_params=pltpu.CompilerParams(
            dimension_semantics=("arbitrary", "arbitrary"), vmem_limit_bytes=VMEM_LIMIT),
        name="proj",
    )(*args)
    outs = list(outs)
    x_new = outs.pop(0) if out_stage is not None else None
    proj = outs.pop(0) if in_stage is not None else None
    return x_new, proj


def _lane_lo(rows):
    return lax.broadcasted_iota(jnp.int32, (rows, LANES), 1) < HEAD_DIM


def _softmax_block(qm, kw, vw, bias):
    s = _dot_nt(qm, kw) + bias
    m = jnp.max(s, axis=-1, keepdims=True)
    p = jnp.exp2(s - m)
    l = jnp.sum(p, axis=-1, keepdims=True)
    acc = jnp.dot(p.astype(BF16), vw, preferred_element_type=F32)
    return acc, m, l


def _interleave(*stages):
    live = list(stages)
    while live:
        for g in list(live):
            try:
                next(g)
            except StopIteration:
                live.remove(g)


def _stack_heads(qb, lo):
    zero = jnp.zeros_like(qb)
    return jnp.concatenate([jnp.where(lo, qb, zero), jnp.where(lo, zero, qb)], axis=0)


A_SCRATCH = (
    [pltpu.VMEM((SEQ, LANES), F32)] * 6
    + [pltpu.VMEM((SEQ, LANES), BF16)] * 6
    + [pltpu.VMEM((3, SEQ, LANES), F32)] * 3
)


def _a_stages(q_ref, k_ref, v_ref, g_ref, b12_ref, b3_ref, o_ref,
              qf_ref, kf_ref, vf_ref, q4_ref, k4_ref, v4_ref, qd_ref, kd_ref, vd_ref,
              qe_ref, ke_ref, ve_ref, acc_ref, m_ref, l_ref):
    lo = _lane_lo(BLK)

    def pair_block(qb, kw, vw, bias):
        acc, m, l = _softmax_block(_stack_heads(qb, lo), kw, vw, bias)
        return (jnp.where(lo, acc[:BLK], acc[BLK:]), jnp.where(lo, m[:BLK], m[BLK:]),
                jnp.where(lo, l[:BLK], l[BLK:]))

    n_blk = SEQ // BLK
    dil1 = A_PATTERNS[1][1]
    seg1 = SEQ // dil1
    per1 = seg1 // BLK
    dil2 = A_PATTERNS[2][1]
    sub = dil2 // dil1
    assert sub * dil1 == dil2
    seg2 = SEQ // dil2
    streams = ((qf_ref, q4_ref, qd_ref, qe_ref), (kf_ref, k4_ref, kd_ref, ke_ref),
               (vf_ref, v4_ref, vd_ref, ve_ref))

    def prologue():
        chunk = 512
        for c in range(SEQ // chunk):
            rows = slice(c * chunk, (c + 1) * chunk)
            qf_ref[rows, :] = q_ref[0, rows, :].astype(F32)
            kf_ref[rows, :] = k_ref[0, rows, :].astype(F32)
            vf_ref[rows, :] = v_ref[0, rows, :].astype(F32)
            yield
        for r in range(dil1):
            dst = slice(r * seg1, (r + 1) * seg1)
            for src_ref, mid_ref, dst_ref, _ in streams:
                t = src_ref[pl.ds(r, seg1, stride=dil1), :]
                mid_ref[dst, :] = t
                dst_ref[dst, :] = t.astype(BF16)
            yield
        for r in range(dil1):
            for r2 in range(sub):
                res = dil1 * r2 + r
                dst = slice(res * seg2, (res + 1) * seg2)
                for _, mid_ref, _, dst_ref in streams:
                    dst_ref[dst, :] = (
                        mid_ref[pl.ds(r * seg1 + r2, seg2, stride=sub), :].astype(BF16))
            yield

    def aligned(x, m):
        return x if isinstance(x, int) else pl.multiple_of(x, m)

    def clamp(x, hi):
        return min(max(x, 0), hi) if isinstance(x, int) else jnp.clip(x, 0, hi)

    def variant(blk, last):
        if isinstance(blk, int):
            return 0 if blk == 0 else (2 if blk == last else 1)
        return jnp.where(blk == 0, 0, jnp.where(blk == last, 2, 1))

    def p0_body(j):
        r0 = aligned(j * BLK, BLK)
        ws = aligned(clamp(j * BLK - A_HALF, SEQ - A_KEYS), A_HALF)
        var = variant(j, n_blk - 1)
        acc, m, l = pair_block(q_ref[0, pl.ds(r0, BLK), :],
                               k_ref[0, pl.ds(ws, A_KEYS), :], v_ref[0, pl.ds(ws, A_KEYS), :],
                               b12_ref[0, 0, var])
        acc_ref[0, pl.ds(r0, BLK), :] = acc
        m_ref[0, pl.ds(r0, BLK), :] = m
        l_ref[0, pl.ds(r0, BLK), :] = l

    def p1_body(j):
        r = j // per1
        blk = j % per1
        r0 = aligned(j * BLK, BLK)
        ws = aligned(r * seg1 + clamp(blk * BLK - A_HALF, seg1 - A_KEYS), A_HALF)
        var = variant(blk, per1 - 1)
        acc, m, l = pair_block(qd_ref[pl.ds(r0, BLK), :],
                               kd_ref[pl.ds(ws, A_KEYS), :], vd_ref[pl.ds(ws, A_KEYS), :],
                               b12_ref[0, 1, var])
        dst = pl.ds(blk * BLK * dil1 + r, BLK, stride=dil1)
        acc_ref[1, dst, :] = acc
        m_ref[1, dst, :] = m
        l_ref[1, dst, :] = l

    def p2_body(j):
        r0 = aligned(j * BLK, BLK)
        acc, m, l = pair_block(qe_ref[pl.ds(r0, BLK), :],
                               ke_ref[pl.ds(r0, BLK), :], ve_ref[pl.ds(r0, BLK), :],
                               b3_ref[0])
        dst = pl.ds(j, BLK, stride=dil2)
        acc_ref[2, dst, :] = acc
        m_ref[2, dst, :] = m
        l_ref[2, dst, :] = l

    bodies = (p0_body, p1_body, p2_body)

    def blocks(work):
        for pattern, j in work:
            bodies[pattern](j)
            yield

    def merge(c):
        rows = pl.ds(aligned(c * A_MERGE_ROWS, A_MERGE_ROWS), A_MERGE_ROWS)
        m0, m1, m2 = m_ref[0, rows, :], m_ref[1, rows, :], m_ref[2, rows, :]
        mx = jnp.maximum(jnp.maximum(m0, m1), m2)
        w0, w1, w2 = jnp.exp2(m0 - mx), jnp.exp2(m1 - mx), jnp.exp2(m2 - mx)
        num = w0 * acc_ref[0, rows, :] + w1 * acc_ref[1, rows, :] + w2 * acc_ref[2, rows, :]
        den = w0 * l_ref[0, rows, :] + w1 * l_ref[1, rows, :] + w2 * l_ref[2, rows, :]
        y = num / den
        o_ref[0, rows, :] = (y * _silu(g_ref[0, rows, :].astype(F32))).astype(BF16)

    return prologue, blocks, merge


B_SCRATCH = (
    [pltpu.VMEM((B_NB, LANES + ONES_ROWS, SEQ), BF16)]
    + [pltpu.VMEM((SEQ, 2 * B_TQ), F32)] * 2
    + [pltpu.VMEM((1, 2 * B_TQ), F32)] * 2
    + [pltpu.VMEM((LANES + ONES_ROWS, 2 * B_TQ), F32)] * 2
)


def _b_stages(q_ref, k_ref, v_ref, g_ref, u_ref, lq1_ref, lk1_ref, lq2_ref, lk2_ref,
              sg_ref, o_ref, vt_ref, sa_ref, sb_ref, ma_ref, mb_ref, acca_ref, accb_ref,
              *, lam_init):
    lo = _lane_lo(B_TQ)
    n_t = SEQ // B_TK
    lam = (jnp.exp(jnp.sum(lq1_ref[...] * lk1_ref[...], axis=-1, keepdims=True))
           - jnp.exp(jnp.sum(lq2_ref[...] * lk2_ref[...], axis=-1, keepdims=True)) + lam_init)

    eye = jnp.where(lax.broadcasted_iota(jnp.int32, (LANES, LANES), 0)
                    == lax.broadcasted_iota(jnp.int32, (LANES, LANES), 1), 1.0, 0.0).astype(BF16)
    ones_row = jnp.where(lax.broadcasted_iota(jnp.int32, (ONES_ROWS, B_TK), 0) == 0,
                         1.0, 0.0).astype(BF16)
    for bi in range(B_NB):
        for t in range(n_t):
            rows = slice(t * B_TK, (t + 1) * B_TK)
            vt_ref[bi, :LANES, rows] = _dot_nt(eye, v_ref[bi, rows, :]).astype(BF16)
            vt_ref[bi, LANES:, rows] = ones_row

    n_q = SEQ // B_TQ

    def locate(i):
        if isinstance(i, int):
            return i // n_q, (i % n_q) * B_TQ
        return i // n_q, pl.multiple_of((i % n_q) * B_TQ, B_TQ)

    def scores(i, s_ref, m_ref):
        bi, r0 = locate(i)
        qst = _stack_heads(q_ref[bi, pl.ds(r0, B_TQ), :], lo)
        m = None
        for t in range(n_t):
            rows = slice(t * B_TK, (t + 1) * B_TK)
            off = pl.multiple_of(t * B_TK + (SEQ - B_TQ) - r0, LANES)
            ub = u_ref[0, pl.ds(off, B_TK), :]
            st = _dot_nt(k_ref[bi, rows, :], qst)
            s1 = st[:, :B_TQ] + ub
            s2 = st[:, B_TQ:] + ub
            s_ref[rows, :B_TQ] = s1
            s_ref[rows, B_TQ:] = s2
            f = jnp.concatenate([jnp.max(s1, axis=0, keepdims=True),
                                 jnp.max(s2, axis=0, keepdims=True)], axis=1)
            m = f if m is None else jnp.maximum(m, f)
            yield
        m_ref[...] = m

    def softmax_pv(i, s_ref, m_ref, acc_ref):
        bi, _ = locate(i)
        m = m_ref[...]
        acct = None
        for t in range(n_t):
            rows = slice(t * B_TK, (t + 1) * B_TK)
            p = jnp.exp2(s_ref[rows, :] - m).astype(BF16)
            d = jnp.dot(vt_ref[bi, :, rows], p, preferred_element_type=F32)
            acct = d if acct is None else acct + d
            yield
        acc_ref[...] = acct

    def epilogue(i, acc_ref):
        bi, r0 = locate(i)
        ot = acc_ref[:LANES, :] * (1.0 / acc_ref[LANES:LANES + 1, :])
        outt = ot[:, :B_TQ] - lam * ot[:, B_TQ:]
        ms = jnp.mean(outt * outt, axis=0, keepdims=True)
        yt = outt * lax.rsqrt(ms + EPS) * sg_ref[...] * (1.0 - lam_init)
        yield
        gate = _silu(g_ref[bi, pl.ds(r0, B_TQ), :].astype(F32))
        o_ref[bi, pl.ds(r0, B_TQ), :] = (yt.T * gate).astype(BF16)

    return scores, softmax_pv, epilogue, (sa_ref, ma_ref, acca_ref), (sb_ref, mb_ref, accb_ref)


def _b_kernel(*refs, lam_init):
    scores, softmax_pv, epilogue, (sa, ma, acca), (sb, mb, accb) = _b_stages(
        *refs, lam_init=lam_init)
    n = B_NB * (SEQ // B_TQ)
    _interleave(scores(0, sa, ma))
    _interleave(scores(1, sb, mb), softmax_pv(0, sa, ma, acca))

    def body(j, carry):
        i = 2 * j
        _interleave(scores(i + 2, sa, ma), softmax_pv(i + 1, sb, mb, accb), epilogue(i, acca))
        _interleave(scores(i + 3, sb, mb), softmax_pv(i + 2, sa, ma, acca), epilogue(i + 1, accb))
        return carry

    lax.fori_loop(0, n // 2 - 1, body, 0)
    _interleave(softmax_pv(n - 1, sb, mb, accb), epilogue(n - 2, acca))
    _interleave(epilogue(n - 1, accb))


def _a_kernel(*refs):
    prologue, blocks, merge = _a_stages(*refs)
    _interleave(prologue())
    _interleave(blocks([(p, j) for p in range(len(A_PATTERNS)) for j in range(SEQ // BLK)]))

    def merge_body(c, carry):
        merge(c)
        return carry

    lax.fori_loop(0, SEQ // A_MERGE_ROWS, merge_body, 0)


def _mixer_ab(proj, bias12, bias3, u, lq1, lk1, lq2, lk2, subln_g, lam_init):
    pairs = A_HEADS // 2
    a_w = A_HEADS * HEAD_DIM // LANES
    b_q = 3 * a_w
    g0 = (3 * A_HEADS * HEAD_DIM + 3 * B_HEADS * 2 * HEAD_DIM) // LANES
    blk = lambda off: pl.BlockSpec((1, SEQ, LANES), lambda i, b: (b, 0, off + i))
    vec = lambda n: pl.BlockSpec((1, n), lambda i, b: (0, 0))
    out = pl.BlockSpec((1, SEQ, LANES), lambda i, b: (b, 0, i))
    params = pltpu.CompilerParams(
        dimension_semantics=("arbitrary", "arbitrary"), vmem_limit_bytes=VMEM_LIMIT)
    za = pl.pallas_call(
        _a_kernel,
        grid=(pairs, BATCH),
        in_specs=[
            blk(0), blk(a_w), blk(2 * a_w), blk(g0),
            pl.BlockSpec((1, 2, 3, 2 * BLK, A_KEYS), lambda i, b: (i, 0, 0, 0, 0)),
            pl.BlockSpec((1, 2 * BLK, BLK), lambda i, b: (i, 0, 0)),
        ],
        out_specs=out,
        out_shape=jax.ShapeDtypeStruct((BATCH, SEQ, A_HEADS * HEAD_DIM), BF16),
        scratch_shapes=A_SCRATCH,
        compiler_params=params,
        name="mixer_a_dilated",
    )(proj, proj, proj, proj, bias12, bias3)
    blk = lambda off: pl.BlockSpec((B_NB, SEQ, LANES), lambda i, b: (b, 0, off + i))
    zb = pl.pallas_call(
        functools.partial(_b_kernel, lam_init=lam_init),
        grid=(B_HEADS, BATCH // B_NB),
        in_specs=[
            blk(b_q), blk(b_q + B_HEADS), blk(b_q + 2 * B_HEADS), blk(g0 + a_w),
            pl.BlockSpec((1, 2 * SEQ - B_TQ, B_TQ), lambda i, b: (i, 0, 0)),
            vec(HEAD_DIM), vec(HEAD_DIM), vec(HEAD_DIM), vec(HEAD_DIM),
            pl.BlockSpec((2 * HEAD_DIM, 1), lambda i, b: (0, 0)),
        ],
        out_specs=blk(0),
        out_shape=jax.ShapeDtypeStruct((BATCH, SEQ, B_HEADS * 2 * HEAD_DIM), BF16),
        scratch_shapes=B_SCRATCH,
        compiler_params=params,
        name="mixer_b_differential",
    )(proj, proj, proj, proj, u, lq1.reshape(1, -1), lk1.reshape(1, -1),
      lq2.reshape(1, -1), lk2.reshape(1, -1), subln_g.reshape(-1, 1))
    return za, zb


def _c_kernel(sink_ref, q_ref, k_ref, v_ref, g_ref, bias_ref, o_ref, ka_ref, vt_ref, *bufs):
    s_refs, m_refs = bufs[:2 * C_GROUP], bufs[2 * C_GROUP:]
    grp = pl.program_id(0)
    half = grp % 2
    lo = _lane_lo(BLK)
    zero = jnp.zeros((BLK, LANES), BF16)
    row = lax.broadcasted_iota(jnp.int32, (LANES, LANES), 0)
    col = lax.broadcasted_iota(jnp.int32, (LANES, LANES), 1)

    chunk = 512
    ones_row = jnp.where(lax.broadcasted_iota(jnp.int32, (ONES_ROWS, chunk), 0) == 0,
                         1.0, 0.0).astype(BF16)
    for a in range(2):
        shift = jnp.where(half == a, 0, HEAD_DIM)
        perm = jnp.where(col == (row + shift) % LANES, 1.0, 0.0).astype(BF16)
        for bi in range(C_NB):
            for c in range(SEQ // chunk):
                rows = slice(c * chunk, (c + 1) * chunk)
                ka_ref[bi, a, rows, :] = jnp.dot(k_ref[bi, rows, :], perm,
                                                 preferred_element_type=F32).astype(BF16)
                vt_ref[bi, a, :LANES, rows] = _dot_nt(perm, v_ref[bi, rows, :]).astype(BF16)
                vt_ref[bi, a, LANES:, rows] = ones_row

    n_blk = SEQ // BLK
    rep = C_HEADS // C_KV_HEADS
    left = lax.broadcasted_iota(jnp.int32, (1, 2 * BLK), 1) < BLK
    upper = lax.broadcasted_iota(jnp.int32, (LANES, BLK), 0) < HEAD_DIM

    def locate(n):
        bi, nl = n // n_blk, n % n_blk
        r0 = pl.multiple_of(nl * BLK, BLK)
        ws = pl.multiple_of(jnp.clip(nl * BLK - C_HALF_WINDOW, 0, SEQ - C_KEYS), BLK)
        return bi, nl, r0, ws

    def scores(n, s_ref, m_ref):
        bi, nl, r0, ws = locate(n)
        var = jnp.where(nl == 0, 0, jnp.where(nl == n_blk - 1, 2, 1))
        for a in range(2):
            qa = [q_ref[bi, pl.ds(r0, BLK), c * LANES:(c + 1) * LANES] for c in range(rep // 2)]
            qm = jnp.concatenate(
                [jnp.where(lo, q, zero) if a == 0 else jnp.where(lo, zero, q) for q in qa], axis=0)
            st = _dot_nt(ka_ref[bi, a, pl.ds(ws, C_KEYS), :], qm) + bias_ref[0, a, var]
            s_ref[a] = st
            m_ref[a] = jnp.max(st, axis=0, keepdims=True)
            yield

    def finish(n, s_ref, m_ref):
        bi, _, r0, ws = locate(n)
        yts = []
        for a in range(2):
            m = m_ref[a]
            p = jnp.exp2(s_ref[a] - m).astype(BF16)
            acct = jnp.dot(vt_ref[bi, a, :, pl.ds(ws, C_KEYS)], p, preferred_element_type=F32)
            l = acct[LANES:LANES + 1]
            sk = jnp.where(left, sink_ref[grp * rep + a], sink_ref[grp * rep + a + 2]) * LOG2E
            mx = jnp.maximum(m, sk)
            e = jnp.exp2(m - mx)
            yts.append(acct[:LANES] * (e / (l * e + jnp.exp2(sk - mx))))
            yield
        for c in range(rep // 2):
            cols = slice(c * LANES, (c + 1) * LANES)
            y = jnp.where(upper, yts[0][:, cols], yts[1][:, cols]).T
            gate = _silu(g_ref[bi, pl.ds(r0, BLK), cols].astype(F32))
            o_ref[bi, pl.ds(r0, BLK), cols] = (y * gate).astype(BF16)

    bufs = [(s_refs[i], m_refs[i]) for i in range(2 * C_GROUP)]
    set_a, set_b = bufs[:C_GROUP], bufs[C_GROUP:]
    n_grp = C_NB * n_blk // C_GROUP

    def scores_of(g, bset):
        return [scores(g * C_GROUP + i, *bset[i]) for i in range(C_GROUP)]

    def finish_of(g, bset):
        return [finish(g * C_GROUP + i, *bset[i]) for i in range(C_GROUP)]

    _interleave(*scores_of(0, set_a))

    def body(j, carry):
        _interleave(*finish_of(2 * j, set_a), *scores_of(2 * j + 1, set_b))
        _interleave(*finish_of(2 * j + 1, set_b), *scores_of(2 * j + 2, set_a))
        return carry

    lax.fori_loop(0, n_grp // 2 - 1, body, 0, unroll=True)
    _interleave(*finish_of(n_grp - 2, set_a), *scores_of(n_grp - 1, set_b))
    _interleave(*finish_of(n_grp - 1, set_b))


def _mixer_c(proj, sink, bias):
    width = C_HEADS // C_KV_HEADS * HEAD_DIM
    kb = C_HEADS * HEAD_DIM // LANES
    vb = kb + C_KV_HEADS * HEAD_DIM // LANES
    gb = (C_HEADS + 2 * C_KV_HEADS) * HEAD_DIM // width
    return pl.pallas_call(
        _c_kernel,
        grid=(C_KV_HEADS, BATCH // C_NB),
        in_specs=[
            pl.BlockSpec(memory_space=pltpu.SMEM),
            pl.BlockSpec((C_NB, SEQ, width), lambda g, b: (b, 0, g)),
            pl.BlockSpec((C_NB, SEQ, LANES), lambda g, b: (b, 0, kb + g // 2)),
            pl.BlockSpec((C_NB, SEQ, LANES), lambda g, b: (b, 0, vb + g // 2)),
            pl.BlockSpec((C_NB, SEQ, width), lambda g, b: (b, 0, gb + g)),
            pl.BlockSpec((1, 2, 3, C_KEYS, 2 * BLK), lambda g, b: (g, 0, 0, 0, 0)),
        ],
        out_specs=pl.BlockSpec((C_NB, SEQ, width), lambda g, b: (b, 0, g)),
        out_shape=jax.ShapeDtypeStruct((BATCH, SEQ, C_HEADS * HEAD_DIM), BF16),
        scratch_shapes=[pltpu.VMEM((C_NB, 2, SEQ, LANES), BF16),
                        pltpu.VMEM((C_NB, 2, LANES + ONES_ROWS, SEQ), BF16)]
        + [pltpu.VMEM((2, C_KEYS, 2 * BLK), F32)] * (2 * C_GROUP)
        + [pltpu.VMEM((2, 1, 2 * BLK), F32)] * (2 * C_GROUP),
        compiler_params=pltpu.CompilerParams(
            dimension_semantics=("arbitrary", "arbitrary"), vmem_limit_bytes=VMEM_LIMIT),
        name="mixer_c_windowed",
    )(sink, proj, proj, proj, proj, bias)


@functools.lru_cache(maxsize=None)
def _bias_tables():
    def stack_pairs(t, first, second):
        return np.concatenate([t[first], t[second]], axis=2)

    sa = _alibi_slopes(A_HEADS)
    offs = (0, A_HALF, 2 * A_HALF)
    ev, od = slice(0, None, 2), slice(1, None, 2)
    b12 = np.stack(
        [stack_pairs(_band_bias(sa, dil, A_HALF, BLK, A_KEYS, offs), ev, od)
         for _, dil in A_PATTERNS[:2]], axis=1)
    b3 = stack_pairs(_band_bias(sa, A_PATTERNS[2][1], A_HALF, BLK, BLK, (0,)), ev, od)[:, 0]
    sc = _alibi_slopes(C_HEADS)
    bc_heads = _band_bias(sc, 1, C_HALF_WINDOW, BLK, C_KEYS,
                          (0, C_HALF_WINDOW, 2 * C_HALF_WINDOW))
    rep = C_HEADS // C_KV_HEADS
    bc = np.stack([stack_pairs(bc_heads, slice(a, None, rep), slice(a + 2, None, rep))
                   for a in range(2)], axis=1)
    bc = np.ascontiguousarray(np.swapaxes(bc, -1, -2))
    sb = _alibi_slopes(B_HEADS)
    il = np.arange(B_TQ)[None, :]
    uu = np.arange(2 * SEQ - B_TQ)[:, None]
    dist = np.abs(il - uu + (SEQ - B_TQ)).astype(np.float32)
    ub = -sb[:, None, None] * dist[None] * np.float32(LOG2E)
    return b12, b3, bc, ub.astype(np.float32)


def kernel(x, c, ada_w, ada_b, norm_g, ab_w_in, ab_w_out, diff_lq1, diff_lk1, diff_lq2,
           diff_lk2, diff_subln_g, c_w_in, c_w_out, c_sink, final_g):
    b12, b3, bc, ub = (jnp.asarray(t) for t in _bias_tables())
    mod = _modulation(c, ada_w, ada_b)
    a_w = A_HEADS * HEAD_DIM
    q_even = ((0, a_w), (3 * a_w, 3 * a_w + B_HEADS * 2 * HEAD_DIM))
    q_odd = ((0, C_HEADS * HEAD_DIM),)

    def in_stage(layer):
        shift = mod[layer, :, :D_MODEL]
        scale = mod[layer, :, D_MODEL:2 * D_MODEL]
        w, q_cols = (ab_w_in, q_even) if layer % 2 == 0 else (c_w_in, q_odd)
        return norm_g[layer], scale, shift, w, layer // 2, q_cols

    _, proj = _proj(x, in_stage=in_stage(0))
    for layer in range(DEPTH):
        gate = mod[layer, :, 2 * D_MODEL:]
        j = layer // 2
        last = layer == DEPTH - 1
        if layer % 2 == 0:
            lam_init = 0.8 - 0.6 * math.exp(-0.3 * layer)
            za, zb = _mixer_ab(proj, b12, b3, ub, diff_lq1[j], diff_lk1[j], diff_lq2[j],
                               diff_lk2[j], diff_subln_g[j], lam_init)
            zs, w_out = [za, zb], ab_w_out
        else:
            zs, w_out = [_mixer_c(proj, c_sink[j], bc)], c_w_out
        x, proj = _proj(x, out_stage=(gate, zs, w_out, j),
                        final_g=final_g if last else None,
                        in_stage=None if last else in_stage(layer + 1))
    return x
```

```python
import functools
import math

import numpy as np
import jax
import jax.numpy as jnp
from jax import lax
from jax.experimental import pallas as pl
from jax.experimental.pallas import tpu as pltpu

D_MODEL = 1024
BATCH = 8
SEQ = 2048
DEPTH = 4
HEAD_DIM = 64
LANES = 128
A_HEADS = 8
A_PATTERNS = ((128, 1), (512, 4), (2048, 16))
A_HALF = 64
B_HEADS = 4
C_HEADS = 16
C_KV_HEADS = 4
C_HALF_WINDOW = 128
EVEN_IN = 4096
ODD_IN = 2560
EPS = 1e-6
NEG_INF = -1e30
LOG2E = math.log2(math.e)
Q_FOLD = HEAD_DIM ** -0.5 * LOG2E
ONES_ROWS = 16

F32 = jnp.float32
BF16 = jnp.bfloat16

ROW_TILE = 512
OUT_ONLY_ROW_TILE = 1024
COL_TILE = 512
W_CAST_ROWS = 128
BLK = 128
A_KEYS = BLK + 2 * A_HALF
C_KEYS = BLK + 2 * C_HALF_WINDOW
A_MERGE_ROWS = 256
C_GROUP = 2
C_NB = 1
B_TQ = 256
B_TK = 256
B_NB = 2
VMEM_LIMIT = 56 * 1024 * 1024


def _silu(t):
    return t * (1.0 / (1.0 + jnp.exp(-t)))


def _dot_nt(a, b):
    return lax.dot_general(a, b, (((1,), (1,)), ((), ())), preferred_element_type=F32)


def _alibi_slopes(n):
    return (2.0 ** (-8.0 * np.arange(1, n + 1, dtype=np.float32) / n)).astype(np.float32)


def _band_bias(slopes, spacing, half, tq, tk, offsets):
    i = np.arange(tq)[:, None]
    c = np.arange(tk)[None, :]
    out = np.empty((len(slopes), len(offsets), tq, tk), np.float32)
    for v, off in enumerate(offsets):
        rel = np.abs(c - (i + off))
        dist = (rel * spacing).astype(np.float32)
        for h, m in enumerate(slopes):
            out[h, v] = np.where(rel <= half, -m * dist * np.float32(LOG2E), np.float32(NEG_INF))
    return out


def _split_bf16(a):
    hi = a.astype(BF16)
    return hi, (a - hi.astype(F32)).astype(BF16)


def _mod_kernel(c_ref, w_ref, b_ref, o_ref):
    c_hi, c_lo = _split_bf16(_silu(c_ref[...]))
    w_hi, w_lo = _split_bf16(w_ref[0])
    dot = functools.partial(jnp.dot, preferred_element_type=F32)
    o_ref[0] = dot(c_hi, w_hi) + (dot(c_hi, w_lo) + dot(c_lo, w_hi)) + b_ref[0]


def _modulation(c, ada_w, ada_b):
    nblk = 3 * D_MODEL // D_MODEL
    return pl.pallas_call(
        _mod_kernel,
        grid=(DEPTH, nblk),
        in_specs=[
            pl.BlockSpec((BATCH, D_MODEL), lambda l, j: (0, 0)),
            pl.BlockSpec((1, D_MODEL, D_MODEL), lambda l, j: (l, 0, j)),
            pl.BlockSpec((1, 1, D_MODEL), lambda l, j: (l, 0, j)),
        ],
        out_specs=pl.BlockSpec((1, BATCH, D_MODEL), lambda l, j: (l, 0, j)),
        out_shape=jax.ShapeDtypeStruct((DEPTH, BATCH, 3 * D_MODEL), F32),
        name="adaln_mod",
    )(c, ada_w, ada_b.reshape(DEPTH, 1, 3 * D_MODEL))


def _rms(x, g):
    ms = jnp.mean(x * x, axis=-1, keepdims=True)
    return x * lax.rsqrt(ms + EPS) * g


def _proj_kernel(*refs, n_z, final, n_out, q_cols):
    refs = iter(refs)
    x_ref = next(refs)
    if n_z:
        gate_ref = next(refs)
        z_refs = [next(refs) for _ in range(n_z)]
        wo_ref = next(refs)
    if final:
        fg_ref = next(refs)
    if n_out:
        g_ref, sc_ref, sh_ref, wi_ref = (next(refs) for _ in range(4))
    if n_z:
        xo_ref = next(refs)
    if n_out:
        proj_ref = next(refs)
    if n_z:
        wob_ref = next(refs)
    if n_out:
        wib_ref, h_ref = next(refs), next(refs)

    @pl.when((pl.program_id(0) == 0) & (pl.program_id(1) == 0))
    def _cast_weights():
        for src_ref, dst_ref in ([(wo_ref, wob_ref)] if n_z else []) + (
                [(wi_ref, wib_ref)] if n_out else []):
            for r in range(0, D_MODEL, W_CAST_ROWS):
                dst_ref[r:r + W_CAST_ROWS, :] = src_ref[r:r + W_CAST_ROWS, :].astype(BF16)

    xn = x_ref[0]
    if n_z:
        kz = D_MODEL // n_z
        acc = jnp.dot(z_refs[0][0], wob_ref[0:kz, :], preferred_element_type=F32)
        for i in range(1, n_z):
            acc = acc + jnp.dot(z_refs[i][0], wob_ref[i * kz:(i + 1) * kz, :],
                                preferred_element_type=F32)
        xn = xn + gate_ref[0] * acc
        xo_ref[0] = _rms(xn, fg_ref[...]) if final else xn
    if n_out:
        h_ref[...] = (_rms(xn, g_ref[...]) * (1.0 + sc_ref[0]) + sh_ref[0]).astype(BF16)
        for j in range(n_out // COL_TILE):
            lo, hi = j * COL_TILE, (j + 1) * COL_TILE
            acc = jnp.dot(h_ref[...], wib_ref[:, lo:hi], preferred_element_type=F32)
            if any(q0 <= lo and hi <= q1 for q0, q1 in q_cols):
                acc = acc * Q_FOLD
            proj_ref[0, :, lo:hi] = acc.astype(BF16)


def _proj(x, out_stage=None, final_g=None, in_stage=None):
    tile = ROW_TILE if in_stage is not None else OUT_ONLY_ROW_TILE
    row = pl.BlockSpec((1, tile, D_MODEL), lambda b, i: (b, i, 0))
    per_batch = pl.BlockSpec((1, 1, D_MODEL), lambda b, i: (b, 0, 0))
    vec = pl.BlockSpec((1, D_MODEL), lambda b, i: (0, 0))
    def weight(stack, j):
        return pl.BlockSpec((None,) + stack.shape[1:], lambda b, i: (j, 0, 0),
                            pipeline_mode=pl.Buffered(1))
    in_specs, args, out_specs, out_shape, scratch = [row], [x], [], [], []
    n_z = n_out = 0
    q_cols = ()
    assert final_g is None or out_stage is not None
    if out_stage is not None:
        gate, zs, w_out, j_out = out_stage
        n_z = len(zs)
        in_specs += [per_batch]
        in_specs += [pl.BlockSpec((1, tile, D_MODEL // n_z), lambda b, i: (b, i, 0))] * n_z
        in_specs += [weight(w_out, j_out)]
        args += [gate.reshape(BATCH, 1, D_MODEL), *zs, w_out]
        out_specs += [row]
        out_shape += [jax.ShapeDtypeStruct((BATCH, SEQ, D_MODEL), F32)]
        scratch += [pltpu.VMEM((D_MODEL, D_MODEL), BF16)]
    if final_g is not None:
        in_specs += [vec]
        args += [final_g.reshape(1, D_MODEL)]
    if in_stage is not None:
        g, scale, shift, w_in, j_in, q_cols = in_stage
        n_out = w_in.shape[2]
        assert all(q0 % COL_TILE == 0 and q1 % COL_TILE == 0 for q0, q1 in q_cols)
        in_specs += [vec, per_batch, per_batch, weight(w_in, j_in)]
        args += [g.reshape(1, D_MODEL), scale.reshape(BATCH, 1, D_MODEL),
                 shift.reshape(BATCH, 1, D_MODEL), w_in]
        out_specs += [pl.BlockSpec((1, tile, n_out), lambda b, i: (b, i, 0))]
        out_shape += [jax.ShapeDtypeStruct((BATCH, SEQ, n_out), BF16)]
        scratch += [pltpu.VMEM((D_MODEL, n_out), BF16), pltpu.VMEM((tile, D_MODEL), BF16)]
    outs = pl.pallas_call(
        functools.partial(_proj_kernel, n_z=n_z, final=final_g is not None, n_out=n_out,
                          q_cols=q_cols),
        grid=(BATCH, SEQ // tile),
        in_specs=in_specs,
        out_specs=out_specs,
        out_shape=out_shape,
        scratch_shapes=scratch,
        compiler_params=pltpu.CompilerParams(
            dimension_semantics=("arbitrary", "arbitrary"), vmem_limit_bytes=VMEM_LIMIT),
        name="proj",
    )(*args)
    outs = list(outs)
    x_new = outs.pop(0) if out_stage is not None else None
    proj = outs.pop(0) if in_stage is not None else None
    return x_new, proj


def _lane_lo(rows):
    return lax.broadcasted_iota(jnp.int32, (rows, LANES), 1) < HEAD_DIM


def _softmax_block(qm, kw, vw, bias):
    s = _dot_nt(qm, kw) + bias
    m = jnp.max(s, axis=-1, keepdims=True)
    p = jnp.exp2(s - m)
    l = jnp.sum(p, axis=-1, keepdims=True)
    acc = jnp.dot(p.astype(BF16), vw, preferred_element_type=F32)
    return acc, m, l


def _interleave(*stages):
    live = list(stages)
    while live:
        for g in list(live):
            try:
                next(g)
            except StopIteration:
                live.remove(g)


def _stack_heads(qb, lo):
    zero = jnp.zeros_like(qb)
    return jnp.concatenate([jnp.where(lo, qb, zero), jnp.where(lo, zero, qb)], axis=0)


A_SCRATCH = (
    [pltpu.VMEM((SEQ, LANES), F32)] * 6
    + [pltpu.VMEM((SEQ, LANES), BF16)] * 6
    + [pltpu.VMEM((3, SEQ, LANES), F32)] * 3
)


def _a_stages(q_ref, k_ref, v_ref, g_ref, b12_ref, b3_ref, o_ref,
              qf_ref, kf_ref, vf_ref, q4_ref, k4_ref, v4_ref, qd_ref, kd_ref, vd_ref,
              qe_ref, ke_ref, ve_ref, acc_ref, m_ref, l_ref):
    lo = _lane_lo(BLK)

    def pair_block(qb, kw, vw, bias):
        acc, m, l = _softmax_block(_stack_heads(qb, lo), kw, vw, bias)
        return (jnp.where(lo, acc[:BLK], acc[BLK:]), jnp.where(lo, m[:BLK], m[BLK:]),
                jnp.where(lo, l[:BLK], l[BLK:]))

    n_blk = SEQ // BLK
    dil1 = A_PATTERNS[1][1]
    seg1 = SEQ // dil1
    per1 = seg1 // BLK
    dil2 = A_PATTERNS[2][1]
    sub = dil2 // dil1
    assert sub * dil1 == dil2
    seg2 = SEQ // dil2
    streams = ((qf_ref, q4_ref, qd_ref, qe_ref), (kf_ref, k4_ref, kd_ref, ke_ref),
               (vf_ref, v4_ref, vd_ref, ve_ref))

    def prologue():
        chunk = 512
        for c in range(SEQ // chunk):
            rows = slice(c * chunk, (c + 1) * chunk)
            qf_ref[rows, :] = q_ref[0, rows, :].astype(F32)
            kf_ref[rows, :] = k_ref[0, rows, :].astype(F32)
            vf_ref[rows, :] = v_ref[0, rows, :].astype(F32)
            yield
        for r in range(dil1):
            dst = slice(r * seg1, (r + 1) * seg1)
            for src_ref, mid_ref, dst_ref, _ in streams:
                t = src_ref[pl.ds(r, seg1, stride=dil1), :]
                mid_ref[dst, :] = t
                dst_ref[dst, :] = t.astype(BF16)
            yield
        for r in range(dil1):
            for r2 in range(sub):
                res = dil1 * r2 + r
                dst = slice(res * seg2, (res + 1) * seg2)
                for _, mid_ref, _, dst_ref in streams:
                    dst_ref[dst, :] = (
                        mid_ref[pl.ds(r * seg1 + r2, seg2, stride=sub), :].astype(BF16))
            yield

    def aligned(x, m):
        return x if isinstance(x, int) else pl.multiple_of(x, m)

    def clamp(x, hi):
        return min(max(x, 0), hi) if isinstance(x, int) else jnp.clip(x, 0, hi)

    def variant(blk, last):
        if isinstance(blk, int):
            return 0 if blk == 0 else (2 if blk == last else 1)
        return jnp.where(blk == 0, 0, jnp.where(blk == last, 2, 1))

    def p0_body(j):
        r0 = aligned(j * BLK, BLK)
        ws = aligned(clamp(j * BLK - A_HALF, SEQ - A_KEYS), A_HALF)
        var = variant(j, n_blk - 1)
        acc, m, l = pair_block(q_ref[0, pl.ds(r0, BLK), :],
                               k_ref[0, pl.ds(ws, A_KEYS), :], v_ref[0, pl.ds(ws, A_KEYS), :],
                               b12_ref[0, 0, var])
        acc_ref[0, pl.ds(r0, BLK), :] = acc
        m_ref[0, pl.ds(r0, BLK), :] = m
        l_ref[0, pl.ds(r0, BLK), :] = l

    def p1_body(j):
        r = j // per1
        blk = j % per1
        r0 = aligned(j * BLK, BLK)
        ws = aligned(r * seg1 + clamp(blk * BLK - A_HALF, seg1 - A_KEYS), A_HALF)
        var = variant(blk, per1 - 1)
        acc, m, l = pair_block(qd_ref[pl.ds(r0, BLK), :],
                               kd_ref[pl.ds(ws, A_KEYS), :], vd_ref[pl.ds(ws, A_KEYS), :],
                               b12_ref[0, 1, var])
        dst = pl.ds(blk * BLK * dil1 + r, BLK, stride=dil1)
        acc_ref[1, dst, :] = acc
        m_ref[1, dst, :] = m
        l_ref[1, dst, :] = l

    def p2_body(j):
        r0 = aligned(j * BLK, BLK)
        acc, m, l = pair_block(qe_ref[pl.ds(r0, BLK), :],
                               ke_ref[pl.ds(r0, BLK), :], ve_ref[pl.ds(r0, BLK), :],
                               b3_ref[0])
        dst = pl.ds(j, BLK, stride=dil2)
        acc_ref[2, dst, :] = acc
        m_ref[2, dst, :] = m
        l_ref[2, dst, :] = l

    bodies = (p0_body, p1_body, p2_body)

    def blocks(work):
        for pattern, j in work:
            bodies[pattern](j)
            yield

    def merge(c):
        rows = pl.ds(aligned(c * A_MERGE_ROWS, A_MERGE_ROWS), A_MERGE_ROWS)
        m0, m1, m2 = m_ref[0, rows, :], m_ref[1, rows, :], m_ref[2, rows, :]
        mx = jnp.maximum(jnp.maximum(m0, m1), m2)
        w0, w1, w2 = jnp.exp2(m0 - mx), jnp.exp2(m1 - mx), jnp.exp2(m2 - mx)
        num = w0 * acc_ref[0, rows, :] + w1 * acc_ref[1, rows, :] + w2 * acc_ref[2, rows, :]
        den = w0 * l_ref[0, rows, :] + w1 * l_ref[1, rows, :] + w2 * l_ref[2, rows, :]
        y = num / den
        o_ref[0, rows, :] = (y * _silu(g_ref[0, rows, :].astype(F32))).astype(BF16)

    return prologue, blocks, merge


B_SCRATCH = (
    [pltpu.VMEM((B_NB, LANES + ONES_ROWS, SEQ), BF16)]
    + [pltpu.VMEM((SEQ, 2 * B_TQ), F32)] * 2
    + [pltpu.VMEM((1, 2 * B_TQ), F32)] * 2
    + [pltpu.VMEM((LANES + ONES_ROWS, 2 * B_TQ), F32)] * 2
)


def _b_stages(q_ref, k_ref, v_ref, g_ref, u_ref, lq1_ref, lk1_ref, lq2_ref, lk2_ref,
              sg_ref, o_ref, vt_ref, sa_ref, sb_ref, ma_ref, mb_ref, acca_ref, accb_ref,
              *, lam_init):
    lo = _lane_lo(B_TQ)
    n_t = SEQ // B_TK
    lam = (jnp.exp(jnp.sum(lq1_ref[...] * lk1_ref[...], axis=-1, keepdims=True))
           - jnp.exp(jnp.sum(lq2_ref[...] * lk2_ref[...], axis=-1, keepdims=True)) + lam_init)

    eye = jnp.where(lax.broadcasted_iota(jnp.int32, (LANES, LANES), 0)
                    == lax.broadcasted_iota(jnp.int32, (LANES, LANES), 1), 1.0, 0.0).astype(BF16)
    ones_row = jnp.where(lax.broadcasted_iota(jnp.int32, (ONES_ROWS, B_TK), 0) == 0,
                         1.0, 0.0).astype(BF16)
    for bi in range(B_NB):
        for t in range(n_t):
            rows = slice(t * B_TK, (t + 1) * B_TK)
            vt_ref[bi, :LANES, rows] = _dot_nt(eye, v_ref[bi, rows, :]).astype(BF16)
            vt_ref[bi, LANES:, rows] = ones_row

    n_q = SEQ // B_TQ

    def locate(i):
        if isinstance(i, int):
            return i // n_q, (i % n_q) * B_TQ
        return i // n_q, pl.multiple_of((i % n_q) * B_TQ, B_TQ)

    def scores(i, s_ref, m_ref):
        bi, r0 = locate(i)
        qst = _stack_heads(q_ref[bi, pl.ds(r0, B_TQ), :], lo)
        m = None
        for t in range(n_t):
            rows = slice(t * B_TK, (t + 1) * B_TK)
            off = pl.multiple_of(t * B_TK + (SEQ - B_TQ) - r0, LANES)
            ub = u_ref[0, pl.ds(off, B_TK), :]
            st = _dot_nt(k_ref[bi, rows, :], qst)
            s1 = st[:, :B_TQ] + ub
            s2 = st[:, B_TQ:] + ub
            s_ref[rows, :B_TQ] = s1
            s_ref[rows, B_TQ:] = s2
            f = jnp.concatenate([jnp.max(s1, axis=0, keepdims=True),
                                 jnp.max(s2, axis=0, keepdims=True)], axis=1)
            m = f if m is None else jnp.maximum(m, f)
            yield
        m_ref[...] = m

    def softmax_pv(i, s_ref, m_ref, acc_ref):
        bi, _ = locate(i)
        m = m_ref[...]
        acct = None
        for t in range(n_t):
            rows = slice(t * B_TK, (t + 1) * B_TK)
            p = jnp.exp2(s_ref[rows, :] - m).astype(BF16)
            d = jnp.dot(vt_ref[bi, :, rows], p, preferred_element_type=F32)
            acct = d if acct is None else acct + d
            yield
        acc_ref[...] = acct

    def epilogue(i, acc_ref):
        bi, r0 = locate(i)
        ot = acc_ref[:LANES, :] * (1.0 / acc_ref[LANES:LANES + 1, :])
        outt = ot[:, :B_TQ] - lam * ot[:, B_TQ:]
        ms = jnp.mean(outt * outt, axis=0, keepdims=True)
        yt = outt * lax.rsqrt(ms + EPS) * sg_ref[...] * (1.0 - lam_init)
        yield
        gate = _silu(g_ref[bi, pl.ds(r0, B_TQ), :].astype(F32))
        o_ref[bi, pl.ds(r0, B_TQ), :] = (yt.T * gate).astype(BF16)

    return scores, softmax_pv, epilogue, (sa_ref, ma_ref, acca_ref), (sb_ref, mb_ref, accb_ref)


def _b_kernel(*refs, lam_init):
    scores, softmax_pv, epilogue, (sa, ma, acca), (sb, mb, accb) = _b_stages(
        *refs, lam_init=lam_init)
    n = B_NB * (SEQ // B_TQ)
    _interleave(scores(0, sa, ma))
    _interleave(scores(1, sb, mb), softmax_pv(0, sa, ma, acca))

    def body(j, carry):
        i = 2 * j
        _interleave(scores(i + 2, sa, ma), softmax_pv(i + 1, sb, mb, accb), epilogue(i, acca))
        _interleave(scores(i + 3, sb, mb), softmax_pv(i + 2, sa, ma, acca), epilogue(i + 1, accb))
        return carry

    lax.fori_loop(0, n // 2 - 1, body, 0)
    _interleave(softmax_pv(n - 1, sb, mb, accb), epilogue(n - 2, acca))
    _interleave(epilogue(n - 1, accb))


def _a_kernel(*refs):
    prologue, blocks, merge = _a_stages(*refs)
    _interleave(prologue())
    _interleave(blocks([(p, j) for p in range(len(A_PATTERNS)) for j in range(SEQ // BLK)]))

    def merge_body(c, carry):
        merge(c)
        return carry

    lax.fori_loop(0, SEQ // A_MERGE_ROWS, merge_body, 0)


def _mixer_ab(proj, bias12, bias3, u, lq1, lk1, lq2, lk2, subln_g, lam_init):
    pairs = A_HEADS // 2
    a_w = A_HEADS * HEAD_DIM // LANES
    b_q = 3 * a_w
    g0 = (3 * A_HEADS * HEAD_DIM + 3 * B_HEADS * 2 * HEAD_DIM) // LANES
    blk = lambda off: pl.BlockSpec((1, SEQ, LANES), lambda i, b: (b, 0, off + i))
    vec = lambda n: pl.BlockSpec((1, n), lambda i, b: (0, 0))
    out = pl.BlockSpec((1, SEQ, LANES), lambda i, b: (b, 0, i))
    params = pltpu.CompilerParams(
        dimension_semantics=("arbitrary", "arbitrary"), vmem_limit_bytes=VMEM_LIMIT)
    za = pl.pallas_call(
        _a_kernel,
        grid=(pairs, BATCH),
        in_specs=[
            blk(0), blk(a_w), blk(2 * a_w), blk(g0),
            pl.BlockSpec((1, 2, 3, 2 * BLK, A_KEYS), lambda i, b: (i, 0, 0, 0, 0)),
            pl.BlockSpec((1, 2 * BLK, BLK), lambda i, b: (i, 0, 0)),
        ],
        out_specs=out,
        out_shape=jax.ShapeDtypeStruct((BATCH, SEQ, A_HEADS * HEAD_DIM), BF16),
        scratch_shapes=A_SCRATCH,
        compiler_params=params,
        name="mixer_a_dilated",
    )(proj, proj, proj, proj, bias12, bias3)
    blk = lambda off: pl.BlockSpec((B_NB, SEQ, LANES), lambda i, b: (b, 0, off + i))
    zb = pl.pallas_call(
        functools.partial(_b_kernel, lam_init=lam_init),
        grid=(B_HEADS, BATCH // B_NB),
        in_specs=[
            blk(b_q), blk(b_q + B_HEADS), blk(b_q + 2 * B_HEADS), blk(g0 + a_w),
            pl.BlockSpec((1, 2 * SEQ - B_TQ, B_TQ), lambda i, b: (i, 0, 0)),
            vec(HEAD_DIM), vec(HEAD_DIM), vec(HEAD_DIM), vec(HEAD_DIM),
            pl.BlockSpec((2 * HEAD_DIM, 1), lambda i, b: (0, 0)),
        ],
        out_specs=blk(0),
        out_shape=jax.ShapeDtypeStruct((BATCH, SEQ, B_HEADS * 2 * HEAD_DIM), BF16),
        scratch_shapes=B_SCRATCH,
        compiler_params=params,
        name="mixer_b_differential",
    )(proj, proj, proj, proj, u, lq1.reshape(1, -1), lk1.reshape(1, -1),
      lq2.reshape(1, -1), lk2.reshape(1, -1), subln_g.reshape(-1, 1))
    return za, zb


def _c_kernel(sink_ref, q_ref, k_ref, v_ref, g_ref, bias_ref, o_ref, ka_ref, vt_ref, *bufs):
    s_refs, m_refs = bufs[:2 * C_GROUP], bufs[2 * C_GROUP:]
    grp = pl.program_id(0)
    half = grp % 2
    lo = _lane_lo(BLK)
    zero = jnp.zeros((BLK, LANES), BF16)
    row = lax.broadcasted_iota(jnp.int32, (LANES, LANES), 0)
    col = lax.broadcasted_iota(jnp.int32, (LANES, LANES), 1)

    chunk = 512
    ones_row = jnp.where(lax.broadcasted_iota(jnp.int32, (ONES_ROWS, chunk), 0) == 0,
                         1.0, 0.0).astype(BF16)
    for a in range(2):
        shift = jnp.where(half == a, 0, HEAD_DIM)
        perm = jnp.where(col == (row + shift) % LANES, 1.0, 0.0).astype(BF16)
        for bi in range(C_NB):
            for c in range(SEQ // chunk):
                rows = slice(c * chunk, (c + 1) * chunk)
                ka_ref[bi, a, rows, :] = jnp.dot(k_ref[bi, rows, :], perm,
                                                 preferred_element_type=F32).astype(BF16)
                vt_ref[bi, a, :LANES, rows] = _dot_nt(perm, v_ref[bi, rows, :]).astype(BF16)
                vt_ref[bi, a, LANES:, rows] = ones_row

    n_blk = SEQ // BLK
    rep = C_HEADS // C_KV_HEADS
    left = lax.broadcasted_iota(jnp.int32, (1, 2 * BLK), 1) < BLK
    upper = lax.broadcasted_iota(jnp.int32, (LANES, BLK), 0) < HEAD_DIM

    def locate(n):
        bi, nl = n // n_blk, n % n_blk
        r0 = pl.multiple_of(nl * BLK, BLK)
        ws = pl.multiple_of(jnp.clip(nl * BLK - C_HALF_WINDOW, 0, SEQ - C_KEYS), BLK)
        return bi, nl, r0, ws

    def scores(n, s_ref, m_ref):
        bi, nl, r0, ws = locate(n)
        var = jnp.where(nl == 0, 0, jnp.where(nl == n_blk - 1, 2, 1))
        for a in range(2):
            qa = [q_ref[bi, pl.ds(r0, BLK), c * LANES:(c + 1) * LANES] for c in range(rep // 2)]
            qm = jnp.concatenate(
                [jnp.where(lo, q, zero) if a == 0 else jnp.where(lo, zero, q) for q in qa], axis=0)
            st = _dot_nt(ka_ref[bi, a, pl.ds(ws, C_KEYS), :], qm) + bias_ref[0, a, var]
            s_ref[a] = st
            m_ref[a] = jnp.max(st, axis=0, keepdims=True)
            yield

    def finish(n, s_ref, m_ref):
        bi, _, r0, ws = locate(n)
        yts = []
        for a in range(2):
            m = m_ref[a]
            p = jnp.exp2(s_ref[a] - m).astype(BF16)
            acct = jnp.dot(vt_ref[bi, a, :, pl.ds(ws, C_KEYS)], p, preferred_element_type=F32)
            l = acct[LANES:LANES + 1]
            sk = jnp.where(left, sink_ref[grp * rep + a], sink_ref[grp * rep + a + 2]) * LOG2E
            mx = jnp.maximum(m, sk)
            e = jnp.exp2(m - mx)
            yts.append(acct[:LANES] * (e / (l * e + jnp.exp2(sk - mx))))
            yield
        for c in range(rep // 2):
            cols = slice(c * LANES, (c + 1) * LANES)
            y = jnp.where(upper, yts[0][:, cols], yts[1][:, cols]).T
            gate = _silu(g_ref[bi, pl.ds(r0, BLK), cols].astype(F32))
            o_ref[bi, pl.ds(r0, BLK), cols] = (y * gate).astype(BF16)

    bufs = [(s_refs[i], m_refs[i]) for i in range(2 * C_GROUP)]
    set_a, set_b = bufs[:C_GROUP], bufs[C_GROUP:]
    n_grp = C_NB * n_blk // C_GROUP

    def scores_of(g, bset):
        return [scores(g * C_GROUP + i, *bset[i]) for i in range(C_GROUP)]

    def finish_of(g, bset):
        return [finish(g * C_GROUP + i, *bset[i]) for i in range(C_GROUP)]

    _interleave(*scores_of(0, set_a))

    def body(j, carry):
        _interleave(*finish_of(2 * j, set_a), *scores_of(2 * j + 1, set_b))
        _interleave(*finish_of(2 * j + 1, set_b), *scores_of(2 * j + 2, set_a))
        return carry

    lax.fori_loop(0, n_grp // 2 - 1, body, 0, unroll=True)
    _interleave(*finish_of(n_grp - 2, set_a), *scores_of(n_grp - 1, set_b))
    _interleave(*finish_of(n_grp - 1, set_b))


def _mixer_c(proj, sink, bias):
    width = C_HEADS // C_KV_HEADS * HEAD_DIM
    kb = C_HEADS * HEAD_DIM // LANES
    vb = kb + C_KV_HEADS * HEAD_DIM // LANES
    gb = (C_HEADS + 2 * C_KV_HEADS) * HEAD_DIM // width
    return pl.pallas_call(
        _c_kernel,
        grid=(C_KV_HEADS, BATCH // C_NB),
        in_specs=[
            pl.BlockSpec(memory_space=pltpu.SMEM),
            pl.BlockSpec((C_NB, SEQ, width), lambda g, b: (b, 0, g)),
            pl.BlockSpec((C_NB, SEQ, LANES), lambda g, b: (b, 0, kb + g // 2)),
            pl.BlockSpec((C_NB, SEQ, LANES), lambda g, b: (b, 0, vb + g // 2)),
            pl.BlockSpec((C_NB, SEQ, width), lambda g, b: (b, 0, gb + g)),
            pl.BlockSpec((1, 2, 3, C_KEYS, 2 * BLK), lambda g, b: (g, 0, 0, 0, 0)),
        ],
        out_specs=pl.BlockSpec((C_NB, SEQ, width), lambda g, b: (b, 0, g)),
        out_shape=jax.ShapeDtypeStruct((BATCH, SEQ, C_HEADS * HEAD_DIM), BF16),
        scratch_shapes=[pltpu.VMEM((C_NB, 2, SEQ, LANES), BF16),
                        pltpu.VMEM((C_NB, 2, LANES + ONES_ROWS, SEQ), BF16)]
        + [pltpu.VMEM((2, C_KEYS, 2 * BLK), F32)] * (2 * C_GROUP)
        + [pltpu.VMEM((2, 1, 2 * BLK), F32)] * (2 * C_GROUP),
        compiler_params=pltpu.CompilerParams(
            dimension_semantics=("arbitrary", "arbitrary"), vmem_limit_bytes=VMEM_LIMIT),
        name="mixer_c_windowed",
    )(sink, proj, proj, proj, proj, bias)


@functools.lru_cache(maxsize=None)
def _bias_tables():
    def stack_pairs(t, first, second):
        return np.concatenate([t[first], t[second]], axis=2)

    sa = _alibi_slopes(A_HEADS)
    offs = (0, A_HALF, 2 * A_HALF)
    ev, od = slice(0, None, 2), slice(1, None, 2)
    b12 = np.stack(
        [stack_pairs(_band_bias(sa, dil, A_HALF, BLK, A_KEYS, offs), ev, od)
         for _, dil in A_PATTERNS[:2]], axis=1)
    b3 = stack_pairs(_band_bias(sa, A_PATTERNS[2][1], A_HALF, BLK, BLK, (0,)), ev, od)[:, 0]
    sc = _alibi_slopes(C_HEADS)
    bc_heads = _band_bias(sc, 1, C_HALF_WINDOW, BLK, C_KEYS,
                          (0, C_HALF_WINDOW, 2 * C_HALF_WINDOW))
    rep = C_HEADS // C_KV_HEADS
    bc = np.stack([stack_pairs(bc_heads, slice(a, None, rep), slice(a + 2, None, rep))
                   for a in range(2)], axis=1)
    bc = np.ascontiguousarray(np.swapaxes(bc, -1, -2))
    sb = _alibi_slopes(B_HEADS)
    il = np.arange(B_TQ)[None, :]
    uu = np.arange(2 * SEQ - B_TQ)[:, None]
    dist = np.abs(il - uu + (SEQ - B_TQ)).astype(np.float32)
    ub = -sb[:, None, None] * dist[None] * np.float32(LOG2E)
    return b12, b3, bc, ub.astype(np.float32)


def kernel(x, c, ada_w, ada_b, norm_g, ab_w_in, ab_w_out, diff_lq1, diff_lk1, diff_lq2,
           diff_lk2, diff_subln_g, c_w_in, c_w_out, c_sink, final_g):
    b12, b3, bc, ub = (jnp.asarray(t) for t in _bias_tables())
    mod = _modulation(c, ada_w, ada_b)
    a_w = A_HEADS * HEAD_DIM
    q_even = ((0, a_w), (3 * a_w, 3 * a_w + B_HEADS * 2 * HEAD_DIM))
    q_odd = ((0, C_HEADS * HEAD_DIM),)

    def in_stage(layer):
        shift = mod[layer, :, :D_MODEL]
        scale = mod[layer, :, D_MODEL:2 * D_MODEL]
        w, q_cols = (ab_w_in, q_even) if layer % 2 == 0 else (c_w_in, q_odd)
        return norm_g[layer], scale, shift, w, layer // 2, q_cols

    _, proj = _proj(x, in_stage=in_stage(0))
    for layer in range(DEPTH):
        gate = mod[layer, :, 2 * D_MODEL:]
        j = layer // 2
        last = layer == DEPTH - 1
        if layer % 2 == 0:
            lam_init = 0.8 - 0.6 * math.exp(-0.3 * layer)
            za, zb = _mixer_ab(proj, b12, b3, ub, diff_lq1[j], diff_lk1[j], diff_lq2[j],
                               diff_lk2[j], diff_subln_g[j], lam_init)
            zs, w_out = [za, zb], ab_w_out
        else:
            zs, w_out = [_mixer_c(proj, c_sink[j], bc)], c_w_out
        x, proj = _proj(x, out_stage=(gate, zs, w_out, j),
                        final_g=final_g if last else None,
                        in_stage=None if last else in_stage(layer + 1))
    return x
```

```python
import functools
import math

import numpy as np
import jax
import jax.numpy as jnp
from jax import lax
from jax.experimental import pallas as pl
from jax.experimental.pallas import tpu as pltpu

D_MODEL = 1024
BATCH = 8
SEQ = 2048
DEPTH = 4
HEAD_DIM = 64
LANES = 128
A_HEADS = 8
A_PATTERNS = ((128, 1), (512, 4), (2048, 16))
A_HALF = 64
B_HEADS = 4
C_HEADS = 16
C_KV_HEADS = 4
C_HALF_WINDOW = 128
EVEN_IN = 4096
ODD_IN = 2560
EPS = 1e-6
NEG_INF = -1e30
LOG2E = math.log2(math.e)
Q_FOLD = HEAD_DIM ** -0.5 * LOG2E
ONES_ROWS = 16

F32 = jnp.float32
BF16 = jnp.bfloat16

ROW_TILE = 512
OUT_ONLY_ROW_TILE = 1024
COL_TILE = 512
W_CAST_ROWS = 128
BLK = 128
A_KEYS = BLK + 2 * A_HALF
C_KEYS = BLK + 2 * C_HALF_WINDOW
A_MERGE_ROWS = 256
C_GROUP = 2
C_NB = 1
B_TQ = 256
B_TK = 256
B_NB = 2
VMEM_LIMIT = 56 * 1024 * 1024


def _silu(t):
    return t * (1.0 / (1.0 + jnp.exp(-t)))


def _dot_nt(a, b):
    return lax.dot_general(a, b, (((1,), (1,)), ((), ())), preferred_element_type=F32)


def _alibi_slopes(n):
    return (2.0 ** (-8.0 * np.arange(1, n + 1, dtype=np.float32) / n)).astype(np.float32)


def _band_bias(slopes, spacing, half, tq, tk, offsets):
    i = np.arange(tq)[:, None]
    c = np.arange(tk)[None, :]
    out = np.empty((len(slopes), len(offsets), tq, tk), np.float32)
    for v, off in enumerate(offsets):
        rel = np.abs(c - (i + off))
        dist = (rel * spacing).astype(np.float32)
        for h, m in enumerate(slopes):
            out[h, v] = np.where(rel <= half, -m * dist * np.float32(LOG2E), np.float32(NEG_INF))
    return out


def _split_bf16(a):
    hi = a.astype(BF16)
    return hi, (a - hi.astype(F32)).astype(BF16)


def _mod_kernel(c_ref, w_ref, b_ref, o_ref):
    c_hi, c_lo = _split_bf16(_silu(c_ref[...]))
    w_hi, w_lo = _split_bf16(w_ref[0])
    dot = functools.partial(jnp.dot, preferred_element_type=F32)
    o_ref[0] = dot(c_hi, w_hi) + (dot(c_hi, w_lo) + dot(c_lo, w_hi)) + b_ref[0]


def _modulation(c, ada_w, ada_b):
    nblk = 3 * D_MODEL // D_MODEL
    return pl.pallas_call(
        _mod_kernel,
        grid=(DEPTH, nblk),
        in_specs=[
            pl.BlockSpec((BATCH, D_MODEL), lambda l, j: (0, 0)),
            pl.BlockSpec((1, D_MODEL, D_MODEL), lambda l, j: (l, 0, j)),
            pl.BlockSpec((1, 1, D_MODEL), lambda l, j: (l, 0, j)),
        ],
        out_specs=pl.BlockSpec((1, BATCH, D_MODEL), lambda l, j: (l, 0, j)),
        out_shape=jax.ShapeDtypeStruct((DEPTH, BATCH, 3 * D_MODEL), F32),
        name="adaln_mod",
    )(c, ada_w, ada_b.reshape(DEPTH, 1, 3 * D_MODEL))


def _rms(x, g):
    ms = jnp.mean(x * x, axis=-1, keepdims=True)
    return x * lax.rsqrt(ms + EPS) * g


def _proj_kernel(*refs, n_z, final, n_out, q_cols):
    refs = iter(refs)
    x_ref = next(refs)
    if n_z:
        gate_ref = next(refs)
        z_refs = [next(refs) for _ in range(n_z)]
        wo_ref = next(refs)
    if final:
        fg_ref = next(refs)
    if n_out:
        g_ref, sc_ref, sh_ref, wi_ref = (next(refs) for _ in range(4))
    if n_z:
        xo_ref = next(refs)
    if n_out:
        proj_ref = next(refs)
    if n_z:
        wob_ref = next(refs)
    if n_out:
        wib_ref, h_ref = next(refs), next(refs)

    @pl.when((pl.program_id(0) == 0) & (pl.program_id(1) == 0))
    def _cast_weights():
        for src_ref, dst_ref in ([(wo_ref, wob_ref)] if n_z else []) + (
                [(wi_ref, wib_ref)] if n_out else []):
            for r in range(0, D_MODEL, W_CAST_ROWS):
                dst_ref[r:r + W_CAST_ROWS, :] = src_ref[r:r + W_CAST_ROWS, :].astype(BF16)

    xn = x_ref[0]
    if n_z:
        kz = D_MODEL // n_z
        acc = jnp.dot(z_refs[0][0], wob_ref[0:kz, :], preferred_element_type=F32)
        for i in range(1, n_z):
            acc = acc + jnp.dot(z_refs[i][0], wob_ref[i * kz:(i + 1) * kz, :],
                                preferred_element_type=F32)
        xn = xn + gate_ref[0] * acc
        xo_ref[0] = _rms(xn, fg_ref[...]) if final else xn
    if n_out:
        h_ref[...] = (_rms(xn, g_ref[...]) * (1.0 + sc_ref[0]) + sh_ref[0]).astype(BF16)
        for j in range(n_out // COL_TILE):
            lo, hi = j * COL_TILE, (j + 1) * COL_TILE
            acc = jnp.dot(h_ref[...], wib_ref[:, lo:hi], preferred_element_type=F32)
            if any(q0 <= lo and hi <= q1 for q0, q1 in q_cols):
                acc = acc * Q_FOLD
            proj_ref[0, :, lo:hi] = acc.astype(BF16)


def _proj(x, out_stage=None, final_g=None, in_stage=None):
    tile = ROW_TILE if in_stage is not None else OUT_ONLY_ROW_TILE
    row = pl.BlockSpec((1, tile, D_MODEL), lambda b, i: (b, i, 0))
    per_batch = pl.BlockSpec((1, 1, D_MODEL), lambda b, i: (b, 0, 0))
    vec = pl.BlockSpec((1, D_MODEL), lambda b, i: (0, 0))
    def weight(stack, j):
        return pl.BlockSpec((None,) + stack.shape[1:], lambda b, i: (j, 0, 0),
                            pipeline_mode=pl.Buffered(1))
    in_specs, args, out_specs, out_shape, scratch = [row], [x], [], [], []
    n_z = n_out = 0
    q_cols = ()
    assert final_g is None or out_stage is not None
    if out_stage is not None:
        gate, zs, w_out, j_out = out_stage
        n_z = len(zs)
        in_specs += [per_batch]
        in_specs += [pl.BlockSpec((1, tile, D_MODEL // n_z), lambda b, i: (b, i, 0))] * n_z
        in_specs += [weight(w_out, j_out)]
        args += [gate.reshape(BATCH, 1, D_MODEL), *zs, w_out]
        out_specs += [row]
        out_shape += [jax.ShapeDtypeStruct((BATCH, SEQ, D_MODEL), F32)]
        scratch += [pltpu.VMEM((D_MODEL, D_MODEL), BF16)]
    if final_g is not None:
        in_specs += [vec]
        args += [final_g.reshape(1, D_MODEL)]
    if in_stage is not None:
        g, scale, shift, w_in, j_in, q_cols = in_stage
        n_out = w_in.shape[2]
        assert all(q0 % COL_TILE == 0 and q1 % COL_TILE == 0 for q0, q1 in q_cols)
        in_specs += [vec, per_batch, per_batch, weight(w_in, j_in)]
        args += [g.reshape(1, D_MODEL), scale.reshape(BATCH, 1, D_MODEL),
                 shift.reshape(BATCH, 1, D_MODEL), w_in]
        out_specs += [pl.BlockSpec((1, tile, n_out), lambda b, i: (b, i, 0))]
        out_shape += [jax.ShapeDtypeStruct((BATCH, SEQ, n_out), BF16)]
        scratch += [pltpu.VMEM((D_MODEL, n_out), BF16), pltpu.VMEM((tile, D_MODEL), BF16)]
    outs = pl.pallas_call(
        functools.partial(_proj_kernel, n_z=n_z, final=final_g is not None, n_out=n_out,
                          q_cols=q_cols),
        grid=(BATCH, SEQ // tile),
        in_specs=in_specs,
        out_specs=out_specs,
        out_shape=out_shape,
        scratch_shapes=scratch,
        compiler_params=pltpu.CompilerParams(
            dimension_semantics=("arbitrary", "arbitrary"), vmem_limit_bytes=VMEM_LIMIT),
        name="proj",
    )(*args)
    outs = list(outs)
    x_new = outs.pop(0) if out_stage is not None else None
    proj = outs.pop(0) if in_stage is not None else None
    return x_new, proj


def _lane_lo(rows):
    return lax.broadcasted_iota(jnp.int32, (rows, LANES), 1) < HEAD_DIM


def _softmax_block(qm, kw, vw, bias):
    s = _dot_nt(qm, kw) + bias
    m = jnp.max(s, axis=-1, keepdims=True)
    p = jnp.exp2(s - m)
    l = jnp.sum(p, axis=-1, keepdims=True)
    acc = jnp.dot(p.astype(BF16), vw, preferred_element_type=F32)
    return acc, m, l


def _interleave(*stages):
    live = list(stages)
    while live:
        for g in list(live):
            try:
                next(g)
            except StopIteration:
                live.remove(g)


def _stack_heads(qb, lo):
    zero = jnp.zeros_like(qb)
    return jnp.concatenate([jnp.where(lo, qb, zero), jnp.where(lo, zero, qb)], axis=0)


A_SCRATCH = (
    [pltpu.VMEM((SEQ, LANES), F32)] * 6
    + [pltpu.VMEM((SEQ, LANES), BF16)] * 6
    + [pltpu.VMEM((3, SEQ, LANES), F32)] * 3
)


def _a_stages(q_ref, k_ref, v_ref, g_ref, b12_ref, b3_ref, o_ref,
              qf_ref, kf_ref, vf_ref, q4_ref, k4_ref, v4_ref, qd_ref, kd_ref, vd_ref,
              qe_ref, ke_ref, ve_ref, acc_ref, m_ref, l_ref):
    lo = _lane_lo(BLK)

    def pair_block(qb, kw, vw, bias):
        acc, m, l = _softmax_block(_stack_heads(qb, lo), kw, vw, bias)
        return (jnp.where(lo, acc[:BLK], acc[BLK:]), jnp.where(lo, m[:BLK], m[BLK:]),
                jnp.where(lo, l[:BLK], l[BLK:]))

    n_blk = SEQ // BLK
    dil1 = A_PATTERNS[1][1]
    seg1 = SEQ // dil1
    per1 = seg1 // BLK
    dil2 = A_PATTERNS[2][1]
    sub = dil2 // dil1
    assert sub * dil1 == dil2
    seg2 = SEQ // dil2
    streams = ((qf_ref, q4_ref, qd_ref, qe_ref), (kf_ref, k4_ref, kd_ref, ke_ref),
               (vf_ref, v4_ref, vd_ref, ve_ref))

    def prologue():
        chunk = 512
        for c in range(SEQ // chunk):
            rows = slice(c * chunk, (c + 1) * chunk)
            qf_ref[rows, :] = q_ref[0, rows, :].astype(F32)
            kf_ref[rows, :] = k_ref[0, rows, :].astype(F32)
            vf_ref[rows, :] = v_ref[0, rows, :].astype(F32)
            yield
        for r in range(dil1):
            dst = slice(r * seg1, (r + 1) * seg1)
            for src_ref, mid_ref, dst_ref, _ in streams:
                t = src_ref[pl.ds(r, seg1, stride=dil1), :]
                mid_ref[dst, :] = t
                dst_ref[dst, :] = t.astype(BF16)
            yield
        for r in range(dil1):
            for r2 in range(sub):
                res = dil1 * r2 + r
                dst = slice(res * seg2, (res + 1) * seg2)
                for _, mid_ref, _, dst_ref in streams:
                    dst_ref[dst, :] = (
                        mid_ref[pl.ds(r * seg1 + r2, seg2, stride=sub), :].astype(BF16))
            yield

    def aligned(x, m):
        return x if isinstance(x, int) else pl.multiple_of(x, m)

    def clamp(x, hi):
        return min(max(x, 0), hi) if isinstance(x, int) else jnp.clip(x, 0, hi)

    def variant(blk, last):
        if isinstance(blk, int):
            return 0 if blk == 0 else (2 if blk == last else 1)
        return jnp.where(blk == 0, 0, jnp.where(blk == last, 2, 1))

    def p0_body(j):
        r0 = aligned(j * BLK, BLK)
        ws = aligned(clamp(j * BLK - A_HALF, SEQ - A_KEYS), A_HALF)
        var = variant(j, n_blk - 1)
        acc, m, l = pair_block(q_ref[0, pl.ds(r0, BLK), :],
                               k_ref[0, pl.ds(ws, A_KEYS), :], v_ref[0, pl.ds(ws, A_KEYS), :],
                               b12_ref[0, 0, var])
        acc_ref[0, pl.ds(r0, BLK), :] = acc
        m_ref[0, pl.ds(r0, BLK), :] = m
        l_ref[0, pl.ds(r0, BLK), :] = l

    def p1_body(j):
        r = j // per1
        blk = j % per1
        r0 = aligned(j * BLK, BLK)
        ws = aligned(r * seg1 + clamp(blk * BLK - A_HALF, seg1 - A_KEYS), A_HALF)
        var = variant(blk, per1 - 1)
        acc, m, l = pair_block(qd_ref[pl.ds(r0, BLK), :],
                               kd_ref[pl.ds(ws, A_KEYS), :], vd_ref[pl.ds(ws, A_KEYS), :],
                               b12_ref[0, 1, var])
        dst = pl.ds(blk * BLK * dil1 + r, BLK, stride=dil1)
        acc_ref[1, dst, :] = acc
        m_ref[1, dst, :] = m
        l_ref[1, dst, :] = l

    def p2_body(j):
        r0 = aligned(j * BLK, BLK)
        acc, m, l = pair_block(qe_ref[pl.ds(r0, BLK), :],
                               ke_ref[pl.ds(r0, BLK), :], ve_ref[pl.ds(r0, BLK), :],
                               b3_ref[0])
        dst = pl.ds(j, BLK, stride=dil2)
        acc_ref[2, dst, :] = acc
        m_ref[2, dst, :] = m
        l_ref[2, dst, :] = l

    bodies = (p0_body, p1_body, p2_body)

    def blocks(work):
        for pattern, j in work:
            bodies[pattern](j)
            yield

    def merge(c):
        rows = pl.ds(aligned(c * A_MERGE_ROWS, A_MERGE_ROWS), A_MERGE_ROWS)
        m0, m1, m2 = m_ref[0, rows, :], m_ref[1, rows, :], m_ref[2, rows, :]
        mx = jnp.maximum(jnp.maximum(m0, m1), m2)
        w0, w1, w2 = jnp.exp2(m0 - mx), jnp.exp2(m1 - mx), jnp.exp2(m2 - mx)
        num = w0 * acc_ref[0, rows, :] + w1 * acc_ref[1, rows, :] + w2 * acc_ref[2, rows, :]
        den = w0 * l_ref[0, rows, :] + w1 * l_ref[1, rows, :] + w2 * l_ref[2, rows, :]
        y = num / den
        o_ref[0, rows, :] = (y * _silu(g_ref[0, rows, :].astype(F32))).astype(BF16)

    return prologue, blocks, merge


B_SCRATCH = (
    [pltpu.VMEM((B_NB, LANES + ONES_ROWS, SEQ), BF16)]
    + [pltpu.VMEM((2, SEQ, B_TQ), F32)] * 2
    + [pltpu.VMEM((1, 2 * B_TQ), F32)] * 2
    + [pltpu.VMEM((LANES + ONES_ROWS, 2 * B_TQ), F32)] * 2
)


def _b_stages(q_ref, k_ref, v_ref, g_ref, u_ref, lq1_ref, lk1_ref, lq2_ref, lk2_ref,
              sg_ref, o_ref, vt_ref, sa_ref, sb_ref, ma_ref, mb_ref, acca_ref, accb_ref,
              *, lam_init):
    lo = _lane_lo(B_TQ)
    n_t = SEQ // B_TK
    lam = (jnp.exp(jnp.sum(lq1_ref[...] * lk1_ref[...], axis=-1, keepdims=True))
           - jnp.exp(jnp.sum(lq2_ref[...] * lk2_ref[...], axis=-1, keepdims=True)) + lam_init)

    eye = jnp.where(lax.broadcasted_iota(jnp.int32, (LANES, LANES), 0)
                    == lax.broadcasted_iota(jnp.int32, (LANES, LANES), 1), 1.0, 0.0).astype(BF16)
    ones_row = jnp.where(lax.broadcasted_iota(jnp.int32, (ONES_ROWS, B_TK), 0) == 0,
                         1.0, 0.0).astype(BF16)
    for bi in range(B_NB):
        for t in range(n_t):
            rows = slice(t * B_TK, (t + 1) * B_TK)
            vt_ref[bi, :LANES, rows] = _dot_nt(eye, v_ref[bi, rows, :]).astype(BF16)
            vt_ref[bi, LANES:, rows] = ones_row

    n_q = SEQ // B_TQ

    def locate(i):
        if isinstance(i, int):
            return i // n_q, (i % n_q) * B_TQ
        return i // n_q, pl.multiple_of((i % n_q) * B_TQ, B_TQ)

    def scores(i, s_ref, m_ref):
        bi, r0 = locate(i)
        qst = _stack_heads(q_ref[bi, pl.ds(r0, B_TQ), :], lo)
        m = None
        for t in range(n_t):
            rows = slice(t * B_TK, (t + 1) * B_TK)
            off = pl.multiple_of(t * B_TK + (SEQ - B_TQ) - r0, LANES)
            ub = u_ref[0, pl.ds(off, B_TK), :]
            st = _dot_nt(k_ref[bi, rows, :], qst)
            s1 = st[:, :B_TQ] + ub
            s2 = st[:, B_TQ:] + ub
            s_ref[0, rows, :] = s1
            s_ref[1, rows, :] = s2
            f = jnp.concatenate([jnp.max(s1, axis=0, keepdims=True),
                                 jnp.max(s2, axis=0, keepdims=True)], axis=1)
            m = f if m is None else jnp.maximum(m, f)
            yield
        m_ref[...] = m

    def softmax_pv(i, s_ref, m_ref, acc_ref):
        bi, _ = locate(i)
        m = m_ref[...]
        acct = None
        for t in range(n_t):
            rows = slice(t * B_TK, (t + 1) * B_TK)
            p = jnp.concatenate([jnp.exp2(s_ref[0, rows, :] - m[:, :B_TQ]),
                                 jnp.exp2(s_ref[1, rows, :] - m[:, B_TQ:])], axis=1).astype(BF16)
            d = jnp.dot(vt_ref[bi, :, rows], p, preferred_element_type=F32)
            acct = d if acct is None else acct + d
            yield
        acc_ref[...] = acct

    def epilogue(i, acc_ref):
        bi, r0 = locate(i)
        ot = acc_ref[:LANES, :] * (1.0 / acc_ref[LANES:LANES + 1, :])
        outt = ot[:, :B_TQ] - lam * ot[:, B_TQ:]
        ms = jnp.mean(outt * outt, axis=0, keepdims=True)
        yt = outt * lax.rsqrt(ms + EPS) * sg_ref[...] * (1.0 - lam_init)
        yield
        gate = _silu(g_ref[bi, pl.ds(r0, B_TQ), :].astype(F32))
        o_ref[bi, pl.ds(r0, B_TQ), :] = (yt.T * gate).astype(BF16)

    return scores, softmax_pv, epilogue, (sa_ref, ma_ref, acca_ref), (sb_ref, mb_ref, accb_ref)


def _b_kernel(*refs, lam_init):
    scores, softmax_pv, epilogue, (sa, ma, acca), (sb, mb, accb) = _b_stages(
        *refs, lam_init=lam_init)
    n = B_NB * (SEQ // B_TQ)
    _interleave(scores(0, sa, ma))
    _interleave(scores(1, sb, mb), softmax_pv(0, sa, ma, acca))

    def body(j, carry):
        i = 2 * j
        _interleave(scores(i + 2, sa, ma), softmax_pv(i + 1, sb, mb, accb), epilogue(i, acca))
        _interleave(scores(i + 3, sb, mb), softmax_pv(i + 2, sa, ma, acca), epilogue(i + 1, accb))
        return carry

    lax.fori_loop(0, n // 2 - 1, body, 0)
    _interleave(softmax_pv(n - 1, sb, mb, accb), epilogue(n - 2, acca))
    _interleave(epilogue(n - 1, accb))


def _a_kernel(*refs):
    prologue, blocks, merge = _a_stages(*refs)
    _interleave(prologue())
    _interleave(blocks([(p, j) for p in range(len(A_PATTERNS)) for j in range(SEQ // BLK)]))

    def merge_body(c, carry):
        merge(c)
        return carry

    lax.fori_loop(0, SEQ // A_MERGE_ROWS, merge_body, 0)


def _mixer_ab(proj, bias12, bias3, u, lq1, lk1, lq2, lk2, subln_g, lam_init):
    pairs = A_HEADS // 2
    a_w = A_HEADS * HEAD_DIM // LANES
    b_q = 3 * a_w
    g0 = (3 * A_HEADS * HEAD_DIM + 3 * B_HEADS * 2 * HEAD_DIM) // LANES
    blk = lambda off: pl.BlockSpec((1, SEQ, LANES), lambda i, b: (b, 0, off + i))
    vec = lambda n: pl.BlockSpec((1, n), lambda i, b: (0, 0))
    out = pl.BlockSpec((1, SEQ, LANES), lambda i, b: (b, 0, i))
    params = pltpu.CompilerParams(
        dimension_semantics=("arbitrary", "arbitrary"), vmem_limit_bytes=VMEM_LIMIT)
    za = pl.pallas_call(
        _a_kernel,
        grid=(pairs, BATCH),
        in_specs=[
            blk(0), blk(a_w), blk(2 * a_w), blk(g0),
            pl.BlockSpec((1, 2, 3, 2 * BLK, A_KEYS), lambda i, b: (i, 0, 0, 0, 0)),
            pl.BlockSpec((1, 2 * BLK, BLK), lambda i, b: (i, 0, 0)),
        ],
        out_specs=out,
        out_shape=jax.ShapeDtypeStruct((BATCH, SEQ, A_HEADS * HEAD_DIM), BF16),
        scratch_shapes=A_SCRATCH,
        compiler_params=params,
        name="mixer_a_dilated",
    )(proj, proj, proj, proj, bias12, bias3)
    blk = lambda off: pl.BlockSpec((B_NB, SEQ, LANES), lambda i, b: (b, 0, off + i))
    zb = pl.pallas_call(
        functools.partial(_b_kernel, lam_init=lam_init),
        grid=(B_HEADS, BATCH // B_NB),
        in_specs=[
            blk(b_q), blk(b_q + B_HEADS), blk(b_q + 2 * B_HEADS), blk(g0 + a_w),
            pl.BlockSpec((1, 2 * SEQ - B_TQ, B_TQ), lambda i, b: (i, 0, 0)),
            vec(HEAD_DIM), vec(HEAD_DIM), vec(HEAD_DIM), vec(HEAD_DIM),
            pl.BlockSpec((2 * HEAD_DIM, 1), lambda i, b: (0, 0)),
        ],
        out_specs=blk(0),
        out_shape=jax.ShapeDtypeStruct((BATCH, SEQ, B_HEADS * 2 * HEAD_DIM), BF16),
        scratch_shapes=B_SCRATCH,
        compiler_params=params,
        name="mixer_b_differential",
    )(proj, proj, proj, proj, u, lq1.reshape(1, -1), lk1.reshape(1, -1),
      lq2.reshape(1, -1), lk2.reshape(1, -1), subln_g.reshape(-1, 1))
    return za, zb


def _c_kernel(sink_ref, q_ref, k_ref, v_ref, g_ref, bias_ref, o_ref, ka_ref, vt_ref, *bufs):
    s_refs, m_refs = bufs[:2 * C_GROUP], bufs[2 * C_GROUP:]
    grp = pl.program_id(0)
    half = grp % 2
    lo = _lane_lo(BLK)
    zero = jnp.zeros((BLK, LANES), BF16)
    row = lax.broadcasted_iota(jnp.int32, (LANES, LANES), 0)
    col = lax.broadcasted_iota(jnp.int32, (LANES, LANES), 1)

    chunk = 512
    ones_row = jnp.where(lax.broadcasted_iota(jnp.int32, (ONES_ROWS, chunk), 0) == 0,
                         1.0, 0.0).astype(BF16)
    for a in range(2):
        shift = jnp.where(half == a, 0, HEAD_DIM)
        perm = jnp.where(col == (row + shift) % LANES, 1.0, 0.0).astype(BF16)
        for bi in range(C_NB):
            for c in range(SEQ // chunk):
                rows = slice(c * chunk, (c + 1) * chunk)
                ka_ref[bi, a, rows, :] = jnp.dot(k_ref[bi, rows, :], perm,
                                                 preferred_element_type=F32).astype(BF16)
                vt_ref[bi, a, :LANES, rows] = _dot_nt(perm, v_ref[bi, rows, :]).astype(BF16)
                vt_ref[bi, a, LANES:, rows] = ones_row

    n_blk = SEQ // BLK
    rep = C_HEADS // C_KV_HEADS
    left = lax.broadcasted_iota(jnp.int32, (1, 2 * BLK), 1) < BLK
    upper = lax.broadcasted_iota(jnp.int32, (LANES, BLK), 0) < HEAD_DIM

    def locate(n):
        bi, nl = n // n_blk, n % n_blk
        r0 = pl.multiple_of(nl * BLK, BLK)
        ws = pl.multiple_of(jnp.clip(nl * BLK - C_HALF_WINDOW, 0, SEQ - C_KEYS), BLK)
        return bi, nl, r0, ws

    def scores(n, s_ref, m_ref):
        bi, nl, r0, ws = locate(n)
        var = jnp.where(nl == 0, 0, jnp.where(nl == n_blk - 1, 2, 1))
        for a in range(2):
            qa = [q_ref[bi, pl.ds(r0, BLK), c * LANES:(c + 1) * LANES] for c in range(rep // 2)]
            qm = jnp.concatenate(
                [jnp.where(lo, q, zero) if a == 0 else jnp.where(lo, zero, q) for q in qa], axis=0)
            st = _dot_nt(ka_ref[bi, a, pl.ds(ws, C_KEYS), :], qm) + bias_ref[0, a, var]
            s_ref[a] = st
            m_ref[a] = jnp.max(st, axis=0, keepdims=True)
            yield

    def finish(n, s_ref, m_ref):
        bi, _, r0, ws = locate(n)
        yts = []
        for a in range(2):
            m = m_ref[a]
            p = jnp.exp2(s_ref[a] - m).astype(BF16)
            acct = jnp.dot(vt_ref[bi, a, :, pl.ds(ws, C_KEYS)], p, preferred_element_type=F32)
            l = acct[LANES:LANES + 1]
            sk = jnp.where(left, sink_ref[grp * rep + a], sink_ref[grp * rep + a + 2]) * LOG2E
            mx = jnp.maximum(m, sk)
            e = jnp.exp2(m - mx)
            yts.append(acct[:LANES] * (e / (l * e + jnp.exp2(sk - mx))))
            yield
        for c in range(rep // 2):
            cols = slice(c * LANES, (c + 1) * LANES)
            y = jnp.where(upper, yts[0][:, cols], yts[1][:, cols]).T
            gate = _silu(g_ref[bi, pl.ds(r0, BLK), cols].astype(F32))
            o_ref[bi, pl.ds(r0, BLK), cols] = (y * gate).astype(BF16)

    bufs = [(s_refs[i], m_refs[i]) for i in range(2 * C_GROUP)]
    set_a, set_b = bufs[:C_GROUP], bufs[C_GROUP:]
    n_grp = C_NB * n_blk // C_GROUP

    def scores_of(g, bset):
        return [scores(g * C_GROUP + i, *bset[i]) for i in range(C_GROUP)]

    def finish_of(g, bset):
        return [finish(g * C_GROUP + i, *bset[i]) for i in range(C_GROUP)]

    _interleave(*scores_of(0, set_a))

    def body(j, carry):
        _interleave(*finish_of(2 * j, set_a), *scores_of(2 * j + 1, set_b))
        _interleave(*finish_of(2 * j + 1, set_b), *scores_of(2 * j + 2, set_a))
        return carry

    lax.fori_loop(0, n_grp // 2 - 1, body, 0, unroll=True)
    _interleave(*finish_of(n_grp - 2, set_a), *scores_of(n_grp - 1, set_b))
    _interleave(*finish_of(n_grp - 1, set_b))


def _mixer_c(proj, sink, bias):
    width = C_HEADS // C_KV_HEADS * HEAD_DIM
    kb = C_HEADS * HEAD_DIM // LANES
    vb = kb + C_KV_HEADS * HEAD_DIM // LANES
    gb = (C_HEADS + 2 * C_KV_HEADS) * HEAD_DIM // width
    return pl.pallas_call(
        _c_kernel,
        grid=(C_KV_HEADS, BATCH // C_NB),
        in_specs=[
            pl.BlockSpec(memory_space=pltpu.SMEM),
            pl.BlockSpec((C_NB, SEQ, width), lambda g, b: (b, 0, g)),
            pl.BlockSpec((C_NB, SEQ, LANES), lambda g, b: (b, 0, kb + g // 2)),
            pl.BlockSpec((C_NB, SEQ, LANES), lambda g, b: (b, 0, vb + g // 2)),
            pl.BlockSpec((C_NB, SEQ, width), lambda g, b: (b, 0, gb + g)),
            pl.BlockSpec((1, 2, 3, C_KEYS, 2 * BLK), lambda g, b: (g, 0, 0, 0, 0)),
        ],
        out_specs=pl.BlockSpec((C_NB, SEQ, width), lambda g, b: (b, 0, g)),
        out_shape=jax.ShapeDtypeStruct((BATCH, SEQ, C_HEADS * HEAD_DIM), BF16),
        scratch_shapes=[pltpu.VMEM((C_NB, 2, SEQ, LANES), BF16),
                        pltpu.VMEM((C_NB, 2, LANES + ONES_ROWS, SEQ), BF16)]
        + [pltpu.VMEM((2, C_KEYS, 2 * BLK), F32)] * (2 * C_GROUP)
        + [pltpu.VMEM((2, 1, 2 * BLK), F32)] * (2 * C_GROUP),
        compiler_params=pltpu.CompilerParams(
            dimension_semantics=("arbitrary", "arbitrary"), vmem_limit_bytes=VMEM_LIMIT),
        name="mixer_c_windowed",
    )(sink, proj, proj, proj, proj, bias)


@functools.lru_cache(maxsize=None)
def _bias_tables():
    def stack_pairs(t, first, second):
        return np.concatenate([t[first], t[second]], axis=2)

    sa = _alibi_slopes(A_HEADS)
    offs = (0, A_HALF, 2 * A_HALF)
    ev, od = slice(0, None, 2), slice(1, None, 2)
    b12 = np.stack(
        [stack_pairs(_band_bias(sa, dil, A_HALF, BLK, A_KEYS, offs), ev, od)
         for _, dil in A_PATTERNS[:2]], axis=1)
    b3 = stack_pairs(_band_bias(sa, A_PATTERNS[2][1], A_HALF, BLK, BLK, (0,)), ev, od)[:, 0]
    sc = _alibi_slopes(C_HEADS)
    bc_heads = _band_bias(sc, 1, C_HALF_WINDOW, BLK, C_KEYS,
                          (0, C_HALF_WINDOW, 2 * C_HALF_WINDOW))
    rep = C_HEADS // C_KV_HEADS
    bc = np.stack([stack_pairs(bc_heads, slice(a, None, rep), slice(a + 2, None, rep))
                   for a in range(2)], axis=1)
    bc = np.ascontiguousarray(np.swapaxes(bc, -1, -2))
    sb = _alibi_slopes(B_HEADS)
    il = np.arange(B_TQ)[None, :]
    uu = np.arange(2 * SEQ - B_TQ)[:, None]
    dist = np.abs(il - uu + (SEQ - B_TQ)).astype(np.float32)
    ub = -sb[:, None, None] * dist[None] * np.float32(LOG2E)
    return b12, b3, bc, ub.astype(np.float32)


def kernel(x, c, ada_w, ada_b, norm_g, ab_w_in, ab_w_out, diff_lq1, diff_lk1, diff_lq2,
           diff_lk2, diff_subln_g, c_w_in, c_w_out, c_sink, final_g):
    b12, b3, bc, ub = (jnp.asarray(t) for t in _bias_tables())
    mod = _modulation(c, ada_w, ada_b)
    a_w = A_HEADS * HEAD_DIM
    q_even = ((0, a_w), (3 * a_w, 3 * a_w + B_HEADS * 2 * HEAD_DIM))
    q_odd = ((0, C_HEADS * HEAD_DIM),)

    def in_stage(layer):
        shift = mod[layer, :, :D_MODEL]
        scale = mod[layer, :, D_MODEL:2 * D_MODEL]
        w, q_cols = (ab_w_in, q_even) if layer % 2 == 0 else (c_w_in, q_odd)
        return norm_g[layer], scale, shift, w, layer // 2, q_cols

    _, proj = _proj(x, in_stage=in_stage(0))
    for layer in range(DEPTH):
        gate = mod[layer, :, 2 * D_MODEL:]
        j = layer // 2
        last = layer == DEPTH - 1
        if layer % 2 == 0:
            lam_init = 0.8 - 0.6 * math.exp(-0.3 * layer)
            za, zb = _mixer_ab(proj, b12, b3, ub, diff_lq1[j], diff_lk1[j], diff_lq2[j],
                               diff_lk2[j], diff_subln_g[j], lam_init)
            zs, w_out = [za, zb], ab_w_out
        else:
            zs, w_out = [_mixer_c(proj, c_sink[j], bc)], c_w_out
        x, proj = _proj(x, out_stage=(gate, zs, w_out, j),
                        final_g=final_g if last else None,
                        in_stage=None if last else in_stage(layer + 1))
    return x
```

```python
import functools
import math

import numpy as np
import jax
import jax.numpy as jnp
from jax import lax
from jax.experimental import pallas as pl
from jax.experimental.pallas import tpu as pltpu

D_MODEL = 1024
BATCH = 8
SEQ = 2048
DEPTH = 4
HEAD_DIM = 64
LANES = 128
A_HEADS = 8
A_PATTERNS = ((128, 1), (512, 4), (2048, 16))
A_HALF = 64
B_HEADS = 4
C_HEADS = 16
C_KV_HEADS = 4
C_HALF_WINDOW = 128
EVEN_IN = 4096
ODD_IN = 2560
EPS = 1e-6
NEG_INF = -1e30
LOG2E = math.log2(math.e)
Q_FOLD = HEAD_DIM ** -0.5 * LOG2E
ONES_ROWS = 16

F32 = jnp.float32
BF16 = jnp.bfloat16

ROW_TILE = 512
OUT_ONLY_ROW_TILE = 1024
COL_TILE = 512
W_CAST_ROWS = 128
BLK = 128
A_KEYS = BLK + 2 * A_HALF
C_KEYS = BLK + 2 * C_HALF_WINDOW
A_MERGE_ROWS = 256
C_GROUP = 2
C_NB = 1
B_TQ = 512
B_TK = 256
B_NB = 2
VMEM_LIMIT = 56 * 1024 * 1024


def _silu(t):
    return t * (1.0 / (1.0 + jnp.exp(-t)))


def _dot_nt(a, b):
    return lax.dot_general(a, b, (((1,), (1,)), ((), ())), preferred_element_type=F32)


def _alibi_slopes(n):
    return (2.0 ** (-8.0 * np.arange(1, n + 1, dtype=np.float32) / n)).astype(np.float32)


def _band_bias(slopes, spacing, half, tq, tk, offsets):
    i = np.arange(tq)[:, None]
    c = np.arange(tk)[None, :]
    out = np.empty((len(slopes), len(offsets), tq, tk), np.float32)
    for v, off in enumerate(offsets):
        rel = np.abs(c - (i + off))
        dist = (rel * spacing).astype(np.float32)
        for h, m in enumerate(slopes):
            out[h, v] = np.where(rel <= half, -m * dist * np.float32(LOG2E), np.float32(NEG_INF))
    return out


def _split_bf16(a):
    hi = a.astype(BF16)
    return hi, (a - hi.astype(F32)).astype(BF16)


def _mod_kernel(c_ref, w_ref, b_ref, o_ref):
    c_hi, c_lo = _split_bf16(_silu(c_ref[...]))
    w_hi, w_lo = _split_bf16(w_ref[0])
    dot = functools.partial(jnp.dot, preferred_element_type=F32)
    o_ref[0] = dot(c_hi, w_hi) + (dot(c_hi, w_lo) + dot(c_lo, w_hi)) + b_ref[0]


def _modulation(c, ada_w, ada_b):
    nblk = 3 * D_MODEL // D_MODEL
    return pl.pallas_call(
        _mod_kernel,
        grid=(DEPTH, nblk),
        in_specs=[
            pl.BlockSpec((BATCH, D_MODEL), lambda l, j: (0, 0)),
            pl.BlockSpec((1, D_MODEL, D_MODEL), lambda l, j: (l, 0, j)),
            pl.BlockSpec((1, 1, D_MODEL), lambda l, j: (l, 0, j)),
        ],
        out_specs=pl.BlockSpec((1, BATCH, D_MODEL), lambda l, j: (l, 0, j)),
        out_shape=jax.ShapeDtypeStruct((DEPTH, BATCH, 3 * D_MODEL), F32),
        name="adaln_mod",
    )(c, ada_w, ada_b.reshape(DEPTH, 1, 3 * D_MODEL))


def _rms(x, g):
    ms = jnp.mean(x * x, axis=-1, keepdims=True)
    return x * lax.rsqrt(ms + EPS) * g


def _proj_kernel(*refs, n_z, final, n_out, q_cols):
    refs = iter(refs)
    x_ref = next(refs)
    if n_z:
        gate_ref = next(refs)
        z_refs = [next(refs) for _ in range(n_z)]
        wo_ref = next(refs)
    if final:
        fg_ref = next(refs)
    if n_out:
        g_ref, sc_ref, sh_ref, wi_ref = (next(refs) for _ in range(4))
    if n_z:
        xo_ref = next(refs)
    if n_out:
        proj_ref = next(refs)
    if n_z:
        wob_ref = next(refs)
    if n_out:
        wib_ref, h_ref = next(refs), next(refs)

    @pl.when((pl.program_id(0) == 0) & (pl.program_id(1) == 0))
    def _cast_weights():
        for src_ref, dst_ref in ([(wo_ref, wob_ref)] if n_z else []) + (
                [(wi_ref, wib_ref)] if n_out else []):
            for r in range(0, D_MODEL, W_CAST_ROWS):
                dst_ref[r:r + W_CAST_ROWS, :] = src_ref[r:r + W_CAST_ROWS, :].astype(BF16)

    xn = x_ref[0]
    if n_z:
        kz = D_MODEL // n_z
        acc = jnp.dot(z_refs[0][0], wob_ref[0:kz, :], preferred_element_type=F32)
        for i in range(1, n_z):
            acc = acc + jnp.dot(z_refs[i][0], wob_ref[i * kz:(i + 1) * kz, :],
                                preferred_element_type=F32)
        xn = xn + gate_ref[0] * acc
        xo_ref[0] = _rms(xn, fg_ref[...]) if final else xn
    if n_out:
        h_ref[...] = (_rms(xn, g_ref[...]) * (1.0 + sc_ref[0]) + sh_ref[0]).astype(BF16)
        for j in range(n_out // COL_TILE):
            lo, hi = j * COL_TILE, (j + 1) * COL_TILE
            acc = jnp.dot(h_ref[...], wib_ref[:, lo:hi], preferred_element_type=F32)
            if any(q0 <= lo and hi <= q1 for q0, q1 in q_cols):
                acc = acc * Q_FOLD
            proj_ref[0, :, lo:hi] = acc.astype(BF16)


def _proj(x, out_stage=None, final_g=None, in_stage=None):
    tile = ROW_TILE if in_stage is not None else OUT_ONLY_ROW_TILE
    row = pl.BlockSpec((1, tile, D_MODEL), lambda b, i: (b, i, 0))
    per_batch = pl.BlockSpec((1, 1, D_MODEL), lambda b, i: (b, 0, 0))
    vec = pl.BlockSpec((1, D_MODEL), lambda b, i: (0, 0))
    def weight(stack, j):
        return pl.BlockSpec((None,) + stack.shape[1:], lambda b, i: (j, 0, 0),
                            pipeline_mode=pl.Buffered(1))
    in_specs, args, out_specs, out_shape, scratch = [row], [x], [], [], []
    n_z = n_out = 0
    q_cols = ()
    assert final_g is None or out_stage is not None
    if out_stage is not None:
        gate, zs, w_out, j_out = out_stage
        n_z = len(zs)
        in_specs += [per_batch]
        in_specs += [pl.BlockSpec((1, tile, D_MODEL // n_z), lambda b, i: (b, i, 0))] * n_z
        in_specs += [weight(w_out, j_out)]
        args += [gate.reshape(BATCH, 1, D_MODEL), *zs, w_out]
        out_specs += [row]
        out_shape += [jax.ShapeDtypeStruct((BATCH, SEQ, D_MODEL), F32)]
        scratch += [pltpu.VMEM((D_MODEL, D_MODEL), BF16)]
    if final_g is not None:
        in_specs += [vec]
        args += [final_g.reshape(1, D_MODEL)]
    if in_stage is not None:
        g, scale, shift, w_in, j_in, q_cols = in_stage
        n_out = w_in.shape[2]
        assert all(q0 % COL_TILE == 0 and q1 % COL_TILE == 0 for q0, q1 in q_cols)
        in_specs += [vec, per_batch, per_batch, weight(w_in, j_in)]
        args += [g.reshape(1, D_MODEL), scale.reshape(BATCH, 1, D_MODEL),
                 shift.reshape(BATCH, 1, D_MODEL), w_in]
        out_specs += [pl.BlockSpec((1, tile, n_out), lambda b, i: (b, i, 0))]
        out_shape += [jax.ShapeDtypeStruct((BATCH, SEQ, n_out), BF16)]
        scratch += [pltpu.VMEM((D_MODEL, n_out), BF16), pltpu.VMEM((tile, D_MODEL), BF16)]
    outs = pl.pallas_call(
        functools.partial(_proj_kernel, n_z=n_z, final=final_g is not None, n_out=n_out,
                          q_cols=q_cols),
        grid=(BATCH, SEQ // tile),
        in_specs=in_specs,
        out_specs=out_specs,
        out_shape=out_shape,
        scratch_shapes=scratch,
        compiler_params=pltpu.CompilerParams(
            dimension_semantics=("arbitrary", "arbitrary"), vmem_limit_bytes=VMEM_LIMIT),
        name="proj",
    )(*args)
    outs = list(outs)
    x_new = outs.pop(0) if out_stage is not None else None
    proj = outs.pop(0) if in_stage is not None else None
    return x_new, proj


def _lane_lo(rows):
    return lax.broadcasted_iota(jnp.int32, (rows, LANES), 1) < HEAD_DIM


def _softmax_block(qm, kw, vw, bias):
    s = _dot_nt(qm, kw) + bias
    m = jnp.max(s, axis=-1, keepdims=True)
    p = jnp.exp2(s - m)
    l = jnp.sum(p, axis=-1, keepdims=True)
    acc = jnp.dot(p.astype(BF16), vw, preferred_element_type=F32)
    return acc, m, l


def _interleave(*stages):
    live = list(stages)
    while live:
        for g in list(live):
            try:
                next(g)
            except StopIteration:
                live.remove(g)


def _stack_heads(qb, lo):
    zero = jnp.zeros_like(qb)
    return jnp.concatenate([jnp.where(lo, qb, zero), jnp.where(lo, zero, qb)], axis=0)


A_SCRATCH = (
    [pltpu.VMEM((SEQ, LANES), F32)] * 6
    + [pltpu.VMEM((SEQ, LANES), BF16)] * 6
    + [pltpu.VMEM((3, SEQ, LANES), F32)] * 3
)


def _a_stages(q_ref, k_ref, v_ref, g_ref, b12_ref, b3_ref, o_ref,
              qf_ref, kf_ref, vf_ref, q4_ref, k4_ref, v4_ref, qd_ref, kd_ref, vd_ref,
              qe_ref, ke_ref, ve_ref, acc_ref, m_ref, l_ref):
    lo = _lane_lo(BLK)

    def pair_block(qb, kw, vw, bias):
        acc, m, l = _softmax_block(_stack_heads(qb, lo), kw, vw, bias)
        return (jnp.where(lo, acc[:BLK], acc[BLK:]), jnp.where(lo, m[:BLK], m[BLK:]),
                jnp.where(lo, l[:BLK], l[BLK:]))

    n_blk = SEQ // BLK
    dil1 = A_PATTERNS[1][1]
    seg1 = SEQ // dil1
    per1 = seg1 // BLK
    dil2 = A_PATTERNS[2][1]
    sub = dil2 // dil1
    assert sub * dil1 == dil2
    seg2 = SEQ // dil2
    streams = ((qf_ref, q4_ref, qd_ref, qe_ref), (kf_ref, k4_ref, kd_ref, ke_ref),
               (vf_ref, v4_ref, vd_ref, ve_ref))

    def prologue():
        chunk = 512
        for c in range(SEQ // chunk):
            rows = slice(c * chunk, (c + 1) * chunk)
            qf_ref[rows, :] = q_ref[0, rows, :].astype(F32)
            kf_ref[rows, :] = k_ref[0, rows, :].astype(F32)
            vf_ref[rows, :] = v_ref[0, rows, :].astype(F32)
            yield
        for r in range(dil1):
            dst = slice(r * seg1, (r + 1) * seg1)
            for src_ref, mid_ref, dst_ref, _ in streams:
                t = src_ref[pl.ds(r, seg1, stride=dil1), :]
                mid_ref[dst, :] = t
                dst_ref[dst, :] = t.astype(BF16)
            yield
        for r in range(dil1):
            for r2 in range(sub):
                res = dil1 * r2 + r
                dst = slice(res * seg2, (res + 1) * seg2)
                for _, mid_ref, _, dst_ref in streams:
                    dst_ref[dst, :] = (
                        mid_ref[pl.ds(r * seg1 + r2, seg2, stride=sub), :].astype(BF16))
            yield

    def aligned(x, m):
        return x if isinstance(x, int) else pl.multiple_of(x, m)

    def clamp(x, hi):
        return min(max(x, 0), hi) if isinstance(x, int) else jnp.clip(x, 0, hi)

    def variant(blk, last):
        if isinstance(blk, int):
            return 0 if blk == 0 else (2 if blk == last else 1)
        return jnp.where(blk == 0, 0, jnp.where(blk == last, 2, 1))

    def p0_body(j):
        r0 = aligned(j * BLK, BLK)
        ws = aligned(clamp(j * BLK - A_HALF, SEQ - A_KEYS), A_HALF)
        var = variant(j, n_blk - 1)
        acc, m, l = pair_block(q_ref[0, pl.ds(r0, BLK), :],
                               k_ref[0, pl.ds(ws, A_KEYS), :], v_ref[0, pl.ds(ws, A_KEYS), :],
                               b12_ref[0, 0, var])
        acc_ref[0, pl.ds(r0, BLK), :] = acc
        m_ref[0, pl.ds(r0, BLK), :] = m
        l_ref[0, pl.ds(r0, BLK), :] = l

    def p1_body(j):
        r = j // per1
        blk = j % per1
        r0 = aligned(j * BLK, BLK)
        ws = aligned(r * seg1 + clamp(blk * BLK - A_HALF, seg1 - A_KEYS), A_HALF)
        var = variant(blk, per1 - 1)
        acc, m, l = pair_block(qd_ref[pl.ds(r0, BLK), :],
                               kd_ref[pl.ds(ws, A_KEYS), :], vd_ref[pl.ds(ws, A_KEYS), :],
                               b12_ref[0, 1, var])
        dst = pl.ds(blk * BLK * dil1 + r, BLK, stride=dil1)
        acc_ref[1, dst, :] = acc
        m_ref[1, dst, :] = m
        l_ref[1, dst, :] = l

    def p2_body(j):
        r0 = aligned(j * BLK, BLK)
        acc, m, l = pair_block(qe_ref[pl.ds(r0, BLK), :],
                               ke_ref[pl.ds(r0, BLK), :], ve_ref[pl.ds(r0, BLK), :],
                               b3_ref[0])
        dst = pl.ds(j, BLK, stride=dil2)
        acc_ref[2, dst, :] = acc
        m_ref[2, dst, :] = m
        l_ref[2, dst, :] = l

    bodies = (p0_body, p1_body, p2_body)

    def blocks(work):
        for pattern, j in work:
            bodies[pattern](j)
            yield

    def merge(c):
        rows = pl.ds(aligned(c * A_MERGE_ROWS, A_MERGE_ROWS), A_MERGE_ROWS)
        m0, m1, m2 = m_ref[0, rows, :], m_ref[1, rows, :], m_ref[2, rows, :]
        mx = jnp.maximum(jnp.maximum(m0, m1), m2)
        w0, w1, w2 = jnp.exp2(m0 - mx), jnp.exp2(m1 - mx), jnp.exp2(m2 - mx)
        num = w0 * acc_ref[0, rows, :] + w1 * acc_ref[1, rows, :] + w2 * acc_ref[2, rows, :]
        den = w0 * l_ref[0, rows, :] + w1 * l_ref[1, rows, :] + w2 * l_ref[2, rows, :]
        y = num / den
        o_ref[0, rows, :] = (y * _silu(g_ref[0, rows, :].astype(F32))).astype(BF16)

    return prologue, blocks, merge


B_SCRATCH = (
    [pltpu.VMEM((B_NB, LANES + ONES_ROWS, SEQ), BF16)]
    + [pltpu.VMEM((2, SEQ, B_TQ), F32)] * 2
    + [pltpu.VMEM((1, 2 * B_TQ), F32)] * 2
    + [pltpu.VMEM((LANES + ONES_ROWS, 2 * B_TQ), F32)] * 2
)


def _b_stages(q_ref, k_ref, v_ref, g_ref, u_ref, lq1_ref, lk1_ref, lq2_ref, lk2_ref,
              sg_ref, o_ref, vt_ref, sa_ref, sb_ref, ma_ref, mb_ref, acca_ref, accb_ref,
              *, lam_init):
    lo = _lane_lo(B_TQ)
    n_t = SEQ // B_TK
    lam = (jnp.exp(jnp.sum(lq1_ref[...] * lk1_ref[...], axis=-1, keepdims=True))
           - jnp.exp(jnp.sum(lq2_ref[...] * lk2_ref[...], axis=-1, keepdims=True)) + lam_init)

    eye = jnp.where(lax.broadcasted_iota(jnp.int32, (LANES, LANES), 0)
                    == lax.broadcasted_iota(jnp.int32, (LANES, LANES), 1), 1.0, 0.0).astype(BF16)
    ones_row = jnp.where(lax.broadcasted_iota(jnp.int32, (ONES_ROWS, B_TK), 0) == 0,
                         1.0, 0.0).astype(BF16)
    for bi in range(B_NB):
        for t in range(n_t):
            rows = slice(t * B_TK, (t + 1) * B_TK)
            vt_ref[bi, :LANES, rows] = _dot_nt(eye, v_ref[bi, rows, :]).astype(BF16)
            vt_ref[bi, LANES:, rows] = ones_row

    n_q = SEQ // B_TQ

    def locate(i):
        if isinstance(i, int):
            return i // n_q, (i % n_q) * B_TQ
        return i // n_q, pl.multiple_of((i % n_q) * B_TQ, B_TQ)

    def scores(i, s_ref, m_ref):
        bi, r0 = locate(i)
        qst = _stack_heads(q_ref[bi, pl.ds(r0, B_TQ), :], lo)
        m = None
        for t in range(n_t):
            rows = slice(t * B_TK, (t + 1) * B_TK)
            off = pl.multiple_of(t * B_TK + (SEQ - B_TQ) - r0, LANES)
            ub = u_ref[0, pl.ds(off, B_TK), :]
            st = _dot_nt(k_ref[bi, rows, :], qst)
            s1 = st[:, :B_TQ] + ub
            s2 = st[:, B_TQ:] + ub
            s_ref[0, rows, :] = s1
            s_ref[1, rows, :] = s2
            f = jnp.concatenate([jnp.max(s1, axis=0, keepdims=True),
                                 jnp.max(s2, axis=0, keepdims=True)], axis=1)
            m = f if m is None else jnp.maximum(m, f)
            yield
        m_ref[...] = m

    def softmax_pv(i, s_ref, m_ref, acc_ref):
        bi, _ = locate(i)
        m = m_ref[...]
        acct = None
        for t in range(n_t):
            rows = slice(t * B_TK, (t + 1) * B_TK)
            p = jnp.concatenate([jnp.exp2(s_ref[0, rows, :] - m[:, :B_TQ]),
                                 jnp.exp2(s_ref[1, rows, :] - m[:, B_TQ:])], axis=1).astype(BF16)
            d = jnp.dot(vt_ref[bi, :, rows], p, preferred_element_type=F32)
            acct = d if acct is None else acct + d
            yield
        acc_ref[...] = acct

    def epilogue(i, acc_ref):
        bi, r0 = locate(i)
        ot = acc_ref[:LANES, :] * (1.0 / acc_ref[LANES:LANES + 1, :])
        outt = ot[:, :B_TQ] - lam * ot[:, B_TQ:]
        ms = jnp.mean(outt * outt, axis=0, keepdims=True)
        yt = outt * lax.rsqrt(ms + EPS) * sg_ref[...] * (1.0 - lam_init)
        yield
        gate = _silu(g_ref[bi, pl.ds(r0, B_TQ), :].astype(F32))
        o_ref[bi, pl.ds(r0, B_TQ), :] = (yt.T * gate).astype(BF16)

    return scores, softmax_pv, epilogue, (sa_ref, ma_ref, acca_ref), (sb_ref, mb_ref, accb_ref)


def _b_kernel(*refs, lam_init):
    scores, softmax_pv, epilogue, (sa, ma, acca), (sb, mb, accb) = _b_stages(
        *refs, lam_init=lam_init)
    n = B_NB * (SEQ // B_TQ)
    _interleave(scores(0, sa, ma))
    _interleave(scores(1, sb, mb), softmax_pv(0, sa, ma, acca))

    def body(j, carry):
        i = 2 * j
        _interleave(scores(i + 2, sa, ma), softmax_pv(i + 1, sb, mb, accb), epilogue(i, acca))
        _interleave(scores(i + 3, sb, mb), softmax_pv(i + 2, sa, ma, acca), epilogue(i + 1, accb))
        return carry

    lax.fori_loop(0, n // 2 - 1, body, 0)
    _interleave(softmax_pv(n - 1, sb, mb, accb), epilogue(n - 2, acca))
    _interleave(epilogue(n - 1, accb))


def _a_kernel(*refs):
    prologue, blocks, merge = _a_stages(*refs)
    _interleave(prologue())
    _interleave(blocks([(p, j) for p in range(len(A_PATTERNS)) for j in range(SEQ // BLK)]))

    def merge_body(c, carry):
        merge(c)
        return carry

    lax.fori_loop(0, SEQ // A_MERGE_ROWS, merge_body, 0)


def _mixer_ab(proj, bias12, bias3, u, lq1, lk1, lq2, lk2, subln_g, lam_init):
    pairs = A_HEADS // 2
    a_w = A_HEADS * HEAD_DIM // LANES
    b_q = 3 * a_w
    g0 = (3 * A_HEADS * HEAD_DIM + 3 * B_HEADS * 2 * HEAD_DIM) // LANES
    blk = lambda off: pl.BlockSpec((1, SEQ, LANES), lambda i, b: (b, 0, off + i))
    vec = lambda n: pl.BlockSpec((1, n), lambda i, b: (0, 0))
    out = pl.BlockSpec((1, SEQ, LANES), lambda i, b: (b, 0, i))
    params = pltpu.CompilerParams(
        dimension_semantics=("arbitrary", "arbitrary"), vmem_limit_bytes=VMEM_LIMIT)
    za = pl.pallas_call(
        _a_kernel,
        grid=(pairs, BATCH),
        in_specs=[
            blk(0), blk(a_w), blk(2 * a_w), blk(g0),
            pl.BlockSpec((1, 2, 3, 2 * BLK, A_KEYS), lambda i, b: (i, 0, 0, 0, 0)),
            pl.BlockSpec((1, 2 * BLK, BLK), lambda i, b: (i, 0, 0)),
        ],
        out_specs=out,
        out_shape=jax.ShapeDtypeStruct((BATCH, SEQ, A_HEADS * HEAD_DIM), BF16),
        scratch_shapes=A_SCRATCH,
        compiler_params=params,
        name="mixer_a_dilated",
    )(proj, proj, proj, proj, bias12, bias3)
    blk = lambda off: pl.BlockSpec((B_NB, SEQ, LANES), lambda i, b: (b, 0, off + i))
    zb = pl.pallas_call(
        functools.partial(_b_kernel, lam_init=lam_init),
        grid=(B_HEADS, BATCH // B_NB),
        in_specs=[
            blk(b_q), blk(b_q + B_HEADS), blk(b_q + 2 * B_HEADS), blk(g0 + a_w),
            pl.BlockSpec((1, 2 * SEQ - B_TQ, B_TQ), lambda i, b: (i, 0, 0)),
            vec(HEAD_DIM), vec(HEAD_DIM), vec(HEAD_DIM), vec(HEAD_DIM),
            pl.BlockSpec((2 * HEAD_DIM, 1), lambda i, b: (0, 0)),
        ],
        out_specs=blk(0),
        out_shape=jax.ShapeDtypeStruct((BATCH, SEQ, B_HEADS * 2 * HEAD_DIM), BF16),
        scratch_shapes=B_SCRATCH,
        compiler_params=params,
        name="mixer_b_differential",
    )(proj, proj, proj, proj, u, lq1.reshape(1, -1), lk1.reshape(1, -1),
      lq2.reshape(1, -1), lk2.reshape(1, -1), subln_g.reshape(-1, 1))
    return za, zb


def _c_kernel(sink_ref, q_ref, k_ref, v_ref, g_ref, bias_ref, o_ref, ka_ref, vt_ref, *bufs):
    s_refs, m_refs = bufs[:2 * C_GROUP], bufs[2 * C_GROUP:]
    grp = pl.program_id(0)
    half = grp % 2
    lo = _lane_lo(BLK)
    zero = jnp.zeros((BLK, LANES), BF16)
    row = lax.broadcasted_iota(jnp.int32, (LANES, LANES), 0)
    col = lax.broadcasted_iota(jnp.int32, (LANES, LANES), 1)

    chunk = 512
    ones_row = jnp.where(lax.broadcasted_iota(jnp.int32, (ONES_ROWS, chunk), 0) == 0,
                         1.0, 0.0).astype(BF16)
    for a in range(2):
        shift = jnp.where(half == a, 0, HEAD_DIM)
        perm = jnp.where(col == (row + shift) % LANES, 1.0, 0.0).astype(BF16)
        for bi in range(C_NB):
            for c in range(SEQ // chunk):
                rows = slice(c * chunk, (c + 1) * chunk)
                ka_ref[bi, a, rows, :] = jnp.dot(k_ref[bi, rows, :], perm,
                                                 preferred_element_type=F32).astype(BF16)
                vt_ref[bi, a, :LANES, rows] = _dot_nt(perm, v_ref[bi, rows, :]).astype(BF16)
                vt_ref[bi, a, LANES:, rows] = ones_row

    n_blk = SEQ // BLK
    rep = C_HEADS // C_KV_HEADS
    left = lax.broadcasted_iota(jnp.int32, (1, 2 * BLK), 1) < BLK
    upper = lax.broadcasted_iota(jnp.int32, (LANES, BLK), 0) < HEAD_DIM

    def locate(n):
        bi, nl = n // n_blk, n % n_blk
        r0 = pl.multiple_of(nl * BLK, BLK)
        ws = pl.multiple_of(jnp.clip(nl * BLK - C_HALF_WINDOW, 0, SEQ - C_KEYS), BLK)
        return bi, nl, r0, ws

    def scores(n, s_ref, m_ref):
        bi, nl, r0, ws = locate(n)
        var = jnp.where(nl == 0, 0, jnp.where(nl == n_blk - 1, 2, 1))
        for a in range(2):
            qa = [q_ref[bi, pl.ds(r0, BLK), c * LANES:(c + 1) * LANES] for c in range(rep // 2)]
            qm = jnp.concatenate(
                [jnp.where(lo, q, zero) if a == 0 else jnp.where(lo, zero, q) for q in qa], axis=0)
            st = _dot_nt(ka_ref[bi, a, pl.ds(ws, C_KEYS), :], qm) + bias_ref[0, a, var]
            s_ref[a] = st
            m_ref[a] = jnp.max(st, axis=0, keepdims=True)
            yield

    def finish(n, s_ref, m_ref):
        bi, _, r0, ws = locate(n)
        yts = []
        for a in range(2):
            m = m_ref[a]
            p = jnp.exp2(s_ref[a] - m).astype(BF16)
            acct = jnp.dot(vt_ref[bi, a, :, pl.ds(ws, C_KEYS)], p, preferred_element_type=F32)
            l = acct[LANES:LANES + 1]
            sk = jnp.where(left, sink_ref[grp * rep + a], sink_ref[grp * rep + a + 2]) * LOG2E
            mx = jnp.maximum(m, sk)
            e = jnp.exp2(m - mx)
            yts.append(acct[:LANES] * (e / (l * e + jnp.exp2(sk - mx))))
            yield
        for c in range(rep // 2):
            cols = slice(c * LANES, (c + 1) * LANES)
            y = jnp.where(upper, yts[0][:, cols], yts[1][:, cols]).T
            gate = _silu(g_ref[bi, pl.ds(r0, BLK), cols].astype(F32))
            o_ref[bi, pl.ds(r0, BLK), cols] = (y * gate).astype(BF16)

    bufs = [(s_refs[i], m_refs[i]) for i in range(2 * C_GROUP)]
    set_a, set_b = bufs[:C_GROUP], bufs[C_GROUP:]
    n_grp = C_NB * n_blk // C_GROUP

    def scores_of(g, bset):
        return [scores(g * C_GROUP + i, *bset[i]) for i in range(C_GROUP)]

    def finish_of(g, bset):
        return [finish(g * C_GROUP + i, *bset[i]) for i in range(C_GROUP)]

    _interleave(*scores_of(0, set_a))

    def body(j, carry):
        _interleave(*finish_of(2 * j, set_a), *scores_of(2 * j + 1, set_b))
        _interleave(*finish_of(2 * j + 1, set_b), *scores_of(2 * j + 2, set_a))
        return carry

    lax.fori_loop(0, n_grp // 2 - 1, body, 0, unroll=True)
    _interleave(*finish_of(n_grp - 2, set_a), *scores_of(n_grp - 1, set_b))
    _interleave(*finish_of(n_grp - 1, set_b))


def _mixer_c(proj, sink, bias):
    width = C_HEADS // C_KV_HEADS * HEAD_DIM
    kb = C_HEADS * HEAD_DIM // LANES
    vb = kb + C_KV_HEADS * HEAD_DIM // LANES
    gb = (C_HEADS + 2 * C_KV_HEADS) * HEAD_DIM // width
    return pl.pallas_call(
        _c_kernel,
        grid=(C_KV_HEADS, BATCH // C_NB),
        in_specs=[
            pl.BlockSpec(memory_space=pltpu.SMEM),
            pl.BlockSpec((C_NB, SEQ, width), lambda g, b: (b, 0, g)),
            pl.BlockSpec((C_NB, SEQ, LANES), lambda g, b: (b, 0, kb + g // 2)),
            pl.BlockSpec((C_NB, SEQ, LANES), lambda g, b: (b, 0, vb + g // 2)),
            pl.BlockSpec((C_NB, SEQ, width), lambda g, b: (b, 0, gb + g)),
            pl.BlockSpec((1, 2, 3, C_KEYS, 2 * BLK), lambda g, b: (g, 0, 0, 0, 0)),
        ],
        out_specs=pl.BlockSpec((C_NB, SEQ, width), lambda g, b: (b, 0, g)),
        out_shape=jax.ShapeDtypeStruct((BATCH, SEQ, C_HEADS * HEAD_DIM), BF16),
        scratch_shapes=[pltpu.VMEM((C_NB, 2, SEQ, LANES), BF16),
                        pltpu.VMEM((C_NB, 2, LANES + ONES_ROWS, SEQ), BF16)]
        + [pltpu.VMEM((2, C_KEYS, 2 * BLK), F32)] * (2 * C_GROUP)
        + [pltpu.VMEM((2, 1, 2 * BLK), F32)] * (2 * C_GROUP),
        compiler_params=pltpu.CompilerParams(
            dimension_semantics=("arbitrary", "arbitrary"), vmem_limit_bytes=VMEM_LIMIT),
        name="mixer_c_windowed",
    )(sink, proj, proj, proj, proj, bias)


@functools.lru_cache(maxsize=None)
def _bias_tables():
    def stack_pairs(t, first, second):
        return np.concatenate([t[first], t[second]], axis=2)

    sa = _alibi_slopes(A_HEADS)
    offs = (0, A_HALF, 2 * A_HALF)
    ev, od = slice(0, None, 2), slice(1, None, 2)
    b12 = np.stack(
        [stack_pairs(_band_bias(sa, dil, A_HALF, BLK, A_KEYS, offs), ev, od)
         for _, dil in A_PATTERNS[:2]], axis=1)
    b3 = stack_pairs(_band_bias(sa, A_PATTERNS[2][1], A_HALF, BLK, BLK, (0,)), ev, od)[:, 0]
    sc = _alibi_slopes(C_HEADS)
    bc_heads = _band_bias(sc, 1, C_HALF_WINDOW, BLK, C_KEYS,
                          (0, C_HALF_WINDOW, 2 * C_HALF_WINDOW))
    rep = C_HEADS // C_KV_HEADS
    bc = np.stack([stack_pairs(bc_heads, slice(a, None, rep), slice(a + 2, None, rep))
                   for a in range(2)], axis=1)
    bc = np.ascontiguousarray(np.swapaxes(bc, -1, -2))
    sb = _alibi_slopes(B_HEADS)
    il = np.arange(B_TQ)[None, :]
    uu = np.arange(2 * SEQ - B_TQ)[:, None]
    dist = np.abs(il - uu + (SEQ - B_TQ)).astype(np.float32)
    ub = -sb[:, None, None] * dist[None] * np.float32(LOG2E)
    return b12, b3, bc, ub.astype(np.float32)


def kernel(x, c, ada_w, ada_b, norm_g, ab_w_in, ab_w_out, diff_lq1, diff_lk1, diff_lq2,
           diff_lk2, diff_subln_g, c_w_in, c_w_out, c_sink, final_g):
    b12, b3, bc, ub = (jnp.asarray(t) for t in _bias_tables())
    mod = _modulation(c, ada_w, ada_b)
    a_w = A_HEADS * HEAD_DIM
    q_even = ((0, a_w), (3 * a_w, 3 * a_w + B_HEADS * 2 * HEAD_DIM))
    q_odd = ((0, C_HEADS * HEAD_DIM),)

    def in_stage(layer):
        shift = mod[layer, :, :D_MODEL]
        scale = mod[layer, :, D_MODEL:2 * D_MODEL]
        w, q_cols = (ab_w_in, q_even) if layer % 2 == 0 else (c_w_in, q_odd)
        return norm_g[layer], scale, shift, w, layer // 2, q_cols

    _, proj = _proj(x, in_stage=in_stage(0))
    for layer in range(DEPTH):
        gate = mod[layer, :, 2 * D_MODEL:]
        j = layer // 2
        last = layer == DEPTH - 1
        if layer % 2 == 0:
            lam_init = 0.8 - 0.6 * math.exp(-0.3 * layer)
            za, zb = _mixer_ab(proj, b12, b3, ub, diff_lq1[j], diff_lk1[j], diff_lq2[j],
                               diff_lk2[j], diff_subln_g[j], lam_init)
            zs, w_out = [za, zb], ab_w_out
        else:
            zs, w_out = [_mixer_c(proj, c_sink[j], bc)], c_w_out
        x, proj = _proj(x, out_stage=(gate, zs, w_out, j),
                        final_g=final_g if last else None,
                        in_stage=None if last else in_stage(layer + 1))
    return x
```

```python
import functools
import math

import numpy as np
import jax
import jax.numpy as jnp
from jax import lax
from jax.experimental import pallas as pl
from jax.experimental.pallas import tpu as pltpu

D_MODEL = 1024
BATCH = 8
SEQ = 2048
DEPTH = 4
HEAD_DIM = 64
LANES = 128
A_HEADS = 8
A_PATTERNS = ((128, 1), (512, 4), (2048, 16))
A_HALF = 64
B_HEADS = 4
C_HEADS = 16
C_KV_HEADS = 4
C_HALF_WINDOW = 128
EVEN_IN = 4096
ODD_IN = 2560
EPS = 1e-6
NEG_INF = -1e30
LOG2E = math.log2(math.e)
Q_FOLD = HEAD_DIM ** -0.5 * LOG2E
ONES_ROWS = 16

F32 = jnp.float32
BF16 = jnp.bfloat16

ROW_TILE = 512
OUT_ONLY_ROW_TILE = 1024
COL_TILE = 512
W_CAST_ROWS = 128
BLK = 128
A_KEYS = BLK + 2 * A_HALF
C_KEYS = BLK + 2 * C_HALF_WINDOW
A_MERGE_ROWS = 256
C_GROUP = 2
C_NB = 1
B_TQ = 512
B_TK = 256
B_NB = 2
VMEM_LIMIT = 56 * 1024 * 1024


def _silu(t):
    return t * (1.0 / (1.0 + jnp.exp(-t)))


def _dot_nt(a, b):
    return lax.dot_general(a, b, (((1,), (1,)), ((), ())), preferred_element_type=F32)


def _alibi_slopes(n):
    return (2.0 ** (-8.0 * np.arange(1, n + 1, dtype=np.float32) / n)).astype(np.float32)


def _band_bias(slopes, spacing, half, tq, tk, offsets):
    i = np.arange(tq)[:, None]
    c = np.arange(tk)[None, :]
    out = np.empty((len(slopes), len(offsets), tq, tk), np.float32)
    for v, off in enumerate(offsets):
        rel = np.abs(c - (i + off))
        dist = (rel * spacing).astype(np.float32)
        for h, m in enumerate(slopes):
            out[h, v] = np.where(rel <= half, -m * dist * np.float32(LOG2E), np.float32(NEG_INF))
    return out


def _split_bf16(a):
    hi = a.astype(BF16)
    return hi, (a - hi.astype(F32)).astype(BF16)


def _mod_kernel(c_ref, w_ref, b_ref, o_ref):
    c_hi, c_lo = _split_bf16(_silu(c_ref[...]))
    w_hi, w_lo = _split_bf16(w_ref[0])
    dot = functools.partial(jnp.dot, preferred_element_type=F32)
    o_ref[0] = dot(c_hi, w_hi) + (dot(c_hi, w_lo) + dot(c_lo, w_hi)) + b_ref[0]


def _modulation(c, ada_w, ada_b):
    nblk = 3 * D_MODEL // D_MODEL
    return pl.pallas_call(
        _mod_kernel,
        grid=(DEPTH, nblk),
        in_specs=[
            pl.BlockSpec((BATCH, D_MODEL), lambda l, j: (0, 0)),
            pl.BlockSpec((1, D_MODEL, D_MODEL), lambda l, j: (l, 0, j)),
            pl.BlockSpec((1, 1, D_MODEL), lambda l, j: (l, 0, j)),
        ],
        out_specs=pl.BlockSpec((1, BATCH, D_MODEL), lambda l, j: (l, 0, j)),
        out_shape=jax.ShapeDtypeStruct((DEPTH, BATCH, 3 * D_MODEL), F32),
        name="adaln_mod",
    )(c, ada_w, ada_b.reshape(DEPTH, 1, 3 * D_MODEL))


def _rms(x, g):
    ms = jnp.mean(x * x, axis=-1, keepdims=True)
    return x * lax.rsqrt(ms + EPS) * g


def _proj_kernel(*refs, n_z, final, n_out, q_cols):
    refs = iter(refs)
    x_ref = next(refs)
    if n_z:
        gate_ref = next(refs)
        z_refs = [next(refs) for _ in range(n_z)]
        wo_ref = next(refs)
    if final:
        fg_ref = next(refs)
    if n_out:
        g_ref, sc_ref, sh_ref, wi_ref = (next(refs) for _ in range(4))
    if n_z:
        xo_ref = next(refs)
    if n_out:
        proj_ref = next(refs)
    if n_z:
        wob_ref = next(refs)
    if n_out:
        wib_ref, h_ref = next(refs), next(refs)

    @pl.when((pl.program_id(0) == 0) & (pl.program_id(1) == 0))
    def _cast_weights():
        for src_ref, dst_ref in ([(wo_ref, wob_ref)] if n_z else []) + (
                [(wi_ref, wib_ref)] if n_out else []):
            for r in range(0, D_MODEL, W_CAST_ROWS):
                dst_ref[r:r + W_CAST_ROWS, :] = src_ref[r:r + W_CAST_ROWS, :].astype(BF16)

    xn = x_ref[0]
    if n_z:
        kz = D_MODEL // n_z
        acc = jnp.dot(z_refs[0][0], wob_ref[0:kz, :], preferred_element_type=F32)
        for i in range(1, n_z):
            acc = acc + jnp.dot(z_refs[i][0], wob_ref[i * kz:(i + 1) * kz, :],
                                preferred_element_type=F32)
        xn = xn + gate_ref[0] * acc
        xo_ref[0] = _rms(xn, fg_ref[...]) if final else xn
    if n_out:
        h_ref[...] = (_rms(xn, g_ref[...]) * (1.0 + sc_ref[0]) + sh_ref[0]).astype(BF16)
        for j in range(n_out // COL_TILE):
            lo, hi = j * COL_TILE, (j + 1) * COL_TILE
            acc = jnp.dot(h_ref[...], wib_ref[:, lo:hi], preferred_element_type=F32)
            if any(q0 <= lo and hi <= q1 for q0, q1 in q_cols):
                acc = acc * Q_FOLD
            proj_ref[0, :, lo:hi] = acc.astype(BF16)


def _proj(x, out_stage=None, final_g=None, in_stage=None):
    tile = ROW_TILE if in_stage is not None else OUT_ONLY_ROW_TILE
    row = pl.BlockSpec((1, tile, D_MODEL), lambda b, i: (b, i, 0))
    per_batch = pl.BlockSpec((1, 1, D_MODEL), lambda b, i: (b, 0, 0))
    vec = pl.BlockSpec((1, D_MODEL), lambda b, i: (0, 0))
    def weight(stack, j):
        return pl.BlockSpec((None,) + stack.shape[1:], lambda b, i: (j, 0, 0),
                            pipeline_mode=pl.Buffered(1))
    in_specs, args, out_specs, out_shape, scratch = [row], [x], [], [], []
    n_z = n_out = 0
    q_cols = ()
    assert final_g is None or out_stage is not None
    if out_stage is not None:
        gate, zs, w_out, j_out = out_stage
        n_z = len(zs)
        in_specs += [per_batch]
        in_specs += [pl.BlockSpec((1, tile, D_MODEL // n_z), lambda b, i: (b, i, 0))] * n_z
        in_specs += [weight(w_out, j_out)]
        args += [gate.reshape(BATCH, 1, D_MODEL), *zs, w_out]
        out_specs += [row]
        out_shape += [jax.ShapeDtypeStruct((BATCH, SEQ, D_MODEL), F32)]
        scratch += [pltpu.VMEM((D_MODEL, D_MODEL), BF16)]
    if final_g is not None:
        in_specs += [vec]
        args += [final_g.reshape(1, D_MODEL)]
    if in_stage is not None:
        g, scale, shift, w_in, j_in, q_cols = in_stage
        n_out = w_in.shape[2]
        assert all(q0 % COL_TILE == 0 and q1 % COL_TILE == 0 for q0, q1 in q_cols)
        in_specs += [vec, per_batch, per_batch, weight(w_in, j_in)]
        args += [g.reshape(1, D_MODEL), scale.reshape(BATCH, 1, D_MODEL),
                 shift.reshape(BATCH, 1, D_MODEL), w_in]
        out_specs += [pl.BlockSpec((1, tile, n_out), lambda b, i: (b, i, 0))]
        out_shape += [jax.ShapeDtypeStruct((BATCH, SEQ, n_out), BF16)]
        scratch += [pltpu.VMEM((D_MODEL, n_out), BF16), pltpu.VMEM((tile, D_MODEL), BF16)]
    outs = pl.pallas_call(
        functools.partial(_proj_kernel, n_z=n_z, final=final_g is not None, n_out=n_out,
                          q_cols=q_cols),
        grid=(BATCH, SEQ // tile),
        in_specs=in_specs,
        out_specs=out_specs,
        out_shape=out_shape,
        scratch_shapes=scratch,
        compiler_params=pltpu.CompilerParams(
            dimension_semantics=("arbitrary", "arbitrary"), vmem_limit_bytes=VMEM_LIMIT),
        name="proj",
    )(*args)
    outs = list(outs)
    x_new = outs.pop(0) if out_stage is not None else None
    proj = outs.pop(0) if in_stage is not None else None
    return x_new, proj


def _lane_lo(rows):
    return lax.broadcasted_iota(jnp.int32, (rows, LANES), 1) < HEAD_DIM


def _softmax_block(qm, kw, vw, bias):
    s = _dot_nt(qm, kw) + bias
    m = jnp.max(s, axis=-1, keepdims=True)
    p = jnp.exp2(s - m)
    l = jnp.sum(p, axis=-1, keepdims=True)
    acc = jnp.dot(p.astype(BF16), vw, preferred_element_type=F32)
    return acc, m, l


def _interleave(*stages):
    live = list(stages)
    while live:
        for g in list(live):
            try:
                next(g)
            except StopIteration:
                live.remove(g)


def _stack_heads(qb, lo):
    zero = jnp.zeros_like(qb)
    return jnp.concatenate([jnp.where(lo, qb, zero), jnp.where(lo, zero, qb)], axis=0)


A_SCRATCH = (
    [pltpu.VMEM((SEQ, LANES), F32)] * 6
    + [pltpu.VMEM((SEQ, LANES), BF16)] * 6
    + [pltpu.VMEM((3, SEQ, LANES), F32)] * 3
)


def _a_stages(q_ref, k_ref, v_ref, g_ref, b12_ref, b3_ref, o_ref,
              qf_ref, kf_ref, vf_ref, q4_ref, k4_ref, v4_ref, qd_ref, kd_ref, vd_ref,
              qe_ref, ke_ref, ve_ref, acc_ref, m_ref, l_ref):
    lo = _lane_lo(BLK)

    def pair_block(qb, kw, vw, bias):
        acc, m, l = _softmax_block(_stack_heads(qb, lo), kw, vw, bias)
        return (jnp.where(lo, acc[:BLK], acc[BLK:]), jnp.where(lo, m[:BLK], m[BLK:]),
                jnp.where(lo, l[:BLK], l[BLK:]))

    n_blk = SEQ // BLK
    dil1 = A_PATTERNS[1][1]
    seg1 = SEQ // dil1
    per1 = seg1 // BLK
    dil2 = A_PATTERNS[2][1]
    sub = dil2 // dil1
    assert sub * dil1 == dil2
    seg2 = SEQ // dil2
    streams = ((qf_ref, q4_ref, qd_ref, qe_ref), (kf_ref, k4_ref, kd_ref, ke_ref),
               (vf_ref, v4_ref, vd_ref, ve_ref))

    def prologue():
        chunk = 512
        for c in range(SEQ // chunk):
            rows = slice(c * chunk, (c + 1) * chunk)
            qf_ref[rows, :] = q_ref[0, rows, :].astype(F32)
            kf_ref[rows, :] = k_ref[0, rows, :].astype(F32)
            vf_ref[rows, :] = v_ref[0, rows, :].astype(F32)
            yield
        for r in range(dil1):
            dst = slice(r * seg1, (r + 1) * seg1)
            for src_ref, mid_ref, dst_ref, _ in streams:
                t = src_ref[pl.ds(r, seg1, stride=dil1), :]
                mid_ref[dst, :] = t
                dst_ref[dst, :] = t.astype(BF16)
            yield
        for r in range(dil1):
            for r2 in range(sub):
                res = dil1 * r2 + r
                dst = slice(res * seg2, (res + 1) * seg2)
                for _, mid_ref, _, dst_ref in streams:
                    dst_ref[dst, :] = (
                        mid_ref[pl.ds(r * seg1 + r2, seg2, stride=sub), :].astype(BF16))
            yield

    def aligned(x, m):
        return x if isinstance(x, int) else pl.multiple_of(x, m)

    def clamp(x, hi):
        return min(max(x, 0), hi) if isinstance(x, int) else jnp.clip(x, 0, hi)

    def variant(blk, last):
        if isinstance(blk, int):
            return 0 if blk == 0 else (2 if blk == last else 1)
        return jnp.where(blk == 0, 0, jnp.where(blk == last, 2, 1))

    def p0_body(j):
        r0 = aligned(j * BLK, BLK)
        ws = aligned(clamp(j * BLK - A_HALF, SEQ - A_KEYS), A_HALF)
        var = variant(j, n_blk - 1)
        acc, m, l = pair_block(q_ref[0, pl.ds(r0, BLK), :],
                               k_ref[0, pl.ds(ws, A_KEYS), :], v_ref[0, pl.ds(ws, A_KEYS), :],
                               b12_ref[0, 0, var])
        acc_ref[0, pl.ds(r0, BLK), :] = acc
        m_ref[0, pl.ds(r0, BLK), :] = m
        l_ref[0, pl.ds(r0, BLK), :] = l

    def p1_body(j):
        r = j // per1
        blk = j % per1
        r0 = aligned(j * BLK, BLK)
        ws = aligned(r * seg1 + clamp(blk * BLK - A_HALF, seg1 - A_KEYS), A_HALF)
        var = variant(blk, per1 - 1)
        acc, m, l = pair_block(qd_ref[pl.ds(r0, BLK), :],
                               kd_ref[pl.ds(ws, A_KEYS), :], vd_ref[pl.ds(ws, A_KEYS), :],
                               b12_ref[0, 1, var])
        dst = pl.ds(blk * BLK * dil1 + r, BLK, stride=dil1)
        acc_ref[1, dst, :] = acc
        m_ref[1, dst, :] = m
        l_ref[1, dst, :] = l

    def p2_body(j):
        r0 = aligned(j * BLK, BLK)
        acc, m, l = pair_block(qe_ref[pl.ds(r0, BLK), :],
                               ke_ref[pl.ds(r0, BLK), :], ve_ref[pl.ds(r0, BLK), :],
                               b3_ref[0])
        dst = pl.ds(j, BLK, stride=dil2)
        acc_ref[2, dst, :] = acc
        m_ref[2, dst, :] = m
        l_ref[2, dst, :] = l

    bodies = (p0_body, p1_body, p2_body)

    def blocks(work):
        for pattern, j in work:
            bodies[pattern](j)
            yield

    def merge(c):
        rows = pl.ds(aligned(c * A_MERGE_ROWS, A_MERGE_ROWS), A_MERGE_ROWS)
        m0, m1, m2 = m_ref[0, rows, :], m_ref[1, rows, :], m_ref[2, rows, :]
        mx = jnp.maximum(jnp.maximum(m0, m1), m2)
        w0, w1, w2 = jnp.exp2(m0 - mx), jnp.exp2(m1 - mx), jnp.exp2(m2 - mx)
        num = w0 * acc_ref[0, rows, :] + w1 * acc_ref[1, rows, :] + w2 * acc_ref[2, rows, :]
        den = w0 * l_ref[0, rows, :] + w1 * l_ref[1, rows, :] + w2 * l_ref[2, rows, :]
        y = num / den
        o_ref[0, rows, :] = (y * _silu(g_ref[0, rows, :].astype(F32))).astype(BF16)

    return prologue, blocks, merge


B_SCRATCH = (
    [pltpu.VMEM((B_NB, LANES + ONES_ROWS, SEQ), BF16)]
    + [pltpu.VMEM((2, SEQ, B_TQ), F32)] * 2
    + [pltpu.VMEM((1, 2 * B_TQ), F32)] * 2
    + [pltpu.VMEM((LANES + ONES_ROWS, 2 * B_TQ), F32)] * 2
)


def _b_stages(q_ref, k_ref, v_ref, g_ref, u_ref, lq1_ref, lk1_ref, lq2_ref, lk2_ref,
              sg_ref, o_ref, vt_ref, sa_ref, sb_ref, ma_ref, mb_ref, acca_ref, accb_ref,
              *, lam_init):
    lo = _lane_lo(B_TQ)
    n_t = SEQ // B_TK
    lam = (jnp.exp(jnp.sum(lq1_ref[...] * lk1_ref[...], axis=-1, keepdims=True))
           - jnp.exp(jnp.sum(lq2_ref[...] * lk2_ref[...], axis=-1, keepdims=True)) + lam_init)

    eye = jnp.where(lax.broadcasted_iota(jnp.int32, (LANES, LANES), 0)
                    == lax.broadcasted_iota(jnp.int32, (LANES, LANES), 1), 1.0, 0.0).astype(BF16)
    ones_row = jnp.where(lax.broadcasted_iota(jnp.int32, (ONES_ROWS, B_TK), 0) == 0,
                         1.0, 0.0).astype(BF16)
    for bi in range(B_NB):
        for t in range(n_t):
            rows = slice(t * B_TK, (t + 1) * B_TK)
            vt_ref[bi, :LANES, rows] = _dot_nt(eye, v_ref[bi, rows, :]).astype(BF16)
            vt_ref[bi, LANES:, rows] = ones_row

    n_q = SEQ // B_TQ

    def locate(i):
        if isinstance(i, int):
            return i // n_q, (i % n_q) * B_TQ
        return i // n_q, pl.multiple_of((i % n_q) * B_TQ, B_TQ)

    def scores(i, s_ref, m_ref):
        bi, r0 = locate(i)
        qst = _stack_heads(q_ref[bi, pl.ds(r0, B_TQ), :], lo)
        m = None
        for t in range(n_t):
            rows = slice(t * B_TK, (t + 1) * B_TK)
            off = pl.multiple_of(t * B_TK + (SEQ - B_TQ) - r0, LANES)
            ub = u_ref[0, pl.ds(off, B_TK), :]
            st = _dot_nt(k_ref[bi, rows, :], qst)
            s1 = st[:, :B_TQ] + ub
            s2 = st[:, B_TQ:] + ub
            s_ref[0, rows, :] = s1
            s_ref[1, rows, :] = s2
            f = jnp.concatenate([jnp.max(s1, axis=0, keepdims=True),
                                 jnp.max(s2, axis=0, keepdims=True)], axis=1)
            m = f if m is None else jnp.maximum(m, f)
            yield
        m_ref[...] = m

    def softmax_pv(i, s_ref, m_ref, acc_ref):
        bi, _ = locate(i)
        m = m_ref[...]
        acct = None
        for t in range(n_t):
            rows = slice(t * B_TK, (t + 1) * B_TK)
            p = jnp.concatenate([jnp.exp2(s_ref[0, rows, :] - m[:, :B_TQ]),
                                 jnp.exp2(s_ref[1, rows, :] - m[:, B_TQ:])], axis=1).astype(BF16)
            d = jnp.dot(vt_ref[bi, :, rows], p, preferred_element_type=F32)
            acct = d if acct is None else acct + d
            yield
        acc_ref[...] = acct

    def epilogue(i, acc_ref):
        bi, r0 = locate(i)
        ot = acc_ref[:LANES, :] * (1.0 / acc_ref[LANES:LANES + 1, :])
        outt = ot[:, :B_TQ] - lam * ot[:, B_TQ:]
        ms = jnp.mean(outt * outt, axis=0, keepdims=True)
        yt = outt * lax.rsqrt(ms + EPS) * sg_ref[...] * (1.0 - lam_init)
        yield
        gate = _silu(g_ref[bi, pl.ds(r0, B_TQ), :].astype(F32))
        o_ref[bi, pl.ds(r0, B_TQ), :] = (yt.T * gate).astype(BF16)

    return scores, softmax_pv, epilogue, (sa_ref, ma_ref, acca_ref), (sb_ref, mb_ref, accb_ref)


def _b_kernel(*refs, lam_init):
    scores, softmax_pv, epilogue, (sa, ma, acca), (sb, mb, accb) = _b_stages(
        *refs, lam_init=lam_init)
    n = B_NB * (SEQ // B_TQ)
    _interleave(scores(0, sa, ma))
    _interleave(scores(1, sb, mb), softmax_pv(0, sa, ma, acca))

    def body(j, carry):
        i = 2 * j
        _interleave(scores(i + 2, sa, ma), softmax_pv(i + 1, sb, mb, accb), epilogue(i, acca))
        _interleave(scores(i + 3, sb, mb), softmax_pv(i + 2, sa, ma, acca), epilogue(i + 1, accb))
        return carry

    lax.fori_loop(0, n // 2 - 1, body, 0)
    _interleave(softmax_pv(n - 1, sb, mb, accb), epilogue(n - 2, acca))
    _interleave(epilogue(n - 1, accb))


def _a_kernel(*refs):
    prologue, blocks, merge = _a_stages(*refs)
    _interleave(prologue())
    _interleave(blocks([(p, j) for p in range(len(A_PATTERNS)) for j in range(SEQ // BLK)]))

    def merge_body(c, carry):
        merge(c)
        return carry

    lax.fori_loop(0, SEQ // A_MERGE_ROWS, merge_body, 0)


def _mixer_ab(proj, bias12, bias3, u, lq1, lk1, lq2, lk2, subln_g, lam_init):
    pairs = A_HEADS // 2
    a_w = A_HEADS * HEAD_DIM // LANES
    b_q = 3 * a_w
    g0 = (3 * A_HEADS * HEAD_DIM + 3 * B_HEADS * 2 * HEAD_DIM) // LANES
    blk = lambda off: pl.BlockSpec((1, SEQ, LANES), lambda i, b: (b, 0, off + i))
    vec = lambda n: pl.BlockSpec((1, n), lambda i, b: (0, 0))
    out = pl.BlockSpec((1, SEQ, LANES), lambda i, b: (b, 0, i))
    params = pltpu.CompilerParams(
        dimension_semantics=("arbitrary", "arbitrary"), vmem_limit_bytes=VMEM_LIMIT)
    za = pl.pallas_call(
        _a_kernel,
        grid=(pairs, BATCH),
        in_specs=[
            blk(0), blk(a_w), blk(2 * a_w), blk(g0),
            pl.BlockSpec((1, 2, 3, 2 * BLK, A_KEYS), lambda i, b: (i, 0, 0, 0, 0)),
            pl.BlockSpec((1, 2 * BLK, BLK), lambda i, b: (i, 0, 0)),
        ],
        out_specs=out,
        out_shape=jax.ShapeDtypeStruct((BATCH, SEQ, A_HEADS * HEAD_DIM), BF16),
        scratch_shapes=A_SCRATCH,
        compiler_params=params,
        name="mixer_a_dilated",
    )(proj, proj, proj, proj, bias12, bias3)
    blk = lambda off: pl.BlockSpec((B_NB, SEQ, LANES), lambda i, b: (b, 0, off + i))
    zb = pl.pallas_call(
        functools.partial(_b_kernel, lam_init=lam_init),
        grid=(B_HEADS, BATCH // B_NB),
        in_specs=[
            blk(b_q), blk(b_q + B_HEADS), blk(b_q + 2 * B_HEADS), blk(g0 + a_w),
            pl.BlockSpec((1, 2 * SEQ - B_TQ, B_TQ), lambda i, b: (i, 0, 0)),
            vec(HEAD_DIM), vec(HEAD_DIM), vec(HEAD_DIM), vec(HEAD_DIM),
            pl.BlockSpec((2 * HEAD_DIM, 1), lambda i, b: (0, 0)),
        ],
        out_specs=blk(0),
        out_shape=jax.ShapeDtypeStruct((BATCH, SEQ, B_HEADS * 2 * HEAD_DIM), BF16),
        scratch_shapes=B_SCRATCH,
        compiler_params=params,
        name="mixer_b_differential",
    )(proj, proj, proj, proj, u, lq1.reshape(1, -1), lk1.reshape(1, -1),
      lq2.reshape(1, -1), lk2.reshape(1, -1), subln_g.reshape(-1, 1))
    return za, zb


def _c_kernel(sink_ref, q_ref, k_ref, v_ref, g_ref, bias_ref, o_ref, ka_ref, vt_ref, *bufs):
    s_refs, m_refs = bufs[:2 * C_GROUP], bufs[2 * C_GROUP:]
    grp = pl.program_id(0)
    half = grp % 2
    lo = _lane_lo(BLK)
    zero = jnp.zeros((BLK, LANES), BF16)
    row = lax.broadcasted_iota(jnp.int32, (LANES, LANES), 0)
    col = lax.broadcasted_iota(jnp.int32, (LANES, LANES), 1)

    chunk = 512
    ones_row = jnp.where(lax.broadcasted_iota(jnp.int32, (ONES_ROWS, chunk), 0) == 0,
                         1.0, 0.0).astype(BF16)
    for a in range(2):
        shift = jnp.where(half == a, 0, HEAD_DIM)
        perm = jnp.where(col == (row + shift) % LANES, 1.0, 0.0).astype(BF16)
        for bi in range(C_NB):
            for c in range(SEQ // chunk):
                rows = slice(c * chunk, (c + 1) * chunk)
                ka_ref[bi, a, rows, :] = jnp.dot(k_ref[bi, rows, :], perm,
                                                 preferred_element_type=F32).astype(BF16)
                vt_ref[bi, a, :LANES, rows] = _dot_nt(perm, v_ref[bi, rows, :]).astype(BF16)
                vt_ref[bi, a, LANES:, rows] = ones_row

    n_blk = SEQ // BLK
    rep = C_HEADS // C_KV_HEADS
    left = lax.broadcasted_iota(jnp.int32, (1, 2 * BLK), 1) < BLK
    upper = lax.broadcasted_iota(jnp.int32, (LANES, BLK), 0) < HEAD_DIM

    def locate(n):
        bi, nl = n // n_blk, n % n_blk
        r0 = pl.multiple_of(nl * BLK, BLK)
        ws = pl.multiple_of(jnp.clip(nl * BLK - C_HALF_WINDOW, 0, SEQ - C_KEYS), BLK)
        return bi, nl, r0, ws

    def scores(n, s_ref, m_ref):
        bi, nl, r0, ws = locate(n)
        var = jnp.where(nl == 0, 0, jnp.where(nl == n_blk - 1, 2, 1))
        for a in range(2):
            qa = [q_ref[bi, pl.ds(r0, BLK), c * LANES:(c + 1) * LANES] for c in range(rep // 2)]
            qm = jnp.concatenate(
                [jnp.where(lo, q, zero) if a == 0 else jnp.where(lo, zero, q) for q in qa], axis=0)
            st = _dot_nt(ka_ref[bi, a, pl.ds(ws, C_KEYS), :], qm) + bias_ref[0, a, var]
            s_ref[a] = st
            m_ref[a] = jnp.max(st, axis=0, keepdims=True)
            yield

    def finish(n, s_ref, m_ref):
        bi, _, r0, ws = locate(n)
        yts = []
        for a in range(2):
            m = m_ref[a]
            p = jnp.exp2(s_ref[a] - m).astype(BF16)
            acct = jnp.dot(vt_ref[bi, a, :, pl.ds(ws, C_KEYS)], p, preferred_element_type=F32)
            l = acct[LANES:LANES + 1]
            sk = jnp.where(left, sink_ref[grp * rep + a], sink_ref[grp * rep + a + 2]) * LOG2E
            mx = jnp.maximum(m, sk)
            e = jnp.exp2(m - mx)
            yts.append(acct[:LANES] * (e / (l * e + jnp.exp2(sk - mx))))
            yield
        for c in range(rep // 2):
            cols = slice(c * LANES, (c + 1) * LANES)
            y = jnp.where(upper, yts[0][:, cols], yts[1][:, cols]).T
            gate = _silu(g_ref[bi, pl.ds(r0, BLK), cols].astype(F32))
            o_ref[bi, pl.ds(r0, BLK), cols] = (y * gate).astype(BF16)

    bufs = [(s_refs[i], m_refs[i]) for i in range(2 * C_GROUP)]
    set_a, set_b = bufs[:C_GROUP], bufs[C_GROUP:]
    n_grp = C_NB * n_blk // C_GROUP

    def scores_of(g, bset):
        return [scores(g * C_GROUP + i, *bset[i]) for i in range(C_GROUP)]

    def finish_of(g, bset):
        return [finish(g * C_GROUP + i, *bset[i]) for i in range(C_GROUP)]

    _interleave(*scores_of(0, set_a))

    def paired(fins, scs):
        for f, s in zip(fins, scs):
            _interleave(f, s)

    def body(j, carry):
        paired(finish_of(2 * j, set_a), scores_of(2 * j + 1, set_b))
        paired(finish_of(2 * j + 1, set_b), scores_of(2 * j + 2, set_a))
        return carry

    lax.fori_loop(0, n_grp // 2 - 1, body, 0, unroll=True)
    paired(finish_of(n_grp - 2, set_a), scores_of(n_grp - 1, set_b))
    _interleave(*finish_of(n_grp - 1, set_b))


def _mixer_c(proj, sink, bias):
    width = C_HEADS // C_KV_HEADS * HEAD_DIM
    kb = C_HEADS * HEAD_DIM // LANES
    vb = kb + C_KV_HEADS * HEAD_DIM // LANES
    gb = (C_HEADS + 2 * C_KV_HEADS) * HEAD_DIM // width
    return pl.pallas_call(
        _c_kernel,
        grid=(C_KV_HEADS, BATCH // C_NB),
        in_specs=[
            pl.BlockSpec(memory_space=pltpu.SMEM),
            pl.BlockSpec((C_NB, SEQ, width), lambda g, b: (b, 0, g)),
            pl.BlockSpec((C_NB, SEQ, LANES), lambda g, b: (b, 0, kb + g // 2)),
            pl.BlockSpec((C_NB, SEQ, LANES), lambda g, b: (b, 0, vb + g // 2)),
            pl.BlockSpec((C_NB, SEQ, width), lambda g, b: (b, 0, gb + g)),
            pl.BlockSpec((1, 2, 3, C_KEYS, 2 * BLK), lambda g, b: (g, 0, 0, 0, 0)),
        ],
        out_specs=pl.BlockSpec((C_NB, SEQ, width), lambda g, b: (b, 0, g)),
        out_shape=jax.ShapeDtypeStruct((BATCH, SEQ, C_HEADS * HEAD_DIM), BF16),
        scratch_shapes=[pltpu.VMEM((C_NB, 2, SEQ, LANES), BF16),
                        pltpu.VMEM((C_NB, 2, LANES + ONES_ROWS, SEQ), BF16)]
        + [pltpu.VMEM((2, C_KEYS, 2 * BLK), F32)] * (2 * C_GROUP)
        + [pltpu.VMEM((2, 1, 2 * BLK), F32)] * (2 * C_GROUP),
        compiler_params=pltpu.CompilerParams(
            dimension_semantics=("arbitrary", "arbitrary"), vmem_limit_bytes=VMEM_LIMIT),
        name="mixer_c_windowed",
    )(sink, proj, proj, proj, proj, bias)


@functools.lru_cache(maxsize=None)
def _bias_tables():
    def stack_pairs(t, first, second):
        return np.concatenate([t[first], t[second]], axis=2)

    sa = _alibi_slopes(A_HEADS)
    offs = (0, A_HALF, 2 * A_HALF)
    ev, od = slice(0, None, 2), slice(1, None, 2)
    b12 = np.stack(
        [stack_pairs(_band_bias(sa, dil, A_HALF, BLK, A_KEYS, offs), ev, od)
         for _, dil in A_PATTERNS[:2]], axis=1)
    b3 = stack_pairs(_band_bias(sa, A_PATTERNS[2][1], A_HALF, BLK, BLK, (0,)), ev, od)[:, 0]
    sc = _alibi_slopes(C_HEADS)
    bc_heads = _band_bias(sc, 1, C_HALF_WINDOW, BLK, C_KEYS,
                          (0, C_HALF_WINDOW, 2 * C_HALF_WINDOW))
    rep = C_HEADS // C_KV_HEADS
    bc = np.stack([stack_pairs(bc_heads, slice(a, None, rep), slice(a + 2, None, rep))
                   for a in range(2)], axis=1)
    bc = np.ascontiguousarray(np.swapaxes(bc, -1, -2))
    sb = _alibi_slopes(B_HEADS)
    il = np.arange(B_TQ)[None, :]
    uu = np.arange(2 * SEQ - B_TQ)[:, None]
    dist = np.abs(il - uu + (SEQ - B_TQ)).astype(np.float32)
    ub = -sb[:, None, None] * dist[None] * np.float32(LOG2E)
    return b12, b3, bc, ub.astype(np.float32)


def kernel(x, c, ada_w, ada_b, norm_g, ab_w_in, ab_w_out, diff_lq1, diff_lk1, diff_lq2,
           diff_lk2, diff_subln_g, c_w_in, c_w_out, c_sink, final_g):
    b12, b3, bc, ub = (jnp.asarray(t) for t in _bias_tables())
    mod = _modulation(c, ada_w, ada_b)
    a_w = A_HEADS * HEAD_DIM
    q_even = ((0, a_w), (3 * a_w, 3 * a_w + B_HEADS * 2 * HEAD_DIM))
    q_odd = ((0, C_HEADS * HEAD_DIM),)

    def in_stage(layer):
        shift = mod[layer, :, :D_MODEL]
        scale = mod[layer, :, D_MODEL:2 * D_MODEL]
        w, q_cols = (ab_w_in, q_even) if layer % 2 == 0 else (c_w_in, q_odd)
        return norm_g[layer], scale, shift, w, layer // 2, q_cols

    _, proj = _proj(x, in_stage=in_stage(0))
    for layer in range(DEPTH):
        gate = mod[layer, :, 2 * D_MODEL:]
        j = layer // 2
        last = layer == DEPTH - 1
        if layer % 2 == 0:
            lam_init = 0.8 - 0.6 * math.exp(-0.3 * layer)
            za, zb = _mixer_ab(proj, b12, b3, ub, diff_lq1[j], diff_lk1[j], diff_lq2[j],
                               diff_lk2[j], diff_subln_g[j], lam_init)
            zs, w_out = [za, zb], ab_w_out
        else:
            zs, w_out = [_mixer_c(proj, c_sink[j], bc)], c_w_out
        x, proj = _proj(x, out_stage=(gate, zs, w_out, j),
                        final_g=final_g if last else None,
                        in_stage=None if last else in_stage(layer + 1))
    return x
```

```python
import functools
import math

import numpy as np
import jax
import jax.numpy as jnp
from jax import lax
from jax.experimental import pallas as pl
from jax.experimental.pallas import tpu as pltpu

D_MODEL = 1024
BATCH = 8
SEQ = 2048
DEPTH = 4
HEAD_DIM = 64
LANES = 128
A_HEADS = 8
A_PATTERNS = ((128, 1), (512, 4), (2048, 16))
A_HALF = 64
B_HEADS = 4
C_HEADS = 16
C_KV_HEADS = 4
C_HALF_WINDOW = 128
EVEN_IN = 4096
ODD_IN = 2560
EPS = 1e-6
NEG_INF = -1e30
LOG2E = math.log2(math.e)
Q_FOLD = HEAD_DIM ** -0.5 * LOG2E
ONES_ROWS = 16

F32 = jnp.float32
BF16 = jnp.bfloat16

ROW_TILE = 512
OUT_ONLY_ROW_TILE = 1024
COL_TILE = 512
W_CAST_ROWS = 128
BLK = 128
A_KEYS = BLK + 2 * A_HALF
C_KEYS = BLK + 2 * C_HALF_WINDOW
A_MERGE_ROWS = 256
C_GROUP = 2
C_NB = 2
B_TQ = 512
B_TK = 256
B_NB = 2
VMEM_LIMIT = 56 * 1024 * 1024


def _silu(t):
    return t * (1.0 / (1.0 + jnp.exp(-t)))


def _dot_nt(a, b):
    return lax.dot_general(a, b, (((1,), (1,)), ((), ())), preferred_element_type=F32)


def _alibi_slopes(n):
    return (2.0 ** (-8.0 * np.arange(1, n + 1, dtype=np.float32) / n)).astype(np.float32)


def _band_bias(slopes, spacing, half, tq, tk, offsets):
    i = np.arange(tq)[:, None]
    c = np.arange(tk)[None, :]
    out = np.empty((len(slopes), len(offsets), tq, tk), np.float32)
    for v, off in enumerate(offsets):
        rel = np.abs(c - (i + off))
        dist = (rel * spacing).astype(np.float32)
        for h, m in enumerate(slopes):
            out[h, v] = np.where(rel <= half, -m * dist * np.float32(LOG2E), np.float32(NEG_INF))
    return out


def _split_bf16(a):
    hi = a.astype(BF16)
    return hi, (a - hi.astype(F32)).astype(BF16)


def _mod_kernel(c_ref, w_ref, b_ref, o_ref):
    c_hi, c_lo = _split_bf16(_silu(c_ref[...]))
    w_hi, w_lo = _split_bf16(w_ref[0])
    dot = functools.partial(jnp.dot, preferred_element_type=F32)
    o_ref[0] = dot(c_hi, w_hi) + (dot(c_hi, w_lo) + dot(c_lo, w_hi)) + b_ref[0]


def _modulation(c, ada_w, ada_b):
    nblk = 3 * D_MODEL // D_MODEL
    return pl.pallas_call(
        _mod_kernel,
        grid=(DEPTH, nblk),
        in_specs=[
            pl.BlockSpec((BATCH, D_MODEL), lambda l, j: (0, 0)),
            pl.BlockSpec((1, D_MODEL, D_MODEL), lambda l, j: (l, 0, j)),
            pl.BlockSpec((1, 1, D_MODEL), lambda l, j: (l, 0, j)),
        ],
        out_specs=pl.BlockSpec((1, BATCH, D_MODEL), lambda l, j: (l, 0, j)),
        out_shape=jax.ShapeDtypeStruct((DEPTH, BATCH, 3 * D_MODEL), F32),
        name="adaln_mod",
    )(c, ada_w, ada_b.reshape(DEPTH, 1, 3 * D_MODEL))


def _rms(x, g):
    ms = jnp.mean(x * x, axis=-1, keepdims=True)
    return x * lax.rsqrt(ms + EPS) * g


def _proj_kernel(*refs, n_z, final, n_out, q_cols):
    refs = iter(refs)
    x_ref = next(refs)
    if n_z:
        gate_ref = next(refs)
        z_refs = [next(refs) for _ in range(n_z)]
        wo_ref = next(refs)
    if final:
        fg_ref = next(refs)
    if n_out:
        g_ref, sc_ref, sh_ref, wi_ref = (next(refs) for _ in range(4))
    if n_z:
        xo_ref = next(refs)
    if n_out:
        proj_ref = next(refs)
    if n_z:
        wob_ref = next(refs)
    if n_out:
        wib_ref, h_ref = next(refs), next(refs)

    @pl.when((pl.program_id(0) == 0) & (pl.program_id(1) == 0))
    def _cast_weights():
        for src_ref, dst_ref in ([(wo_ref, wob_ref)] if n_z else []) + (
                [(wi_ref, wib_ref)] if n_out else []):
            for r in range(0, D_MODEL, W_CAST_ROWS):
                dst_ref[r:r + W_CAST_ROWS, :] = src_ref[r:r + W_CAST_ROWS, :].astype(BF16)

    xn = x_ref[0]
    if n_z:
        kz = D_MODEL // n_z
        acc = jnp.dot(z_refs[0][0], wob_ref[0:kz, :], preferred_element_type=F32)
        for i in range(1, n_z):
            acc = acc + jnp.dot(z_refs[i][0], wob_ref[i * kz:(i + 1) * kz, :],
                                preferred_element_type=F32)
        xn = xn + gate_ref[0] * acc
        xo_ref[0] = _rms(xn, fg_ref[...]) if final else xn
    if n_out:
        h_ref[...] = (_rms(xn, g_ref[...]) * (1.0 + sc_ref[0]) + sh_ref[0]).astype(BF16)
        for j in range(n_out // COL_TILE):
            lo, hi = j * COL_TILE, (j + 1) * COL_TILE
            acc = jnp.dot(h_ref[...], wib_ref[:, lo:hi], preferred_element_type=F32)
            if any(q0 <= lo and hi <= q1 for q0, q1 in q_cols):
                acc = acc * Q_FOLD
            proj_ref[0, :, lo:hi] = acc.astype(BF16)


def _proj(x, out_stage=None, final_g=None, in_stage=None):
    tile = ROW_TILE if in_stage is not None else OUT_ONLY_ROW_TILE
    row = pl.BlockSpec((1, tile, D_MODEL), lambda b, i: (b, i, 0))
    per_batch = pl.BlockSpec((1, 1, D_MODEL), lambda b, i: (b, 0, 0))
    vec = pl.BlockSpec((1, D_MODEL), lambda b, i: (0, 0))
    def weight(stack, j):
        return pl.BlockSpec((None,) + stack.shape[1:], lambda b, i: (j, 0, 0),
                            pipeline_mode=pl.Buffered(1))
    in_specs, args, out_specs, out_shape, scratch = [row], [x], [], [], []
    n_z = n_out = 0
    q_cols = ()
    assert final_g is None or out_stage is not None
    if out_stage is not None:
        gate, zs, w_out, j_out = out_stage
        n_z = len(zs)
        in_specs += [per_batch]
        in_specs += [pl.BlockSpec((1, tile, D_MODEL // n_z), lambda b, i: (b, i, 0))] * n_z
        in_specs += [weight(w_out, j_out)]
        args += [gate.reshape(BATCH, 1, D_MODEL), *zs, w_out]
        out_specs += [row]
        out_shape += [jax.ShapeDtypeStruct((BATCH, SEQ, D_MODEL), F32)]
        scratch += [pltpu.VMEM((D_MODEL, D_MODEL), BF16)]
    if final_g is not None:
        in_specs += [vec]
        args += [final_g.reshape(1, D_MODEL)]
    if in_stage is not None:
        g, scale, shift, w_in, j_in, q_cols = in_stage
        n_out = w_in.shape[2]
        assert all(q0 % COL_TILE == 0 and q1 % COL_TILE == 0 for q0, q1 in q_cols)
        in_specs += [vec, per_batch, per_batch, weight(w_in, j_in)]
        args += [g.reshape(1, D_MODEL), scale.reshape(BATCH, 1, D_MODEL),
                 shift.reshape(BATCH, 1, D_MODEL), w_in]
        out_specs += [pl.BlockSpec((1, tile, n_out), lambda b, i: (b, i, 0))]
        out_shape += [jax.ShapeDtypeStruct((BATCH, SEQ, n_out), BF16)]
        scratch += [pltpu.VMEM((D_MODEL, n_out), BF16), pltpu.VMEM((tile, D_MODEL), BF16)]
    outs = pl.pallas_call(
        functools.partial(_proj_kernel, n_z=n_z, final=final_g is not None, n_out=n_out,
                          q_cols=q_cols),
        grid=(BATCH, SEQ // tile),
        in_specs=in_specs,
        out_specs=out_specs,
        out_shape=out_shape,
        scratch_shapes=scratch,
        compiler_params=pltpu.CompilerParams(
            dimension_semantics=("arbitrary", "arbitrary"), vmem_limit_bytes=VMEM_LIMIT),
        name="proj",
    )(*args)
    outs = list(outs)
    x_new = outs.pop(0) if out_stage is not None else None
    proj = outs.pop(0) if in_stage is not None else None
    return x_new, proj


def _lane_lo(rows):
    return lax.broadcasted_iota(jnp.int32, (rows, LANES), 1) < HEAD_DIM


def _softmax_block(qm, kw, vw, bias):
    s = _dot_nt(qm, kw) + bias
    m = jnp.max(s, axis=-1, keepdims=True)
    p = jnp.exp2(s - m)
    l = jnp.sum(p, axis=-1, keepdims=True)
    acc = jnp.dot(p.astype(BF16), vw, preferred_element_type=F32)
    return acc, m, l


def _interleave(*stages):
    live = list(stages)
    while live:
        for g in list(live):
            try:
                next(g)
            except StopIteration:
                live.remove(g)


def _stack_heads(qb, lo):
    zero = jnp.zeros_like(qb)
    return jnp.concatenate([jnp.where(lo, qb, zero), jnp.where(lo, zero, qb)], axis=0)


A_SCRATCH = (
    [pltpu.VMEM((SEQ, LANES), F32)] * 6
    + [pltpu.VMEM((SEQ, LANES), BF16)] * 6
    + [pltpu.VMEM((3, SEQ, LANES), F32)] * 3
)


def _a_stages(q_ref, k_ref, v_ref, g_ref, b12_ref, b3_ref, o_ref,
              qf_ref, kf_ref, vf_ref, q4_ref, k4_ref, v4_ref, qd_ref, kd_ref, vd_ref,
              qe_ref, ke_ref, ve_ref, acc_ref, m_ref, l_ref):
    lo = _lane_lo(BLK)

    def pair_block(qb, kw, vw, bias):
        acc, m, l = _softmax_block(_stack_heads(qb, lo), kw, vw, bias)
        return (jnp.where(lo, acc[:BLK], acc[BLK:]), jnp.where(lo, m[:BLK], m[BLK:]),
                jnp.where(lo, l[:BLK], l[BLK:]))

    n_blk = SEQ // BLK
    dil1 = A_PATTERNS[1][1]
    seg1 = SEQ // dil1
    per1 = seg1 // BLK
    dil2 = A_PATTERNS[2][1]
    sub = dil2 // dil1
    assert sub * dil1 == dil2
    seg2 = SEQ // dil2
    streams = ((qf_ref, q4_ref, qd_ref, qe_ref), (kf_ref, k4_ref, kd_ref, ke_ref),
               (vf_ref, v4_ref, vd_ref, ve_ref))

    def prologue():
        chunk = 512
        for c in range(SEQ // chunk):
            rows = slice(c * chunk, (c + 1) * chunk)
            qf_ref[rows, :] = q_ref[0, rows, :].astype(F32)
            kf_ref[rows, :] = k_ref[0, rows, :].astype(F32)
            vf_ref[rows, :] = v_ref[0, rows, :].astype(F32)
            yield
        for r in range(dil1):
            dst = slice(r * seg1, (r + 1) * seg1)
            for src_ref, mid_ref, dst_ref, _ in streams:
                t = src_ref[pl.ds(r, seg1, stride=dil1), :]
                mid_ref[dst, :] = t
                dst_ref[dst, :] = t.astype(BF16)
            yield
        for r in range(dil1):
            for r2 in range(sub):
                res = dil1 * r2 + r
                dst = slice(res * seg2, (res + 1) * seg2)
                for _, mid_ref, _, dst_ref in streams:
                    dst_ref[dst, :] = (
                        mid_ref[pl.ds(r * seg1 + r2, seg2, stride=sub), :].astype(BF16))
            yield

    def aligned(x, m):
        return x if isinstance(x, int) else pl.multiple_of(x, m)

    def clamp(x, hi):
        return min(max(x, 0), hi) if isinstance(x, int) else jnp.clip(x, 0, hi)

    def variant(blk, last):
        if isinstance(blk, int):
            return 0 if blk == 0 else (2 if blk == last else 1)
        return jnp.where(blk == 0, 0, jnp.where(blk == last, 2, 1))

    def p0_body(j):
        r0 = aligned(j * BLK, BLK)
        ws = aligned(clamp(j * BLK - A_HALF, SEQ - A_KEYS), A_HALF)
        var = variant(j, n_blk - 1)
        acc, m, l = pair_block(q_ref[0, pl.ds(r0, BLK), :],
                               k_ref[0, pl.ds(ws, A_KEYS), :], v_ref[0, pl.ds(ws, A_KEYS), :],
                               b12_ref[0, 0, var])
        acc_ref[0, pl.ds(r0, BLK), :] = acc
        m_ref[0, pl.ds(r0, BLK), :] = m
        l_ref[0, pl.ds(r0, BLK), :] = l

    def p1_body(j):
        r = j // per1
        blk = j % per1
        r0 = aligned(j * BLK, BLK)
        ws = aligned(r * seg1 + clamp(blk * BLK - A_HALF, seg1 - A_KEYS), A_HALF)
        var = variant(blk, per1 - 1)
        acc, m, l = pair_block(qd_ref[pl.ds(r0, BLK), :],
                               kd_ref[pl.ds(ws, A_KEYS), :], vd_ref[pl.ds(ws, A_KEYS), :],
                               b12_ref[0, 1, var])
        dst = pl.ds(blk * BLK * dil1 + r, BLK, stride=dil1)
        acc_ref[1, dst, :] = acc
        m_ref[1, dst, :] = m
        l_ref[1, dst, :] = l

    def p2_body(j):
        r0 = aligned(j * BLK, BLK)
        acc, m, l = pair_block(qe_ref[pl.ds(r0, BLK), :],
                               ke_ref[pl.ds(r0, BLK), :], ve_ref[pl.ds(r0, BLK), :],
                               b3_ref[0])
        dst = pl.ds(j, BLK, stride=dil2)
        acc_ref[2, dst, :] = acc
        m_ref[2, dst, :] = m
        l_ref[2, dst, :] = l

    bodies = (p0_body, p1_body, p2_body)

    def blocks(work):
        for pattern, j in work:
            bodies[pattern](j)
            yield

    def merge(c):
        rows = pl.ds(aligned(c * A_MERGE_ROWS, A_MERGE_ROWS), A_MERGE_ROWS)
        m0, m1, m2 = m_ref[0, rows, :], m_ref[1, rows, :], m_ref[2, rows, :]
        mx = jnp.maximum(jnp.maximum(m0, m1), m2)
        w0, w1, w2 = jnp.exp2(m0 - mx), jnp.exp2(m1 - mx), jnp.exp2(m2 - mx)
        num = w0 * acc_ref[0, rows, :] + w1 * acc_ref[1, rows, :] + w2 * acc_ref[2, rows, :]
        den = w0 * l_ref[0, rows, :] + w1 * l_ref[1, rows, :] + w2 * l_ref[2, rows, :]
        y = num / den
        o_ref[0, rows, :] = (y * _silu(g_ref[0, rows, :].astype(F32))).astype(BF16)

    return prologue, blocks, merge


B_SCRATCH = (
    [pltpu.VMEM((B_NB, LANES + ONES_ROWS, SEQ), BF16)]
    + [pltpu.VMEM((2, SEQ, B_TQ), F32)] * 2
    + [pltpu.VMEM((1, 2 * B_TQ), F32)] * 2
    + [pltpu.VMEM((LANES + ONES_ROWS, 2 * B_TQ), F32)] * 2
)


def _b_stages(q_ref, k_ref, v_ref, g_ref, u_ref, lq1_ref, lk1_ref, lq2_ref, lk2_ref,
              sg_ref, o_ref, vt_ref, sa_ref, sb_ref, ma_ref, mb_ref, acca_ref, accb_ref,
              *, lam_init):
    lo = _lane_lo(B_TQ)
    n_t = SEQ // B_TK
    lam = (jnp.exp(jnp.sum(lq1_ref[...] * lk1_ref[...], axis=-1, keepdims=True))
           - jnp.exp(jnp.sum(lq2_ref[...] * lk2_ref[...], axis=-1, keepdims=True)) + lam_init)

    eye = jnp.where(lax.broadcasted_iota(jnp.int32, (LANES, LANES), 0)
                    == lax.broadcasted_iota(jnp.int32, (LANES, LANES), 1), 1.0, 0.0).astype(BF16)
    ones_row = jnp.where(lax.broadcasted_iota(jnp.int32, (ONES_ROWS, B_TK), 0) == 0,
                         1.0, 0.0).astype(BF16)
    for bi in range(B_NB):
        for t in range(n_t):
            rows = slice(t * B_TK, (t + 1) * B_TK)
            vt_ref[bi, :LANES, rows] = _dot_nt(eye, v_ref[bi, rows, :]).astype(BF16)
            vt_ref[bi, LANES:, rows] = ones_row

    n_q = SEQ // B_TQ

    def locate(i):
        if isinstance(i, int):
            return i // n_q, (i % n_q) * B_TQ
        return i // n_q, pl.multiple_of((i % n_q) * B_TQ, B_TQ)

    def scores(i, s_ref, m_ref):
        bi, r0 = locate(i)
        qst = _stack_heads(q_ref[bi, pl.ds(r0, B_TQ), :], lo)
        m = None
        for t in range(n_t):
            rows = slice(t * B_TK, (t + 1) * B_TK)
            off = pl.multiple_of(t * B_TK + (SEQ - B_TQ) - r0, LANES)
            ub = u_ref[0, pl.ds(off, B_TK), :]
            st = _dot_nt(k_ref[bi, rows, :], qst)
            s1 = st[:, :B_TQ] + ub
            s2 = st[:, B_TQ:] + ub
            s_ref[0, rows, :] = s1
            s_ref[1, rows, :] = s2
            f = jnp.concatenate([jnp.max(s1, axis=0, keepdims=True),
                                 jnp.max(s2, axis=0, keepdims=True)], axis=1)
            m = f if m is None else jnp.maximum(m, f)
            yield
        m_ref[...] = m

    def softmax_pv(i, s_ref, m_ref, acc_ref):
        bi, _ = locate(i)
        m = m_ref[...]
        acct = None
        for t in range(n_t):
            rows = slice(t * B_TK, (t + 1) * B_TK)
            p = jnp.concatenate([jnp.exp2(s_ref[0, rows, :] - m[:, :B_TQ]),
                                 jnp.exp2(s_ref[1, rows, :] - m[:, B_TQ:])], axis=1).astype(BF16)
            d = jnp.dot(vt_ref[bi, :, rows], p, preferred_element_type=F32)
            acct = d if acct is None else acct + d
            yield
        acc_ref[...] = acct

    def epilogue(i, acc_ref):
        bi, r0 = locate(i)
        ot = acc_ref[:LANES, :] * (1.0 / acc_ref[LANES:LANES + 1, :])
        outt = ot[:, :B_TQ] - lam * ot[:, B_TQ:]
        ms = jnp.mean(outt * outt, axis=0, keepdims=True)
        yt = outt * lax.rsqrt(ms + EPS) * sg_ref[...] * (1.0 - lam_init)
        yield
        gate = _silu(g_ref[bi, pl.ds(r0, B_TQ), :].astype(F32))
        o_ref[bi, pl.ds(r0, B_TQ), :] = (yt.T * gate).astype(BF16)

    return scores, softmax_pv, epilogue, (sa_ref, ma_ref, acca_ref), (sb_ref, mb_ref, accb_ref)


def _b_kernel(*refs, lam_init):
    scores, softmax_pv, epilogue, (sa, ma, acca), (sb, mb, accb) = _b_stages(
        *refs, lam_init=lam_init)
    n = B_NB * (SEQ // B_TQ)
    _interleave(scores(0, sa, ma))
    _interleave(scores(1, sb, mb), softmax_pv(0, sa, ma, acca))

    def body(j, carry):
        i = 2 * j
        _interleave(scores(i + 2, sa, ma), softmax_pv(i + 1, sb, mb, accb), epilogue(i, acca))
        _interleave(scores(i + 3, sb, mb), softmax_pv(i + 2, sa, ma, acca), epilogue(i + 1, accb))
        return carry

    lax.fori_loop(0, n // 2 - 1, body, 0)
    _interleave(softmax_pv(n - 1, sb, mb, accb), epilogue(n - 2, acca))
    _interleave(epilogue(n - 1, accb))


def _a_kernel(*refs):
    prologue, blocks, merge = _a_stages(*refs)
    _interleave(prologue())
    _interleave(blocks([(p, j) for p in range(len(A_PATTERNS)) for j in range(SEQ // BLK)]))

    def merge_body(c, carry):
        merge(c)
        return carry

    lax.fori_loop(0, SEQ // A_MERGE_ROWS, merge_body, 0)


def _mixer_ab(proj, bias12, bias3, u, lq1, lk1, lq2, lk2, subln_g, lam_init):
    pairs = A_HEADS // 2
    a_w = A_HEADS * HEAD_DIM // LANES
    b_q = 3 * a_w
    g0 = (3 * A_HEADS * HEAD_DIM + 3 * B_HEADS * 2 * HEAD_DIM) // LANES
    blk = lambda off: pl.BlockSpec((1, SEQ, LANES), lambda i, b: (b, 0, off + i))
    vec = lambda n: pl.BlockSpec((1, n), lambda i, b: (0, 0))
    out = pl.BlockSpec((1, SEQ, LANES), lambda i, b: (b, 0, i))
    params = pltpu.CompilerParams(
        dimension_semantics=("arbitrary", "arbitrary"), vmem_limit_bytes=VMEM_LIMIT)
    za = pl.pallas_call(
        _a_kernel,
        grid=(pairs, BATCH),
        in_specs=[
            blk(0), blk(a_w), blk(2 * a_w), blk(g0),
            pl.BlockSpec((1, 2, 3, 2 * BLK, A_KEYS), lambda i, b: (i, 0, 0, 0, 0)),
            pl.BlockSpec((1, 2 * BLK, BLK), lambda i, b: (i, 0, 0)),
        ],
        out_specs=out,
        out_shape=jax.ShapeDtypeStruct((BATCH, SEQ, A_HEADS * HEAD_DIM), BF16),
        scratch_shapes=A_SCRATCH,
        compiler_params=params,
        name="mixer_a_dilated",
    )(proj, proj, proj, proj, bias12, bias3)
    blk = lambda off: pl.BlockSpec((B_NB, SEQ, LANES), lambda i, b: (b, 0, off + i))
    zb = pl.pallas_call(
        functools.partial(_b_kernel, lam_init=lam_init),
        grid=(B_HEADS, BATCH // B_NB),
        in_specs=[
            blk(b_q), blk(b_q + B_HEADS), blk(b_q + 2 * B_HEADS), blk(g0 + a_w),
            pl.BlockSpec((1, 2 * SEQ - B_TQ, B_TQ), lambda i, b: (i, 0, 0)),
            vec(HEAD_DIM), vec(HEAD_DIM), vec(HEAD_DIM), vec(HEAD_DIM),
            pl.BlockSpec((2 * HEAD_DIM, 1), lambda i, b: (0, 0)),
        ],
        out_specs=blk(0),
        out_shape=jax.ShapeDtypeStruct((BATCH, SEQ, B_HEADS * 2 * HEAD_DIM), BF16),
        scratch_shapes=B_SCRATCH,
        compiler_params=params,
        name="mixer_b_differential",
    )(proj, proj, proj, proj, u, lq1.reshape(1, -1), lk1.reshape(1, -1),
      lq2.reshape(1, -1), lk2.reshape(1, -1), subln_g.reshape(-1, 1))
    return za, zb


def _c_kernel(sink_ref, q_ref, k_ref, v_ref, g_ref, bias_ref, o_ref, ka_ref, vt_ref, *bufs):
    s_refs, m_refs = bufs[:2 * C_GROUP], bufs[2 * C_GROUP:]
    grp = pl.program_id(0)
    half = grp % 2
    lo = _lane_lo(BLK)
    zero = jnp.zeros((BLK, LANES), BF16)
    row = lax.broadcasted_iota(jnp.int32, (LANES, LANES), 0)
    col = lax.broadcasted_iota(jnp.int32, (LANES, LANES), 1)

    chunk = 512
    ones_row = jnp.where(lax.broadcasted_iota(jnp.int32, (ONES_ROWS, chunk), 0) == 0,
                         1.0, 0.0).astype(BF16)
    for a in range(2):
        shift = jnp.where(half == a, 0, HEAD_DIM)
        perm = jnp.where(col == (row + shift) % LANES, 1.0, 0.0).astype(BF16)
        for bi in range(C_NB):
            for c in range(SEQ // chunk):
                rows = slice(c * chunk, (c + 1) * chunk)
                ka_ref[bi, a, rows, :] = jnp.dot(k_ref[bi, rows, :], perm,
                                                 preferred_element_type=F32).astype(BF16)
                vt_ref[bi, a, :LANES, rows] = _dot_nt(perm, v_ref[bi, rows, :]).astype(BF16)
                vt_ref[bi, a, LANES:, rows] = ones_row

    n_blk = SEQ // BLK
    rep = C_HEADS // C_KV_HEADS
    left = lax.broadcasted_iota(jnp.int32, (1, 2 * BLK), 1) < BLK
    upper = lax.broadcasted_iota(jnp.int32, (LANES, BLK), 0) < HEAD_DIM

    def locate(n):
        bi, nl = n // n_blk, n % n_blk
        r0 = pl.multiple_of(nl * BLK, BLK)
        ws = pl.multiple_of(jnp.clip(nl * BLK - C_HALF_WINDOW, 0, SEQ - C_KEYS), BLK)
        return bi, nl, r0, ws

    def scores(n, s_ref, m_ref):
        bi, nl, r0, ws = locate(n)
        var = jnp.where(nl == 0, 0, jnp.where(nl == n_blk - 1, 2, 1))
        for a in range(2):
            qa = [q_ref[bi, pl.ds(r0, BLK), c * LANES:(c + 1) * LANES] for c in range(rep // 2)]
            qm = jnp.concatenate(
                [jnp.where(lo, q, zero) if a == 0 else jnp.where(lo, zero, q) for q in qa], axis=0)
            st = _dot_nt(ka_ref[bi, a, pl.ds(ws, C_KEYS), :], qm) + bias_ref[0, a, var]
            s_ref[a] = st
            m_ref[a] = jnp.max(st, axis=0, keepdims=True)
            yield

    def finish(n, s_ref, m_ref):
        bi, _, r0, ws = locate(n)
        yts = []
        for a in range(2):
            m = m_ref[a]
            p = jnp.exp2(s_ref[a] - m).astype(BF16)
            acct = jnp.dot(vt_ref[bi, a, :, pl.ds(ws, C_KEYS)], p, preferred_element_type=F32)
            l = acct[LANES:LANES + 1]
            sk = jnp.where(left, sink_ref[grp * rep + a], sink_ref[grp * rep + a + 2]) * LOG2E
            mx = jnp.maximum(m, sk)
            e = jnp.exp2(m - mx)
            yts.append(acct[:LANES] * (e / (l * e + jnp.exp2(sk - mx))))
            yield
        for c in range(rep // 2):
            cols = slice(c * LANES, (c + 1) * LANES)
            y = jnp.where(upper, yts[0][:, cols], yts[1][:, cols]).T
            gate = _silu(g_ref[bi, pl.ds(r0, BLK), cols].astype(F32))
            o_ref[bi, pl.ds(r0, BLK), cols] = (y * gate).astype(BF16)

    bufs = [(s_refs[i], m_refs[i]) for i in range(2 * C_GROUP)]
    set_a, set_b = bufs[:C_GROUP], bufs[C_GROUP:]
    n_grp = C_NB * n_blk // C_GROUP

    def scores_of(g, bset):
        return [scores(g * C_GROUP + i, *bset[i]) for i in range(C_GROUP)]

    def finish_of(g, bset):
        return [finish(g * C_GROUP + i, *bset[i]) for i in range(C_GROUP)]

    _interleave(*scores_of(0, set_a))

    def paired(fins, scs):
        for f, s in zip(fins, scs):
            _interleave(f, s)

    def body(j, carry):
        paired(finish_of(2 * j, set_a), scores_of(2 * j + 1, set_b))
        paired(finish_of(2 * j + 1, set_b), scores_of(2 * j + 2, set_a))
        return carry

    lax.fori_loop(0, n_grp // 2 - 1, body, 0, unroll=True)
    paired(finish_of(n_grp - 2, set_a), scores_of(n_grp - 1, set_b))
    _interleave(*finish_of(n_grp - 1, set_b))


def _mixer_c(proj, sink, bias):
    width = C_HEADS // C_KV_HEADS * HEAD_DIM
    kb = C_HEADS * HEAD_DIM // LANES
    vb = kb + C_KV_HEADS * HEAD_DIM // LANES
    gb = (C_HEADS + 2 * C_KV_HEADS) * HEAD_DIM // width
    return pl.pallas_call(
        _c_kernel,
        grid=(C_KV_HEADS, BATCH // C_NB),
        in_specs=[
            pl.BlockSpec(memory_space=pltpu.SMEM),
            pl.BlockSpec((C_NB, SEQ, width), lambda g, b: (b, 0, g)),
            pl.BlockSpec((C_NB, SEQ, LANES), lambda g, b: (b, 0, kb + g // 2)),
            pl.BlockSpec((C_NB, SEQ, LANES), lambda g, b: (b, 0, vb + g // 2)),
            pl.BlockSpec((C_NB, SEQ, width), lambda g, b: (b, 0, gb + g)),
            pl.BlockSpec((1, 2, 3, C_KEYS, 2 * BLK), lambda g, b: (g, 0, 0, 0, 0)),
        ],
        out_specs=pl.BlockSpec((C_NB, SEQ, width), lambda g, b: (b, 0, g)),
        out_shape=jax.ShapeDtypeStruct((BATCH, SEQ, C_HEADS * HEAD_DIM), BF16),
        scratch_shapes=[pltpu.VMEM((C_NB, 2, SEQ, LANES), BF16),
                        pltpu.VMEM((C_NB, 2, LANES + ONES_ROWS, SEQ), BF16)]
        + [pltpu.VMEM((2, C_KEYS, 2 * BLK), F32)] * (2 * C_GROUP)
        + [pltpu.VMEM((2, 1, 2 * BLK), F32)] * (2 * C_GROUP),
        compiler_params=pltpu.CompilerParams(
            dimension_semantics=("arbitrary", "arbitrary"), vmem_limit_bytes=VMEM_LIMIT),
        name="mixer_c_windowed",
    )(sink, proj, proj, proj, proj, bias)


@functools.lru_cache(maxsize=None)
def _bias_tables():
    def stack_pairs(t, first, second):
        return np.concatenate([t[first], t[second]], axis=2)

    sa = _alibi_slopes(A_HEADS)
    offs = (0, A_HALF, 2 * A_HALF)
    ev, od = slice(0, None, 2), slice(1, None, 2)
    b12 = np.stack(
        [stack_pairs(_band_bias(sa, dil, A_HALF, BLK, A_KEYS, offs), ev, od)
         for _, dil in A_PATTERNS[:2]], axis=1)
    b3 = stack_pairs(_band_bias(sa, A_PATTERNS[2][1], A_HALF, BLK, BLK, (0,)), ev, od)[:, 0]
    sc = _alibi_slopes(C_HEADS)
    bc_heads = _band_bias(sc, 1, C_HALF_WINDOW, BLK, C_KEYS,
                          (0, C_HALF_WINDOW, 2 * C_HALF_WINDOW))
    rep = C_HEADS // C_KV_HEADS
    bc = np.stack([stack_pairs(bc_heads, slice(a, None, rep), slice(a + 2, None, rep))
                   for a in range(2)], axis=1)
    bc = np.ascontiguousarray(np.swapaxes(bc, -1, -2))
    sb = _alibi_slopes(B_HEADS)
    il = np.arange(B_TQ)[None, :]
    uu = np.arange(2 * SEQ - B_TQ)[:, None]
    dist = np.abs(il - uu + (SEQ - B_TQ)).astype(np.float32)
    ub = -sb[:, None, None] * dist[None] * np.float32(LOG2E)
    return b12, b3, bc, ub.astype(np.float32)


def kernel(x, c, ada_w, ada_b, norm_g, ab_w_in, ab_w_out, diff_lq1, diff_lk1, diff_lq2,
           diff_lk2, diff_subln_g, c_w_in, c_w_out, c_sink, final_g):
    b12, b3, bc, ub = (jnp.asarray(t) for t in _bias_tables())
    mod = _modulation(c, ada_w, ada_b)
    a_w = A_HEADS * HEAD_DIM
    q_even = ((0, a_w), (3 * a_w, 3 * a_w + B_HEADS * 2 * HEAD_DIM))
    q_odd = ((0, C_HEADS * HEAD_DIM),)

    def in_stage(layer):
        shift = mod[layer, :, :D_MODEL]
        scale = mod[layer, :, D_MODEL:2 * D_MODEL]
        w, q_cols = (ab_w_in, q_even) if layer % 2 == 0 else (c_w_in, q_odd)
        return norm_g[layer], scale, shift, w, layer // 2, q_cols

    _, proj = _proj(x, in_stage=in_stage(0))
    for layer in range(DEPTH):
        gate = mod[layer, :, 2 * D_MODEL:]
        j = layer // 2
        last = layer == DEPTH - 1
        if layer % 2 == 0:
            lam_init = 0.8 - 0.6 * math.exp(-0.3 * layer)
            za, zb = _mixer_ab(proj, b12, b3, ub, diff_lq1[j], diff_lk1[j], diff_lq2[j],
                               diff_lk2[j], diff_subln_g[j], lam_init)
            zs, w_out = [za, zb], ab_w_out
        else:
            zs, w_out = [_mixer_c(proj, c_sink[j], bc)], c_w_out
        x, proj = _proj(x, out_stage=(gate, zs, w_out, j),
                        final_g=final_g if last else None,
                        in_stage=None if last else in_stage(layer + 1))
    return x
```
